```python
import jax, jax.numpy as jnp
from jax import lax
import numpy as np

D_MODEL = 2048
BATCH = 4
SEQ = 2048
DEPTH = 1

HEAD_DIM = 128
N_HEADS_FOX = 8
N_HEADS_DIL = 8
DIL_PATTERNS = ((128, 1), (512, 4), (2048, 16))
Q_BLOCK = 128
D_FF = 5632
CONV_WIDTH = 3
EPS = 1e-6
NEG_INF = -1e30
W_FOX = N_HEADS_FOX * HEAD_DIM
W_DIL = N_HEADS_DIL * HEAD_DIM
IN_SPLITS = (W_FOX, W_FOX, W_FOX, N_HEADS_FOX, W_DIL, W_DIL, W_DIL, D_MODEL, D_MODEL)
IN_COLS = sum(IN_SPLITS)

kernel_name = "hybrid_fox_dilated_convffn"


def rms_norm(x, g):
    xf = x.astype(jnp.float32)
    y = xf * lax.rsqrt(jnp.mean(xf * xf, axis=-1, keepdims=True) + EPS)
    return (y * g.astype(jnp.float32)).astype(x.dtype)


def alibi_slopes(n_heads):
    return jnp.asarray(2.0 ** (-8.0 * np.arange(1, n_heads + 1) / n_heads), dtype=jnp.float32)


def split_heads(a, n_heads):
    b, s, _ = a.shape
    return a.reshape(b, s, n_heads, HEAD_DIM).transpose(0, 2, 1, 3)


def merge_heads(a):
    b, h, s, d = a.shape
    return a.transpose(0, 2, 1, 3).reshape(b, s, h * d)


def fox_attention(q, k, v, log_f):
    b, h, s, d = q.shape
    nb = s // Q_BLOCK
    scale = 1.0 / np.sqrt(d)
    c = jnp.cumsum(log_f, axis=-1)
    qb = q.reshape(b, h, nb, Q_BLOCK, d).transpose(2, 0, 1, 3, 4)
    cb = c.reshape(b, h, nb, Q_BLOCK).transpose(2, 0, 1, 3)
    kpos = jnp.arange(s)

    def block(args):
        n, qn, cn = args
        sc = jnp.einsum('bhqd,bhkd->bhqk', qn, k) * scale
        sc = sc + cn[..., :, None] - c[:, :, None, :]
        qpos = n * Q_BLOCK + jnp.arange(Q_BLOCK)
        sc = jnp.where(kpos[None, :] <= qpos[:, None], sc, NEG_INF)
        p = jax.nn.softmax(sc, axis=-1)
        return jnp.einsum('bhqk,bhkd->bhqd', p, v)

    o = lax.map(block, (jnp.arange(nb), qb, cb))
    return o.transpose(1, 2, 0, 3, 4).reshape(b, h, s, d)


def dilated_pattern(q, k, v, slopes, window, dil):
    b, h, s, d = q.shape
    steps = window // dil
    L = s // dil
    nb = -(-L // steps)
    Lp = nb * steps
    scale = 1.0 / np.sqrt(d)

    def to_sub(a):
        return a.reshape(b, h, L, dil, d).transpose(0, 1, 3, 2, 4)

    qs = jnp.pad(to_sub(q), ((0, 0), (0, 0), (0, 0), (0, Lp - L), (0, 0)))
    qb = qs.reshape(b, h, dil, nb, steps, d)

    def band(a):
        ap = jnp.pad(to_sub(a), ((0, 0), (0, 0), (0, 0), (steps, Lp - L), (0, 0)))
        prev = ap[:, :, :, :Lp].reshape(b, h, dil, nb, steps, d)
        cur = ap[:, :, :, steps:].reshape(b, h, dil, nb, steps, d)
        return jnp.concatenate([prev, cur], axis=4)

    kb, vb = band(k), band(v)
    sc = jnp.einsum('bhrnqd,bhrnkd->bhrnqk', qb, kb) * scale
    i = jnp.arange(steps)[:, None]
    j = jnp.arange(2 * steps)[None, :]
    dist = i + steps - j
    ksub = jnp.arange(nb)[:, None, None] * steps + j[None] - steps
    valid = (dist >= 0) & (dist <= steps) & (ksub >= 0)
    penalty = slopes[None, :, None, None, None, None] * (dil * dist).astype(jnp.float32)
    sc = jnp.where(valid, sc - penalty, NEG_INF)
    m = jnp.max(sc, axis=-1, keepdims=True)
    e = jnp.exp(sc - m)
    den = jnp.sum(e, axis=-1)
    o = jnp.einsum('bhrnqk,bhrnkd->bhrnqd', e, vb) / den[..., None]
    lse = m[..., 0] + jnp.log(den)

    def from_sub(a, tail):
        a = a.reshape((b, h, dil, Lp) + tail)[:, :, :, :L]
        perm = (0, 1, 3, 2) + tuple(range(4, 4 + len(tail)))
        return a.transpose(perm).reshape((b, h, s) + tail)

    return from_sub(o, (d,)), from_sub(lse, ())


def dilated_attention(q, k, v, slopes):
    outs, lses = [], []
    for window, dil in DIL_PATTERNS:
        o, l = dilated_pattern(q, k, v, slopes, window, dil)
        outs.append(o)
        lses.append(l)
    o = jnp.stack(outs, axis=0)
    w = jax.nn.softmax(jnp.stack(lses, axis=0), axis=0)
    return jnp.sum(w[..., None] * o, axis=0)


def qk_norm(a, g):
    af = a.astype(jnp.float32)
    af = af * lax.rsqrt(jnp.mean(af * af, axis=-1, keepdims=True) + EPS)
    return af * g.astype(jnp.float32)[None, :, None, :]


def setup_inputs(seed: int = 0) -> dict:
    key = jax.random.key(seed)
    ks = jax.random.split(key, 20)
    f32 = jnp.float32

    def nrm(k, shape, scale):
        return jax.random.normal(k, shape, f32) * scale

    return {
        "x": jax.random.normal(ks[0], (BATCH, SEQ, D_MODEL), f32),
        "g_attn": 1.0 + nrm(ks[1], (DEPTH, D_MODEL), 0.02),
        "w_in": nrm(ks[2], (DEPTH, D_MODEL, IN_COLS), D_MODEL ** -0.5),
        "b_forget": 3.0 + nrm(ks[3], (DEPTH, N_HEADS_FOX), 0.5),
        "g_q_fox": 1.0 + nrm(ks[4], (DEPTH, N_HEADS_FOX, HEAD_DIM), 0.02),
        "g_k_fox": 1.0 + nrm(ks[5], (DEPTH, N_HEADS_FOX, HEAD_DIM), 0.02),
        "g_q_dil": 1.0 + nrm(ks[6], (DEPTH, N_HEADS_DIL, HEAD_DIM), 0.02),
        "g_k_dil": 1.0 + nrm(ks[7], (DEPTH, N_HEADS_DIL, HEAD_DIM), 0.02),
        "w_br_fox": nrm(ks[8], (DEPTH, W_FOX, D_MODEL), W_FOX ** -0.5),
        "w_br_dil": nrm(ks[9], (DEPTH, W_DIL, D_MODEL), W_DIL ** -0.5),
        "w_out": nrm(ks[10], (DEPTH, D_MODEL, D_MODEL), D_MODEL ** -0.5),
        "g_ffn": 1.0 + nrm(ks[11], (DEPTH, D_MODEL), 0.02),
        "w_up": nrm(ks[12], (DEPTH, D_MODEL, 2 * D_FF), D_MODEL ** -0.5),
        "w_conv": nrm(ks[13], (DEPTH, CONV_WIDTH, 2 * D_FF), CONV_WIDTH ** -0.5),
        "b_conv": nrm(ks[14], (DEPTH, 2 * D_FF), 0.02),
        "w_down": nrm(ks[15], (DEPTH, D_FF, D_MODEL), D_FF ** -0.5),
    }


def reference(x, g_attn, w_in, b_forget, g_q_fox, g_k_fox, g_q_dil, g_k_dil,
              w_br_fox, w_br_dil, w_out, g_ffn, w_up, w_conv, b_conv, w_down):
    b, s, _ = x.shape
    offs = np.cumsum(IN_SPLITS)[:-1].tolist()
    slopes = alibi_slopes(N_HEADS_DIL)
    for l in range(DEPTH):
        h = rms_norm(x, g_attn[l])
        proj = h @ w_in[l]
        qa, ka, va, fa, qb, kb, vb, ga, gb = jnp.split(proj, offs, axis=-1)
        q_a = qk_norm(split_heads(qa, N_HEADS_FOX), g_q_fox[l])
        k_a = qk_norm(split_heads(ka, N_HEADS_FOX), g_k_fox[l])
        v_a = split_heads(va, N_HEADS_FOX).astype(jnp.float32)
        log_f = jax.nn.log_sigmoid((fa + b_forget[l]).astype(jnp.float32)).transpose(0, 2, 1)
        o_a = merge_heads(fox_attention(q_a, k_a, v_a, log_f)).astype(x.dtype)
        q_b = qk_norm(split_heads(qb, N_HEADS_DIL), g_q_dil[l])
        k_b = qk_norm(split_heads(kb, N_HEADS_DIL), g_k_dil[l])
        v_b = split_heads(vb, N_HEADS_DIL).astype(jnp.float32)
        o_b = merge_heads(dilated_attention(q_b, k_b, v_b, slopes)).astype(x.dtype)
        merged = jax.nn.sigmoid(ga) * (o_a @ w_br_fox[l]) + jax.nn.sigmoid(gb) * (o_b @ w_br_dil[l])
        x = x + merged @ w_out[l]
        h = rms_norm(x, g_ffn[l])
        u = h @ w_up[l]
        up = jnp.pad(u, ((0, 0), (CONV_WIDTH - 1, 0), (0, 0)))
        wc = w_conv[l]
        conv = sum(wc[t] * up[:, t:t + s] for t in range(CONV_WIDTH)) + b_conv[l]
        gate, val = jnp.split(conv, 2, axis=-1)
        x = x + (jax.nn.silu(gate) * val) @ w_down[l]
    return x
```

```python
import functools

import numpy as np
import jax
import jax.numpy as jnp
from jax import lax
from jax.experimental import pallas as pl
from jax.experimental.pallas import tpu as pltpu

HEAD_DIM = 128
N_HEADS_FOX = 8
N_HEADS_DIL = 8
DIL_PATTERNS = ((128, 1), (512, 4), (2048, 16))
CONV_WIDTH = 3
EPS = 1e-6
NEG_INF = -1e30

F32 = jnp.float32
BF16 = jnp.bfloat16

LANES = 128
BF16_ROWS = 16
VMEM_LIMIT = 56 * 1024 * 1024

NT_DIMS = (((1,), (1,)), ((), ()))


def _params(sem):
    return pltpu.CompilerParams(dimension_semantics=sem, vmem_limit_bytes=VMEM_LIMIT)


def _in_proj_kernel(x_ref, g_ref, w_ref, wf_ref, gqk_ref, qkv_ref, gate_ref, f_ref, h_scr,
                    *, n_qkv_tiles, tiles_per_group, heads_per_tile):
    j = pl.program_id(1)

    @pl.when(j == 0)
    def _():
        x = x_ref[...]
        ms = jnp.mean(x * x, axis=-1, keepdims=True)
        h = (x * lax.rsqrt(ms + EPS) * g_ref[...]).astype(BF16)
        h_scr[...] = h
        f_ref[...] = jnp.dot(h, wf_ref[...], preferred_element_type=F32)

    acc = jnp.dot(h_scr[...], w_ref[...], preferred_element_type=F32)
    is_qkv = j < n_qkv_tiles
    is_qk = jnp.logical_and(is_qkv, (j % (3 * tiles_per_group)) < 2 * tiles_per_group)

    @pl.when(is_qk)
    def _():
        for hh in range(heads_per_tile):
            sl = slice(hh * HEAD_DIM, (hh + 1) * HEAD_DIM)
            a = acc[:, sl]
            ms = jnp.mean(a * a, axis=-1, keepdims=True)
            qkv_ref[:, sl] = (a * lax.rsqrt(ms + EPS) * gqk_ref[:, sl]).astype(BF16)

    @pl.when(jnp.logical_and(is_qkv, jnp.logical_not(is_qk)))
    def _():
        qkv_ref[...] = acc.astype(BF16)

    @pl.when(jnp.logical_not(is_qkv))
    def _():
        gate_ref[...] = jax.nn.sigmoid(acc)


def _in_proj(x2, g_attn, w_main, w_f, g_qk, *, n_qkv, n_gate):
    m, d = x2.shape
    tm, tn = 1024, 512
    n_qkv_tiles = n_qkv // tn
    n_tiles = (n_qkv + n_gate) // tn
    kern = functools.partial(
        _in_proj_kernel, n_qkv_tiles=n_qkv_tiles,
        tiles_per_group=(N_HEADS_FOX * HEAD_DIM) // tn, heads_per_tile=tn // HEAD_DIM)
    return pl.pallas_call(
        kern,
        grid=(m // tm, n_tiles),
        in_specs=[
            pl.BlockSpec((tm, d), lambda i, j: (i, 0)),
            pl.BlockSpec((1, d), lambda i, j: (0, 0)),
            pl.BlockSpec((d, tn), lambda i, j: (0, j)),
            pl.BlockSpec((d, LANES), lambda i, j: (0, 0)),
            pl.BlockSpec((1, tn), lambda i, j: (0, jnp.minimum(j, n_qkv_tiles - 1))),
        ],
        out_specs=[
            pl.BlockSpec((tm, tn), lambda i, j: (i, jnp.minimum(j, n_qkv_tiles - 1))),
            pl.BlockSpec((tm, tn), lambda i, j: (i, jnp.maximum(j - n_qkv_tiles, 0))),
            pl.BlockSpec((tm, LANES), lambda i, j: (i, 0)),
        ],
        out_shape=[
            jax.ShapeDtypeStruct((m, n_qkv), BF16),
            jax.ShapeDtypeStruct((m, n_gate), F32),
            jax.ShapeDtypeStruct((m, LANES), F32),
        ],
        scratch_shapes=[pltpu.VMEM((tm, d), BF16)],
        compiler_params=_params(("arbitrary", "arbitrary")),
        name="in_proj",
    )(x2, g_attn, w_main, w_f, g_qk)


def _forget_kernel(f_ref, b_ref, ccol_ref, crow_ref, *, n_heads):
    z = f_ref[0] + b_ref[...]
    c = jax.nn.log_sigmoid(z)
    s = c.shape[0]
    row = lax.broadcasted_iota(jnp.int32, c.shape, 0)
    k = 1
    while k < s:
        c = c + jnp.where(row >= k, pltpu.roll(c, k, axis=0), 0.0)
        k *= 2
    ccol_ref[0] = c
    crow_ref[0] = c.T[:n_heads, :]


def _forget(f_pre, b_pad, *, n_heads):
    b, s, _ = f_pre.shape
    return pl.pallas_call(
        functools.partial(_forget_kernel, n_heads=n_heads),
        grid=(b,),
        in_specs=[
            pl.BlockSpec((1, s, LANES), lambda i: (i, 0, 0)),
            pl.BlockSpec((1, LANES), lambda i: (0, 0)),
        ],
        out_specs=[
            pl.BlockSpec((1, s, LANES), lambda i: (i, 0, 0)),
            pl.BlockSpec((1, n_heads, s), lambda i: (i, 0, 0)),
        ],
        out_shape=[
            jax.ShapeDtypeStruct((b, s, LANES), F32),
            jax.ShapeDtypeStruct((b, n_heads, s), F32),
        ],
        compiler_params=_params(("arbitrary",)),
        name="forget_cumsum",
    )(f_pre, b_pad)


def _fox_kernel(q_ref, k_ref, v_ref, ccol_ref, crow_ref, o_ref, m_scr, l_scr, acc_scr, *, tq):
    h = pl.program_id(1)
    qi = pl.program_id(2)
    scale = 1.0 / np.sqrt(HEAD_DIM)
    q = q_ref[0]
    lane = lax.broadcasted_iota(jnp.int32, (tq, LANES), 1)
    cq = jnp.sum(jnp.where(lane == h, ccol_ref[0], 0.0), axis=1, keepdims=True)

    m_scr[...] = jnp.full(m_scr.shape, NEG_INF, F32)
    l_scr[...] = jnp.zeros(l_scr.shape, F32)
    acc_scr[...] = jnp.zeros(acc_scr.shape, F32)

    def step(j, causal):
        ks = pl.multiple_of(j * tq, tq)
        k = k_ref[0, pl.ds(ks, tq), :]
        v = v_ref[0, pl.ds(ks, tq), :]
        ck = crow_ref[0, 0, pl.ds(j, 1), :]
        s = lax.dot_general(q, k, NT_DIMS, preferred_element_type=F32) * scale
        s = s + cq - ck
        if causal:
            r = lax.broadcasted_iota(jnp.int32, s.shape, 0)
            c = lax.broadcasted_iota(jnp.int32, s.shape, 1)
            s = jnp.where(c <= r, s, NEG_INF)
        m_prev = m_scr[...]
        m_new = jnp.maximum(m_prev, jnp.max(s, axis=1, keepdims=True))
        alpha = jnp.exp(m_prev - m_new)
        p = jnp.exp(s - m_new)
        l_scr[...] = alpha * l_scr[...] + jnp.sum(p, axis=1, keepdims=True)
        acc_scr[...] = alpha * acc_scr[...] + jnp.dot(p.astype(BF16), v, preferred_element_type=F32)
        m_scr[...] = m_new

    def body(j, carry):
        step(j, False)
        return carry

    lax.fori_loop(0, qi, body, 0)
    step(qi, True)
    o_ref[0] = (acc_scr[...] / l_scr[...]).astype(o_ref.dtype)


def _fox(qkv, ccol, crow4, *, q_off, k_off, v_off, n_heads):
    b, s, _ = qkv.shape
    tq = crow4.shape[-1]
    nq = s // tq
    return pl.pallas_call(
        functools.partial(_fox_kernel, tq=tq),
        grid=(b, n_heads, nq),
        in_specs=[
            pl.BlockSpec((1, tq, HEAD_DIM), lambda bi, h, qi: (bi, qi, q_off + h)),
            pl.BlockSpec((1, s, HEAD_DIM), lambda bi, h, qi: (bi, 0, k_off + h)),
            pl.BlockSpec((1, s, HEAD_DIM), lambda bi, h, qi: (bi, 0, v_off + h)),
            pl.BlockSpec((1, tq, LANES), lambda bi, h, qi: (bi, qi, 0)),
            pl.BlockSpec((1, 1, nq, tq), lambda bi, h, qi: (bi, h, 0, 0)),
        ],
        out_specs=pl.BlockSpec((1, tq, HEAD_DIM), lambda bi, h, qi: (bi, qi, h)),
        out_shape=jax.ShapeDtypeStruct((b, s, n_heads * HEAD_DIM), BF16),
        scratch_shapes=[
            pltpu.VMEM((tq, 1), F32),
            pltpu.VMEM((tq, 1), F32),
            pltpu.VMEM((tq, HEAD_DIM), F32),
        ],
        compiler_params=_params(("arbitrary", "arbitrary", "arbitrary")),
        name="fox_attention",
    )(qkv, qkv, qkv, ccol, crow4)


def _band_attend(qb, kb, vb, bias):
    scale = 1.0 / np.sqrt(HEAD_DIM)
    s = lax.dot_general(qb, kb, NT_DIMS, preferred_element_type=F32) * scale + bias
    m = jnp.max(s, axis=-1, keepdims=True)
    e = jnp.exp(s - m)
    den = jnp.sum(e, axis=-1, keepdims=True)
    o = jnp.dot(e.astype(BF16), vb, preferred_element_type=F32) / den
    lse = m + jnp.log(den)
    return o, jnp.broadcast_to(lse, o.shape)


def _dilated_kernel(slopes_ref, q_ref, k_ref, v_ref, o_ref,
                    qf, kf, vf, o1, o2, o3, l1, l2, l3):
    h = pl.program_id(1)
    slope = slopes_ref[h]
    s_len = q_ref.shape[1]
    steps = 128

    qf[...] = q_ref[0].astype(F32)
    kf[...] = k_ref[0].astype(F32)
    vf[...] = v_ref[0].astype(F32)

    i_idx = lax.broadcasted_iota(jnp.int32, (steps, 2 * steps), 0)
    j_idx = lax.broadcasted_iota(jnp.int32, (steps, 2 * steps), 1)
    dist = i_idx + steps - j_idx
    valid = jnp.logical_and(dist >= 0, dist <= steps)

    def make_bias(dil):
        pen = slope * (dil * dist).astype(F32)
        return jnp.where(valid, -pen, NEG_INF)

    def run_pattern(dil, o_scr, l_scr):
        bias = make_bias(dil)
        bias_cur = bias[:, steps:]
        sub_len = s_len // dil
        nb = sub_len // steps

        def residue(r, carry):
            if dil == 1:
                def load(ref_bf, ref_f, start, size):
                    return ref_bf[0, pl.ds(start, size), :]
            else:
                def load(ref_bf, ref_f, start, size):
                    return ref_f[pl.ds(r + dil * start, size, stride=dil), :].astype(BF16)

            def store(start, o, lse):
                if dil == 1:
                    o_scr[pl.ds(start, steps), :] = o
                    l_scr[pl.ds(start, steps), :] = lse
                else:
                    o_scr[pl.ds(r + dil * start, steps, stride=dil), :] = o
                    l_scr[pl.ds(r + dil * start, steps, stride=dil), :] = lse

            o, lse = _band_attend(load(q_ref, qf, 0, steps), load(k_ref, kf, 0, steps),
                                  load(v_ref, vf, 0, steps), bias_cur)
            store(0, o, lse)

            def block(n, c):
                q0 = pl.multiple_of(n * steps, steps)
                k0 = pl.multiple_of((n - 1) * steps, steps)
                o, lse = _band_attend(load(q_ref, qf, q0, steps), load(k_ref, kf, k0, 2 * steps),
                                      load(v_ref, vf, k0, 2 * steps), bias)
                store(q0, o, lse)
                return c

            if nb > 1:
                lax.fori_loop(1, nb, block, 0)
            return carry

        if dil == 1:
            residue(0, 0)
        else:
            lax.fori_loop(0, dil, residue, 0)

    for (window, dil), o_scr, l_scr in zip(DIL_PATTERNS, (o1, o2, o3), (l1, l2, l3)):
        assert window // dil == steps
        run_pattern(dil, o_scr, l_scr)

    la, lb, lc = l1[...], l2[...], l3[...]
    mx = jnp.maximum(jnp.maximum(la, lb), lc)
    wa, wb, wc = jnp.exp(la - mx), jnp.exp(lb - mx), jnp.exp(lc - mx)
    den = wa + wb + wc
    o_ref[0] = ((wa * o1[...] + wb * o2[...] + wc * o3[...]) / den).astype(o_ref.dtype)


def _dilated(qkv, slopes, *, q_off, k_off, v_off, n_heads):
    b, s, _ = qkv.shape
    spec = lambda off: pl.BlockSpec((1, s, HEAD_DIM), lambda bi, h, sl: (bi, 0, off + h))
    return pl.pallas_call(
        _dilated_kernel,
        grid_spec=pltpu.PrefetchScalarGridSpec(
            num_scalar_prefetch=1,
            grid=(b, n_heads),
            in_specs=[spec(q_off), spec(k_off), spec(v_off)],
            out_specs=pl.BlockSpec((1, s, HEAD_DIM), lambda bi, h, sl: (bi, 0, h)),
            scratch_shapes=[pltpu.VMEM((s, HEAD_DIM), F32)] * 9,
        ),
        out_shape=jax.ShapeDtypeStruct((b, s, n_heads * HEAD_DIM), BF16),
        compiler_params=_params(("arbitrary", "arbitrary")),
        name="dilated_attention",
    )(slopes, qkv, qkv, qkv)


def _mix_kernel(oa_ref, ob_ref, ga_ref, gb_ref, x_ref, wa_ref, wb_ref, wo_ref, g_ref,
                x1_ref, h2_ref):
    ta = jnp.dot(oa_ref[...], wa_ref[...], preferred_element_type=F32)
    tb = jnp.dot(ob_ref[...], wb_ref[...], preferred_element_type=F32)
    merged = (ga_ref[...] * ta + gb_ref[...] * tb).astype(BF16)
    x1 = x_ref[...] + jnp.dot(merged, wo_ref[...], preferred_element_type=F32)
    x1_ref[...] = x1
    ms = jnp.mean(x1 * x1, axis=-1, keepdims=True)
    h2_ref[...] = (x1 * lax.rsqrt(ms + EPS) * g_ref[...]).astype(BF16)


def _mix(oa, ob, gates, x2, w_a, w_b, w_o, g_ffn):
    m, d = x2.shape
    wa_rows, wb_rows = w_a.shape[0], w_b.shape[0]
    tm = 256
    resident = lambda shape: pl.BlockSpec(shape, lambda i: (0, 0), pipeline_mode=pl.Buffered(1))
    return pl.pallas_call(
        _mix_kernel,
        grid=(m // tm,),
        in_specs=[
            pl.BlockSpec((tm, wa_rows), lambda i: (i, 0)),
            pl.BlockSpec((tm, wb_rows), lambda i: (i, 0)),
            pl.BlockSpec((tm, d), lambda i: (i, 0)),
            pl.BlockSpec((tm, d), lambda i: (i, 1)),
            pl.BlockSpec((tm, d), lambda i: (i, 0)),
            resident((wa_rows, d)),
            resident((wb_rows, d)),
            resident((d, d)),
            pl.BlockSpec((1, d), lambda i: (0, 0)),
        ],
        out_specs=[
            pl.BlockSpec((tm, d), lambda i: (i, 0)),
            pl.BlockSpec((tm, d), lambda i: (i, 0)),
        ],
        out_shape=[
            jax.ShapeDtypeStruct((m, d), F32),
            jax.ShapeDtypeStruct((m, d), BF16),
        ],
        compiler_params=_params(("arbitrary",)),
        name="branch_mix_out_proj",
    )(oa, ob, gates, gates, x2, w_a, w_b, w_o, g_ffn)


def _ffn_kernel(h_ref, halo_ref, x1_ref, wg_ref, wv_ref, cg_ref, cv_ref, bg_ref, bv_ref, wd_ref,
                out_ref, lhs_scr, ug_scr, uv_scr, acc_scr, *, tm, halo, tiles_per_seq):
    i = pl.program_id(0)
    j = pl.program_id(1)

    @pl.when(j == 0)
    def _():
        first = (i % tiles_per_seq) == 0
        prev = halo_ref[...]
        lhs_scr[0:halo, :] = jnp.where(first, jnp.zeros_like(prev), prev)
        lhs_scr[halo:, :] = h_ref[...]
        acc_scr[...] = jnp.zeros(acc_scr.shape, F32)

    lhs = lhs_scr[...]
    ug_scr[...] = jnp.dot(lhs, wg_ref[...], preferred_element_type=F32)
    uv_scr[...] = jnp.dot(lhs, wv_ref[...], preferred_element_type=F32)

    def conv(u_scr, c_ref, b_ref):
        out = b_ref[...]
        for t in range(CONV_WIDTH):
            shift = CONV_WIDTH - 1 - t
            out = out + c_ref[t:t + 1, :] * u_scr[pl.ds(halo - shift, tm), :]
        return out

    gate = conv(ug_scr, cg_ref, bg_ref)
    val = conv(uv_scr, cv_ref, bv_ref)
    a = (gate * jax.nn.sigmoid(gate) * val).astype(BF16)
    acc_scr[...] += jnp.dot(a, wd_ref[...], preferred_element_type=F32)

    @pl.when(j == pl.num_programs(1) - 1)
    def _():
        out_ref[...] = x1_ref[...] + acc_scr[...]


def _ffn(h2, x1, w_up, w_conv, b_conv, w_down, *, seq_len):
    m, d = h2.shape
    d_ff = w_down.shape[0]
    tm, tf, halo = 512, 512, BF16_ROWS
    nj = d_ff // tf
    kern = functools.partial(_ffn_kernel, tm=tm, halo=halo, tiles_per_seq=seq_len // tm)
    return pl.pallas_call(
        kern,
        grid=(m // tm, nj),
        in_specs=[
            pl.BlockSpec((tm, d), lambda i, j: (i, 0)),
            pl.BlockSpec((halo, d), lambda i, j: (jnp.maximum(i * (tm // halo) - 1, 0), 0)),
            pl.BlockSpec((tm, d), lambda i, j: (i, 0)),
            pl.BlockSpec((d, tf), lambda i, j: (0, j)),
            pl.BlockSpec((d, tf), lambda i, j: (0, nj + j)),
            pl.BlockSpec((CONV_WIDTH, tf), lambda i, j: (0, j)),
            pl.BlockSpec((CONV_WIDTH, tf), lambda i, j: (0, nj + j)),
            pl.BlockSpec((1, tf), lambda i, j: (0, j)),
            pl.BlockSpec((1, tf), lambda i, j: (0, nj + j)),
            pl.BlockSpec((tf, d), lambda i, j: (j, 0)),
        ],
        out_specs=pl.BlockSpec((tm, d), lambda i, j: (i, 0)),
        out_shape=jax.ShapeDtypeStruct((m, d), F32),
        scratch_shapes=[
            pltpu.VMEM((tm + halo, d), BF16),
            pltpu.VMEM((tm + halo, tf), F32),
            pltpu.VMEM((tm + halo, tf), F32),
            pltpu.VMEM((tm, d), F32),
        ],
        compiler_params=_params(("arbitrary", "arbitrary")),
        name="conv_ffn",
    )(h2, h2, x1, w_up, w_up, w_conv, w_conv, b_conv, b_conv, w_down)


def _layer(x, g_attn, w_in, b_forget, g_q_fox, g_k_fox, g_q_dil, g_k_dil,
           w_br_fox, w_br_dil, w_out, g_ffn, w_up, w_conv, b_conv, w_down):
    b, s, d = x.shape
    w_fox = N_HEADS_FOX * HEAD_DIM
    w_dil = N_HEADS_DIL * HEAD_DIM
    n_qkv = 3 * w_fox + 3 * w_dil
    f_lo, f_hi = 3 * w_fox, 3 * w_fox + N_HEADS_FOX

    w_main = jnp.concatenate([w_in[:, :f_lo], w_in[:, f_hi:]], axis=1).astype(BF16)
    w_f = jnp.pad(w_in[:, f_lo:f_hi], ((0, 0), (0, LANES - N_HEADS_FOX))).astype(BF16)
    ones = jnp.ones((w_fox,), F32)
    g_qk = jnp.concatenate([g_q_fox.reshape(-1), g_k_fox.reshape(-1), ones,
                            g_q_dil.reshape(-1), g_k_dil.reshape(-1), ones]).reshape(1, n_qkv)
    b_pad = jnp.pad(b_forget, (0, LANES - N_HEADS_FOX)).reshape(1, LANES)
    slopes = jnp.asarray(2.0 ** (-8.0 * np.arange(1, N_HEADS_DIL + 1) / N_HEADS_DIL), dtype=F32)

    x2 = x.reshape(b * s, d)
    qkv, gates, f_pre = _in_proj(x2, g_attn.reshape(1, d), w_main, w_f, g_qk,
                                 n_qkv=n_qkv, n_gate=2 * d)
    qkv = qkv.reshape(b, s, n_qkv)

    tq = 512
    ccol, crow = _forget(f_pre.reshape(b, s, LANES), b_pad, n_heads=N_HEADS_FOX)
    crow4 = crow.reshape(b, N_HEADS_FOX, s // tq, tq)
    o_a = _fox(qkv, ccol, crow4, q_off=0, k_off=N_HEADS_FOX, v_off=2 * N_HEADS_FOX,
               n_heads=N_HEADS_FOX)
    base = 3 * N_HEADS_FOX
    o_b = _dilated(qkv, slopes, q_off=base, k_off=base + N_HEADS_DIL,
                   v_off=base + 2 * N_HEADS_DIL, n_heads=N_HEADS_DIL)

    x1, h2 = _mix(o_a.reshape(b * s, w_fox), o_b.reshape(b * s, w_dil), gates, x2,
                  w_br_fox.astype(BF16), w_br_dil.astype(BF16), w_out.astype(BF16),
                  g_ffn.reshape(1, d))
    out = _ffn(h2, x1, w_up.astype(BF16), w_conv, b_conv.reshape(1, -1), w_down.astype(BF16),
               seq_len=s)
    return out.reshape(b, s, d)


def kernel(x, g_attn, w_in, b_forget, g_q_fox, g_k_fox, g_q_dil, g_k_dil, w_br_fox, w_br_dil,
           w_out, g_ffn, w_up, w_conv, b_conv, w_down):
    for l in range(w_in.shape[0]):
        x = _layer(x, g_attn[l], w_in[l], b_forget[l], g_q_fox[l], g_k_fox[l], g_q_dil[l],
                   g_k_dil[l], w_br_fox[l], w_br_dil[l], w_out[l], g_ffn[l], w_up[l], w_conv[l],
                   b_conv[l], w_down[l])
    return x
```

```python
import functools

import numpy as np
import jax
import jax.numpy as jnp
from jax import lax
from jax.experimental import pallas as pl
from jax.experimental.pallas import tpu as pltpu

HEAD_DIM = 128
N_HEADS_FOX = 8
N_HEADS_DIL = 8
DIL_PATTERNS = ((128, 1), (512, 4), (2048, 16))
CONV_WIDTH = 3
EPS = 1e-6
NEG_INF = -1e30

F32 = jnp.float32
BF16 = jnp.bfloat16

LANES = 128
BF16_ROWS = 16
VMEM_LIMIT = 56 * 1024 * 1024

NT_DIMS = (((1,), (1,)), ((), ()))


def _params(sem):
    return pltpu.CompilerParams(dimension_semantics=sem, vmem_limit_bytes=VMEM_LIMIT)


def _in_proj_kernel(x_ref, g_ref, wa_ref, wb_ref, wf_ref, gqk_ref, qkv_ref, gate_ref, f_ref, h_scr,
                    *, n_a_tiles, n_qkv_tiles, tiles_per_group, heads_per_tile):
    j = pl.program_id(1)

    @pl.when(j == 0)
    def _():
        x = x_ref[...]
        ms = jnp.mean(x * x, axis=-1, keepdims=True)
        h = (x * lax.rsqrt(ms + EPS) * g_ref[...]).astype(BF16)
        h_scr[...] = h
        f_ref[...] = jnp.dot(h, wf_ref[...], preferred_element_type=F32)

    def project(w_ref):
        return jnp.dot(h_scr[...], w_ref[...], preferred_element_type=F32)

    def store_qk(acc):
        for hh in range(heads_per_tile):
            sl = slice(hh * HEAD_DIM, (hh + 1) * HEAD_DIM)
            a = acc[:, sl]
            ms = jnp.mean(a * a, axis=-1, keepdims=True)
            qkv_ref[:, sl] = (a * lax.rsqrt(ms + EPS) * gqk_ref[:, sl]).astype(BF16)

    def store_v(acc):
        qkv_ref[...] = acc.astype(BF16)

    def store_gate(acc):
        gate_ref[...] = jax.nn.sigmoid(acc)

    is_qk = (j % (3 * tiles_per_group)) < 2 * tiles_per_group
    from_a = j < n_a_tiles
    is_qkv = j < n_qkv_tiles
    cases = (
        (jnp.logical_and(from_a, is_qk), wa_ref, store_qk),
        (jnp.logical_and(from_a, jnp.logical_not(is_qk)), wa_ref, store_v),
        (jnp.logical_and(jnp.logical_and(jnp.logical_not(from_a), is_qkv), is_qk), wb_ref, store_qk),
        (jnp.logical_and(jnp.logical_and(jnp.logical_not(from_a), is_qkv), jnp.logical_not(is_qk)),
         wb_ref, store_v),
        (jnp.logical_not(is_qkv), wb_ref, store_gate),
    )
    for cond, w_ref, store in cases:
        pl.when(cond)(lambda w_ref=w_ref, store=store: store(project(w_ref)))


def _in_proj(x2, g_attn, w_a, w_b, w_f, g_qk, *, n_qkv, n_gate):
    m, d = x2.shape
    tm, tn = 1024, 512
    n_a_tiles = w_a.shape[1] // tn
    n_qkv_tiles = n_qkv // tn
    n_tiles = (n_qkv + n_gate) // tn
    kern = functools.partial(
        _in_proj_kernel, n_a_tiles=n_a_tiles, n_qkv_tiles=n_qkv_tiles,
        tiles_per_group=(N_HEADS_FOX * HEAD_DIM) // tn, heads_per_tile=tn // HEAD_DIM)
    return pl.pallas_call(
        kern,
        grid=(m // tm, n_tiles),
        in_specs=[
            pl.BlockSpec((tm, d), lambda i, j: (i, 0)),
            pl.BlockSpec((1, d), lambda i, j: (0, 0)),
            pl.BlockSpec((d, tn), lambda i, j: (0, jnp.minimum(j, n_a_tiles - 1))),
            pl.BlockSpec((d, tn), lambda i, j: (0, jnp.maximum(j - n_a_tiles, 0))),
            pl.BlockSpec((d, LANES), lambda i, j: (0, 0)),
            pl.BlockSpec((1, tn), lambda i, j: (0, jnp.minimum(j, n_qkv_tiles - 1))),
        ],
        out_specs=[
            pl.BlockSpec((tm, tn), lambda i, j: (i, jnp.minimum(j, n_qkv_tiles - 1))),
            pl.BlockSpec((tm, tn), lambda i, j: (i, jnp.maximum(j - n_qkv_tiles, 0))),
            pl.BlockSpec((tm, LANES), lambda i, j: (i, 0)),
        ],
        out_shape=[
            jax.ShapeDtypeStruct((m, n_qkv), BF16),
            jax.ShapeDtypeStruct((m, n_gate), F32),
            jax.ShapeDtypeStruct((m, LANES), F32),
        ],
        scratch_shapes=[pltpu.VMEM((tm, d), BF16)],
        compiler_params=_params(("arbitrary", "arbitrary")),
        name="in_proj",
    )(x2, g_attn, w_a, w_b, w_f, g_qk)


def _forget_kernel(f_ref, b_ref, ccol_ref, crow_ref, *, n_heads):
    z = f_ref[0] + b_ref[...]
    c = jax.nn.log_sigmoid(z)
    s = c.shape[0]
    row = lax.broadcasted_iota(jnp.int32, c.shape, 0)
    k = 1
    while k < s:
        c = c + jnp.where(row >= k, pltpu.roll(c, k, axis=0), 0.0)
        k *= 2
    ccol_ref[0] = c
    crow_ref[0] = c.T[:n_heads, :]


def _forget(f_pre, b_pad, *, n_heads):
    b, s, _ = f_pre.shape
    return pl.pallas_call(
        functools.partial(_forget_kernel, n_heads=n_heads),
        grid=(b,),
        in_specs=[
            pl.BlockSpec((1, s, LANES), lambda i: (i, 0, 0)),
            pl.BlockSpec((1, LANES), lambda i: (0, 0)),
        ],
        out_specs=[
            pl.BlockSpec((1, s, LANES), lambda i: (i, 0, 0)),
            pl.BlockSpec((1, n_heads, s), lambda i: (i, 0, 0)),
        ],
        out_shape=[
            jax.ShapeDtypeStruct((b, s, LANES), F32),
            jax.ShapeDtypeStruct((b, n_heads, s), F32),
        ],
        compiler_params=_params(("arbitrary",)),
        name="forget_cumsum",
    )(f_pre, b_pad)


def _fox_kernel(q_ref, k_ref, v_ref, ccol_ref, crow_ref, o_ref, *, tq):
    h = pl.program_id(1)
    s_len = q_ref.shape[1]
    scale = 1.0 / np.sqrt(HEAD_DIM)
    lane = lax.broadcasted_iota(jnp.int32, (s_len, LANES), 1)
    cq_all = jnp.sum(jnp.where(lane == h, ccol_ref[0], 0.0), axis=1, keepdims=True)
    ck_all = crow_ref[0, 0]
    r = lax.broadcasted_iota(jnp.int32, (tq, tq), 0)
    c = lax.broadcasted_iota(jnp.int32, (tq, tq), 1)
    causal = c <= r
    for qi in range(s_len // tq):
        q0, kv = qi * tq, (qi + 1) * tq
        s = lax.dot_general(q_ref[0, q0:kv, :], k_ref[0, :kv, :], NT_DIMS,
                            preferred_element_type=F32) * scale
        s = s + cq_all[q0:kv] - ck_all[:, :kv]
        s_diag = jnp.where(causal, s[:, q0:], NEG_INF)
        s = jnp.concatenate([s[:, :q0], s_diag], axis=1) if qi else s_diag
        m = jnp.max(s, axis=1, keepdims=True)
        p = jnp.exp(s - m)
        l = jnp.sum(p, axis=1, keepdims=True)
        o = jnp.dot(p.astype(BF16), v_ref[0, :kv, :], preferred_element_type=F32) / l
        o_ref[0, q0:kv, :] = o.astype(o_ref.dtype)


def _fox(qkv, ccol, crow4, *, q_off, k_off, v_off, n_heads, tq):
    b, s, _ = qkv.shape
    spec = lambda off: pl.BlockSpec((1, s, HEAD_DIM), lambda bi, h: (bi, 0, off + h))
    return pl.pallas_call(
        functools.partial(_fox_kernel, tq=tq),
        grid=(b, n_heads),
        in_specs=[
            spec(q_off), spec(k_off), spec(v_off),
            pl.BlockSpec((1, s, LANES), lambda bi, h: (bi, 0, 0)),
            pl.BlockSpec((1, 1, 1, s), lambda bi, h: (bi, h, 0, 0)),
        ],
        out_specs=spec(0),
        out_shape=jax.ShapeDtypeStruct((b, s, n_heads * HEAD_DIM), BF16),
        compiler_params=_params(("arbitrary", "arbitrary")),
        name="fox_attention",
    )(qkv, qkv, qkv, ccol, crow4)


def _band_attend(qb, kb, vb, bias):
    scale = 1.0 / np.sqrt(HEAD_DIM)
    s = lax.dot_general(qb, kb, NT_DIMS, preferred_element_type=F32) * scale + bias
    m = jnp.max(s, axis=-1, keepdims=True)
    e = jnp.exp(s - m)
    den = jnp.sum(e, axis=-1, keepdims=True)
    o = jnp.dot(e.astype(BF16), vb, preferred_element_type=F32) / den
    lse = m + jnp.log(den)
    return o, jnp.broadcast_to(lse, o.shape)


def _dilated_kernel(slopes_ref, q_ref, k_ref, v_ref, o_ref, qf, kf, vf, *scratch):
    n_pat = len(DIL_PATTERNS)
    perm = scratch[:3 * n_pat]
    o_scr = scratch[3 * n_pat:4 * n_pat]
    l_scr = scratch[4 * n_pat:5 * n_pat]
    h = pl.program_id(1)
    slope = slopes_ref[h]
    s_len = q_ref.shape[1]
    steps = DIL_PATTERNS[0][0] // DIL_PATTERNS[0][1]
    n_blocks = s_len // steps

    qf[...] = q_ref[0].astype(F32)
    kf[...] = k_ref[0].astype(F32)
    vf[...] = v_ref[0].astype(F32)

    i_idx = lax.broadcasted_iota(jnp.int32, (steps, 2 * steps), 0)
    j_idx = lax.broadcasted_iota(jnp.int32, (steps, 2 * steps), 1)
    dist = i_idx + steps - j_idx
    valid = jnp.logical_and(dist >= 0, dist <= steps)

    for p, (window, dil) in enumerate(DIL_PATTERNS):
        assert window // dil == steps
        bias = jnp.where(valid, -(slope * (dil * dist).astype(F32)), NEG_INF)
        bias_cur = bias[:, steps:]
        sub_len = s_len // dil
        blocks_per_residue = sub_len // steps
        if dil == 1:
            qp, kp, vp = (lambda a, b: q_ref[0, a:b, :]), (lambda a, b: k_ref[0, a:b, :]), \
                         (lambda a, b: v_ref[0, a:b, :])
        else:
            for src, dst in zip((qf, kf, vf), perm[3 * p:3 * p + 3]):
                for res in range(dil):
                    dst[res * sub_len:(res + 1) * sub_len, :] = (
                        src[pl.ds(res, sub_len, stride=dil), :].astype(BF16))
            qp, kp, vp = [(lambda a, b, ref=ref: ref[a:b, :]) for ref in perm[3 * p:3 * p + 3]]

        for nb in range(n_blocks):
            res, n = divmod(nb, blocks_per_residue)
            lo, hi = nb * steps, (nb + 1) * steps
            if n == 0:
                o, lse = _band_attend(qp(lo, hi), kp(lo, hi), vp(lo, hi), bias_cur)
            else:
                o, lse = _band_attend(qp(lo, hi), kp(lo - steps, hi), vp(lo - steps, hi), bias)
            if dil == 1:
                o_scr[p][lo:hi, :] = o
                l_scr[p][lo:hi, :] = lse
            else:
                t0 = res + dil * steps * n
                o_scr[p][pl.ds(t0, steps, stride=dil), :] = o
                l_scr[p][pl.ds(t0, steps, stride=dil), :] = lse

    lses = [ref[...] for ref in l_scr]
    mx = functools.reduce(jnp.maximum, lses)
    ws = [jnp.exp(l - mx) for l in lses]
    num = sum(w * ref[...] for w, ref in zip(ws, o_scr))
    o_ref[0] = (num / sum(ws)).astype(o_ref.dtype)


def _dilated(qkv, slopes, *, q_off, k_off, v_off, n_heads):
    b, s, _ = qkv.shape
    n_pat = len(DIL_PATTERNS)
    spec = lambda off: pl.BlockSpec((1, s, HEAD_DIM), lambda bi, h, sl: (bi, 0, off + h))
    return pl.pallas_call(
        _dilated_kernel,
        grid_spec=pltpu.PrefetchScalarGridSpec(
            num_scalar_prefetch=1,
            grid=(b, n_heads),
            in_specs=[spec(q_off), spec(k_off), spec(v_off)],
            out_specs=spec(0),
            scratch_shapes=([pltpu.VMEM((s, HEAD_DIM), F32)] * 3
                            + [pltpu.VMEM((s, HEAD_DIM), BF16)] * (3 * n_pat)
                            + [pltpu.VMEM((s, HEAD_DIM), F32)] * (2 * n_pat)),
        ),
        out_shape=jax.ShapeDtypeStruct((b, s, n_heads * HEAD_DIM), BF16),
        compiler_params=_params(("arbitrary", "arbitrary")),
        name="dilated_attention",
    )(slopes, qkv, qkv, qkv)


def _mix_kernel(oa_ref, ob_ref, ga_ref, gb_ref, x_ref, wa_ref, wb_ref, wo_ref, g_ref,
                x1_ref, h2_ref):
    ta = jnp.dot(oa_ref[...], wa_ref[...], preferred_element_type=F32)
    tb = jnp.dot(ob_ref[...], wb_ref[...], preferred_element_type=F32)
    merged = (ga_ref[...] * ta + gb_ref[...] * tb).astype(BF16)
    x1 = x_ref[...] + jnp.dot(merged, wo_ref[...], preferred_element_type=F32)
    x1_ref[...] = x1
    ms = jnp.mean(x1 * x1, axis=-1, keepdims=True)
    h2_ref[...] = (x1 * lax.rsqrt(ms + EPS) * g_ref[...]).astype(BF16)


def _mix(oa, ob, gates, x2, w_a, w_b, w_o, g_ffn):
    m, d = x2.shape
    wa_rows, wb_rows = w_a.shape[0], w_b.shape[0]
    tm = 256
    resident = lambda shape: pl.BlockSpec(shape, lambda i: (0, 0), pipeline_mode=pl.Buffered(1))
    return pl.pallas_call(
        _mix_kernel,
        grid=(m // tm,),
        in_specs=[
            pl.BlockSpec((tm, wa_rows), lambda i: (i, 0)),
            pl.BlockSpec((tm, wb_rows), lambda i: (i, 0)),
            pl.BlockSpec((tm, d), lambda i: (i, 0)),
            pl.BlockSpec((tm, d), lambda i: (i, 1)),
            pl.BlockSpec((tm, d), lambda i: (i, 0)),
            resident((wa_rows, d)),
            resident((wb_rows, d)),
            resident((d, d)),
            pl.BlockSpec((1, d), lambda i: (0, 0)),
        ],
        out_specs=[
            pl.BlockSpec((tm, d), lambda i: (i, 0)),
            pl.BlockSpec((tm, d), lambda i: (i, 0)),
        ],
        out_shape=[
            jax.ShapeDtypeStruct((m, d), F32),
            jax.ShapeDtypeStruct((m, d), BF16),
        ],
        compiler_params=_params(("arbitrary",)),
        name="branch_mix_out_proj",
    )(oa, ob, gates, gates, x2, w_a, w_b, w_o, g_ffn)


def _ffn_kernel(h_ref, halo_ref, x1_ref, wg_ref, wv_ref, cg_ref, cv_ref, bg_ref, bv_ref, wd_ref,
                out_ref, lhs_scr, ug_scr, uv_scr, acc_scr, *, tm, halo, tiles_per_seq):
    i = pl.program_id(0)
    j = pl.program_id(1)

    @pl.when(j == 0)
    def _():
        first = (i % tiles_per_seq) == 0
        prev = halo_ref[...]
        lhs_scr[0:halo, :] = jnp.where(first, jnp.zeros_like(prev), prev)
        lhs_scr[halo:, :] = h_ref[...]
        acc_scr[...] = jnp.zeros(acc_scr.shape, F32)

    lhs = lhs_scr[...]
    ug_scr[...] = jnp.dot(lhs, wg_ref[...], preferred_element_type=F32)
    uv_scr[...] = jnp.dot(lhs, wv_ref[...], preferred_element_type=F32)

    def conv(u_scr, c_ref, b_ref):
        out = b_ref[...]
        for t in range(CONV_WIDTH):
            shift = CONV_WIDTH - 1 - t
            out = out + c_ref[t:t + 1, :] * u_scr[pl.ds(halo - shift, tm), :]
        return out

    gate = conv(ug_scr, cg_ref, bg_ref)
    val = conv(uv_scr, cv_ref, bv_ref)
    a = (gate * jax.nn.sigmoid(gate) * val).astype(BF16)
    acc_scr[...] += jnp.dot(a, wd_ref[...], preferred_element_type=F32)

    @pl.when(j == pl.num_programs(1) - 1)
    def _():
        out_ref[...] = x1_ref[...] + acc_scr[...]


def _ffn(h2, x1, w_up, w_conv, b_conv, w_down, *, seq_len):
    m, d = h2.shape
    d_ff = w_down.shape[0]
    tm, tf, halo = 512, 512, BF16_ROWS
    nj = d_ff // tf
    kern = functools.partial(_ffn_kernel, tm=tm, halo=halo, tiles_per_seq=seq_len // tm)
    return pl.pallas_call(
        kern,
        grid=(m // tm, nj),
        in_specs=[
            pl.BlockSpec((tm, d), lambda i, j: (i, 0)),
            pl.BlockSpec((halo, d), lambda i, j: (jnp.maximum(i * (tm // halo) - 1, 0), 0)),
            pl.BlockSpec((tm, d), lambda i, j: (i, 0)),
            pl.BlockSpec((d, tf), lambda i, j: (0, j)),
            pl.BlockSpec((d, tf), lambda i, j: (0, nj + j)),
            pl.BlockSpec((CONV_WIDTH, tf), lambda i, j: (0, j)),
            pl.BlockSpec((CONV_WIDTH, tf), lambda i, j: (0, nj + j)),
            pl.BlockSpec((1, tf), lambda i, j: (0, j)),
            pl.BlockSpec((1, tf), lambda i, j: (0, nj + j)),
            pl.BlockSpec((tf, d), lambda i, j: (j, 0)),
        ],
        out_specs=pl.BlockSpec((tm, d), lambda i, j: (i, 0)),
        out_shape=jax.ShapeDtypeStruct((m, d), F32),
        scratch_shapes=[
            pltpu.VMEM((tm + halo, d), BF16),
            pltpu.VMEM((tm + halo, tf), F32),
            pltpu.VMEM((tm + halo, tf), F32),
            pltpu.VMEM((tm, d), F32),
        ],
        compiler_params=_params(("arbitrary", "arbitrary")),
        name="conv_ffn",
    )(h2, h2, x1, w_up, w_up, w_conv, w_conv, b_conv, b_conv, w_down)


def _layer(x, g_attn, w_in, b_forget, g_q_fox, g_k_fox, g_q_dil, g_k_dil,
           w_br_fox, w_br_dil, w_out, g_ffn, w_up, w_conv, b_conv, w_down):
    b, s, d = x.shape
    w_fox = N_HEADS_FOX * HEAD_DIM
    w_dil = N_HEADS_DIL * HEAD_DIM
    n_qkv = 3 * w_fox + 3 * w_dil
    f_lo, f_hi = 3 * w_fox, 3 * w_fox + N_HEADS_FOX

    w_a = w_in[:, :f_lo].astype(BF16)
    w_b = w_in[:, f_hi:].astype(BF16)
    w_f = jnp.pad(w_in[:, f_lo:f_hi], ((0, 0), (0, LANES - N_HEADS_FOX))).astype(BF16)
    ones = jnp.ones((w_fox,), F32)
    g_qk = jnp.concatenate([g_q_fox.reshape(-1), g_k_fox.reshape(-1), ones,
                            g_q_dil.reshape(-1), g_k_dil.reshape(-1), ones]).reshape(1, n_qkv)
    b_pad = jnp.pad(b_forget, (0, LANES - N_HEADS_FOX)).reshape(1, LANES)
    slopes = jnp.asarray(2.0 ** (-8.0 * np.arange(1, N_HEADS_DIL + 1) / N_HEADS_DIL), dtype=F32)

    x2 = x.reshape(b * s, d)
    qkv, gates, f_pre = _in_proj(x2, g_attn.reshape(1, d), w_a, w_b, w_f, g_qk,
                                 n_qkv=n_qkv, n_gate=2 * d)
    qkv = qkv.reshape(b, s, n_qkv)

    ccol, crow = _forget(f_pre.reshape(b, s, LANES), b_pad, n_heads=N_HEADS_FOX)
    crow4 = crow.reshape(b, N_HEADS_FOX, 1, s)
    o_a = _fox(qkv, ccol, crow4, q_off=0, k_off=N_HEADS_FOX, v_off=2 * N_HEADS_FOX,
               n_heads=N_HEADS_FOX, tq=512)
    base = 3 * N_HEADS_FOX
    o_b = _dilated(qkv, slopes, q_off=base, k_off=base + N_HEADS_DIL,
                   v_off=base + 2 * N_HEADS_DIL, n_heads=N_HEADS_DIL)

    x1, h2 = _mix(o_a.reshape(b * s, w_fox), o_b.reshape(b * s, w_dil), gates, x2,
                  w_br_fox.astype(BF16), w_br_dil.astype(BF16), w_out.astype(BF16),
                  g_ffn.reshape(1, d))
    out = _ffn(h2, x1, w_up.astype(BF16), w_conv, b_conv.reshape(1, -1), w_down.astype(BF16),
               seq_len=s)
    return out.reshape(b, s, d)


def kernel(x, g_attn, w_in, b_forget, g_q_fox, g_k_fox, g_q_dil, g_k_dil, w_br_fox, w_br_dil,
           w_out, g_ffn, w_up, w_conv, b_conv, w_down):
    for l in range(w_in.shape[0]):
        x = _layer(x, g_attn[l], w_in[l], b_forget[l], g_q_fox[l], g_k_fox[l], g_q_dil[l],
                   g_k_dil[l], w_br_fox[l], w_br_dil[l], w_out[l], g_ffn[l], w_up[l], w_conv[l],
                   b_conv[l], w_down[l])
    return x
```

```python
import functools

import numpy as np
import jax
import jax.numpy as jnp
from jax import lax
from jax.experimental import pallas as pl
from jax.experimental.pallas import tpu as pltpu

HEAD_DIM = 128
N_HEADS_FOX = 8
N_HEADS_DIL = 8
DIL_PATTERNS = ((128, 1), (512, 4), (2048, 16))
CONV_WIDTH = 3
EPS = 1e-6
NEG_INF = -1e30

F32 = jnp.float32
BF16 = jnp.bfloat16

LANES = 128
SUBLANES = 8
VMEM_LIMIT = 56 * 1024 * 1024

NT_DIMS = (((1,), (1,)), ((), ()))


def _params(sem):
    return pltpu.CompilerParams(dimension_semantics=sem, vmem_limit_bytes=VMEM_LIMIT)


def _in_proj_kernel(x_ref, g_ref, w_ref, wf_ref, gqk_ref, qkv_ref, gate_ref, f_ref, h_scr,
                    *, n_qkv_tiles, tiles_per_group, heads_per_tile):
    j = pl.program_id(1)

    def project(h, w_t):
        return lax.dot_general(h, w_t.astype(BF16), NT_DIMS, preferred_element_type=F32)

    @pl.when(j == 0)
    def _():
        x = x_ref[...]
        ms = jnp.mean(x * x, axis=-1, keepdims=True)
        h = (x * lax.rsqrt(ms + EPS) * g_ref[...]).astype(BF16)
        h_scr[...] = h
        wf = wf_ref[...]
        wf = jnp.concatenate([wf, jnp.zeros((LANES - wf.shape[0], wf.shape[1]), F32)], axis=0)
        f_ref[...] = project(h, wf)

    def store_qk(acc):
        for hh in range(heads_per_tile):
            sl = slice(hh * HEAD_DIM, (hh + 1) * HEAD_DIM)
            a = acc[:, sl]
            ms = jnp.mean(a * a, axis=-1, keepdims=True)
            qkv_ref[:, sl] = (a * lax.rsqrt(ms + EPS) * gqk_ref[:, sl]).astype(BF16)

    def store_v(acc):
        qkv_ref[...] = acc.astype(BF16)

    def store_gate(acc):
        gate_ref[...] = acc

    is_qkv = j < n_qkv_tiles
    is_qk = jnp.logical_and(is_qkv, (j % (3 * tiles_per_group)) < 2 * tiles_per_group)
    cases = (
        (is_qk, store_qk),
        (jnp.logical_and(is_qkv, jnp.logical_not(is_qk)), store_v),
        (jnp.logical_not(is_qkv), store_gate),
    )
    for cond, store in cases:
        pl.when(cond)(lambda store=store: store(project(h_scr[...], w_ref[...])))


def _in_proj(x2, g_attn, w_t, g_qk, *, f_lo, f_hi, n_qkv, n_gate):
    m, d = x2.shape
    tm, tn = 2048, 512
    assert f_lo % tn == 0 and f_hi % SUBLANES == 0
    n_qkv_tiles = n_qkv // tn
    n_tiles = (n_qkv + n_gate) // tn
    kern = functools.partial(
        _in_proj_kernel, n_qkv_tiles=n_qkv_tiles,
        tiles_per_group=(N_HEADS_FOX * HEAD_DIM) // tn, heads_per_tile=tn // HEAD_DIM)
    w_row = lambda j: pl.multiple_of(j * tn + jnp.where(j * tn >= f_lo, f_hi - f_lo, 0), SUBLANES)
    return pl.pallas_call(
        kern,
        grid=(m // tm, n_tiles),
        in_specs=[
            pl.BlockSpec((tm, d), lambda i, j: (i, 0), pipeline_mode=pl.Buffered(1)),
            pl.BlockSpec((1, d), lambda i, j: (0, 0)),
            pl.BlockSpec((pl.Element(tn), pl.Element(d)), lambda i, j: (w_row(j), 0)),
            pl.BlockSpec((pl.Element(f_hi - f_lo), pl.Element(d)), lambda i, j: (f_lo, 0)),
            pl.BlockSpec((1, tn), lambda i, j: (0, jnp.minimum(j, n_qkv_tiles - 1))),
        ],
        out_specs=[
            pl.BlockSpec((tm, tn), lambda i, j: (i, jnp.minimum(j, n_qkv_tiles - 1))),
            pl.BlockSpec((tm, tn), lambda i, j: (i, jnp.maximum(j - n_qkv_tiles, 0))),
            pl.BlockSpec((tm, LANES), lambda i, j: (i, 0)),
        ],
        out_shape=[
            jax.ShapeDtypeStruct((m, n_qkv), BF16),
            jax.ShapeDtypeStruct((m, n_gate), F32),
            jax.ShapeDtypeStruct((m, LANES), F32),
        ],
        scratch_shapes=[pltpu.VMEM((tm, d), BF16)],
        compiler_params=_params(("arbitrary", "arbitrary")),
        name="in_proj",
    )(x2, g_attn, w_t, w_t, g_qk)


def _forget_kernel(f_ref, b_ref, ccol_ref, crow_ref, *, n_heads):
    z = f_ref[0] + b_ref[...]
    c = jax.nn.log_sigmoid(z)
    s = c.shape[0]
    row = lax.broadcasted_iota(jnp.int32, c.shape, 0)
    k = 1
    while k < s:
        c = c + jnp.where(row >= k, pltpu.roll(c, k, axis=0), 0.0)
        k *= 2
    ccol_ref[0] = c
    crow_ref[0] = c.T[:n_heads, :]


def _forget(f_pre, b_pad, *, n_heads):
    b, s, _ = f_pre.shape
    return pl.pallas_call(
        functools.partial(_forget_kernel, n_heads=n_heads),
        grid=(b,),
        in_specs=[
            pl.BlockSpec((1, s, LANES), lambda i: (i, 0, 0)),
            pl.BlockSpec((1, LANES), lambda i: (0, 0)),
        ],
        out_specs=[
            pl.BlockSpec((1, s, LANES), lambda i: (i, 0, 0)),
            pl.BlockSpec((1, n_heads, s), lambda i: (i, 0, 0)),
        ],
        out_shape=[
            jax.ShapeDtypeStruct((b, s, LANES), F32),
            jax.ShapeDtypeStruct((b, n_heads, s), F32),
        ],
        compiler_params=_params(("arbitrary",)),
        name="forget_cumsum",
    )(f_pre, b_pad)


def _fox_kernel(q_ref, k_ref, v_ref, ccol_ref, crow_ref, o_ref, *, tq):
    h = pl.program_id(1)
    s_len = q_ref.shape[1]
    scale = 1.0 / np.sqrt(HEAD_DIM)
    lane = lax.broadcasted_iota(jnp.int32, (s_len, LANES), 1)
    cq_all = jnp.sum(jnp.where(lane == h, ccol_ref[0], 0.0), axis=1, keepdims=True)
    ck_all = crow_ref[0, 0]
    r = lax.broadcasted_iota(jnp.int32, (tq, tq), 0)
    c = lax.broadcasted_iota(jnp.int32, (tq, tq), 1)
    causal = c <= r
    for qi in range(s_len // tq):
        q0, kv = qi * tq, (qi + 1) * tq
        s = lax.dot_general(q_ref[0, q0:kv, :], k_ref[0, :kv, :], NT_DIMS,
                            preferred_element_type=F32) * scale
        s = s + cq_all[q0:kv] - ck_all[:, :kv]
        s_diag = jnp.where(causal, s[:, q0:], NEG_INF)
        s = jnp.concatenate([s[:, :q0], s_diag], axis=1) if qi else s_diag
        m = jnp.max(s, axis=1, keepdims=True)
        p = jnp.exp(s - m)
        l = jnp.sum(p, axis=1, keepdims=True)
        o = jnp.dot(p.astype(BF16), v_ref[0, :kv, :], preferred_element_type=F32) / l
        o_ref[0, q0:kv, :] = o.astype(o_ref.dtype)


def _fox(qkv, ccol, crow4, *, q_off, k_off, v_off, n_heads, tq):
    b, s, _ = qkv.shape
    spec = lambda off: pl.BlockSpec((1, s, HEAD_DIM), lambda bi, h: (bi, 0, off + h))
    return pl.pallas_call(
        functools.partial(_fox_kernel, tq=tq),
        grid=(b, n_heads),
        in_specs=[
            spec(q_off), spec(k_off), spec(v_off),
            pl.BlockSpec((1, s, LANES), lambda bi, h: (bi, 0, 0)),
            pl.BlockSpec((1, 1, 1, s), lambda bi, h: (bi, h, 0, 0)),
        ],
        out_specs=spec(0),
        out_shape=jax.ShapeDtypeStruct((b, s, n_heads * HEAD_DIM), BF16),
        compiler_params=_params(("arbitrary", "arbitrary")),
        name="fox_attention",
    )(qkv, qkv, qkv, ccol, crow4)


def _band_attend(qb, kb, vb, bias):
    scale = 1.0 / np.sqrt(HEAD_DIM)
    s = lax.dot_general(qb, kb, NT_DIMS, preferred_element_type=F32) * scale + bias
    m = jnp.max(s, axis=-1, keepdims=True)
    e = jnp.exp(s - m)
    den = jnp.sum(e, axis=-1, keepdims=True)
    o = jnp.dot(e.astype(BF16), vb, preferred_element_type=F32) / den
    lse = m + jnp.log(den)
    return o, jnp.broadcast_to(lse, o.shape)


def _dilated_kernel(slopes_ref, q_ref, k_ref, v_ref, o_ref, qf, kf, vf, *scratch):
    n_pat = len(DIL_PATTERNS)
    perm = scratch[:3 * n_pat]
    o_scr = scratch[3 * n_pat:4 * n_pat]
    l_scr = scratch[4 * n_pat:5 * n_pat]
    h = pl.program_id(1)
    slope = slopes_ref[h]
    s_len = q_ref.shape[1]
    steps = DIL_PATTERNS[0][0] // DIL_PATTERNS[0][1]
    n_blocks = s_len // steps

    qf[...] = q_ref[0].astype(F32)
    kf[...] = k_ref[0].astype(F32)
    vf[...] = v_ref[0].astype(F32)

    i_idx = lax.broadcasted_iota(jnp.int32, (steps, 2 * steps), 0)
    j_idx = lax.broadcasted_iota(jnp.int32, (steps, 2 * steps), 1)
    dist = i_idx + steps - j_idx
    valid = jnp.logical_and(dist >= 0, dist <= steps)

    for p, (window, dil) in enumerate(DIL_PATTERNS):
        assert window // dil == steps
        bias = jnp.where(valid, -(slope * (dil * dist).astype(F32)), NEG_INF)
        bias_cur = bias[:, steps:]
        sub_len = s_len // dil
        blocks_per_residue = sub_len // steps
        if dil == 1:
            qp, kp, vp = (lambda a, b: q_ref[0, a:b, :]), (lambda a, b: k_ref[0, a:b, :]), \
                         (lambda a, b: v_ref[0, a:b, :])
        else:
            for src, dst in zip((qf, kf, vf), perm[3 * p:3 * p + 3]):
                for res in range(dil):
                    dst[res * sub_len:(res + 1) * sub_len, :] = (
                        src[pl.ds(res, sub_len, stride=dil), :].astype(BF16))
            qp, kp, vp = [(lambda a, b, ref=ref: ref[a:b, :]) for ref in perm[3 * p:3 * p + 3]]

        for nb in range(n_blocks):
            res, n = divmod(nb, blocks_per_residue)
            lo, hi = nb * steps, (nb + 1) * steps
            if n == 0:
                o, lse = _band_attend(qp(lo, hi), kp(lo, hi), vp(lo, hi), bias_cur)
            else:
                o, lse = _band_attend(qp(lo, hi), kp(lo - steps, hi), vp(lo - steps, hi), bias)
            if dil == 1:
                o_scr[p][lo:hi, :] = o
                l_scr[p][lo:hi, :] = lse
            else:
                t0 = res + dil * steps * n
                o_scr[p][pl.ds(t0, steps, stride=dil), :] = o
                l_scr[p][pl.ds(t0, steps, stride=dil), :] = lse

    lses = [ref[...] for ref in l_scr]
    mx = functools.reduce(jnp.maximum, lses)
    ws = [jnp.exp(l - mx) for l in lses]
    num = sum(w * ref[...] for w, ref in zip(ws, o_scr))
    o_ref[0] = (num / sum(ws)).astype(o_ref.dtype)


def _dilated(qkv, slopes, *, q_off, k_off, v_off, n_heads):
    b, s, _ = qkv.shape
    n_pat = len(DIL_PATTERNS)
    spec = lambda off: pl.BlockSpec((1, s, HEAD_DIM), lambda bi, h, sl: (bi, 0, off + h))
    return pl.pallas_call(
        _dilated_kernel,
        grid_spec=pltpu.PrefetchScalarGridSpec(
            num_scalar_prefetch=1,
            grid=(b, n_heads),
            in_specs=[spec(q_off), spec(k_off), spec(v_off)],
            out_specs=spec(0),
            scratch_shapes=([pltpu.VMEM((s, HEAD_DIM), F32)] * 3
                            + [pltpu.VMEM((s, HEAD_DIM), BF16)] * (3 * n_pat)
                            + [pltpu.VMEM((s, HEAD_DIM), F32)] * (2 * n_pat)),
        ),
        out_shape=jax.ShapeDtypeStruct((b, s, n_heads * HEAD_DIM), BF16),
        compiler_params=_params(("arbitrary", "arbitrary")),
        name="dilated_attention",
    )(slopes, qkv, qkv, qkv)


def _mix_kernel(oa_ref, ob_ref, ga_ref, gb_ref, x_ref, wa_ref, wb_ref, wo_ref, g_ref,
                x1_ref, h2_ref):
    ta = jnp.dot(oa_ref[...], wa_ref[...], preferred_element_type=F32)
    tb = jnp.dot(ob_ref[...], wb_ref[...], preferred_element_type=F32)
    merged = (jax.nn.sigmoid(ga_ref[...]) * ta + jax.nn.sigmoid(gb_ref[...]) * tb).astype(BF16)
    x1 = x_ref[...] + jnp.dot(merged, wo_ref[...], preferred_element_type=F32)
    x1_ref[...] = x1
    ms = jnp.mean(x1 * x1, axis=-1, keepdims=True)
    h2_ref[...] = (x1 * lax.rsqrt(ms + EPS) * g_ref[...]).astype(BF16)


def _mix(oa, ob, gates, x2, w_a, w_b, w_o, g_ffn):
    m, d = x2.shape
    wa_rows, wb_rows = w_a.shape[0], w_b.shape[0]
    tm = 256
    resident = lambda shape: pl.BlockSpec(shape, lambda i: (0, 0), pipeline_mode=pl.Buffered(1))
    return pl.pallas_call(
        _mix_kernel,
        grid=(m // tm,),
        in_specs=[
            pl.BlockSpec((tm, wa_rows), lambda i: (i, 0)),
            pl.BlockSpec((tm, wb_rows), lambda i: (i, 0)),
            pl.BlockSpec((tm, d), lambda i: (i, 0)),
            pl.BlockSpec((tm, d), lambda i: (i, 1)),
            pl.BlockSpec((tm, d), lambda i: (i, 0)),
            resident((wa_rows, d)),
            resident((wb_rows, d)),
            resident((d, d)),
            pl.BlockSpec((1, d), lambda i: (0, 0)),
        ],
        out_specs=[
            pl.BlockSpec((tm, d), lambda i: (i, 0)),
            pl.BlockSpec((tm, d), lambda i: (i, 0)),
        ],
        out_shape=[
            jax.ShapeDtypeStruct((m, d), F32),
            jax.ShapeDtypeStruct((m, d), BF16),
        ],
        compiler_params=_params(("arbitrary",)),
        name="branch_mix_out_proj",
    )(oa, ob, gates, gates, x2, w_a, w_b, w_o, g_ffn)


def _ffn_kernel(h_ref, x1_ref, wg_ref, wv_ref, cg_ref, cv_ref, bg_ref, bv_ref, wd_ref,
                out_ref, ug_scr, uv_scr, carry_g, carry_v, *, tm, tiles_per_seq):
    i = pl.program_id(0)
    j = pl.program_id(1)
    first = (i % tiles_per_seq) == 0

    @pl.when(jnp.logical_and(i == 0, j == 0))
    def _():
        carry_g[...] = jnp.zeros(carry_g.shape, F32)
        carry_v[...] = jnp.zeros(carry_v.shape, F32)

    @pl.when(j == 0)
    def _():
        out_ref[...] = x1_ref[...]

    h = h_ref[...]

    def up_conv(w_ref, c_ref, b_ref, u_scr, carry):
        prev = carry[j]
        u_scr[0:SUBLANES, :] = jnp.where(first, jnp.zeros_like(prev), prev)
        u_scr[SUBLANES:, :] = jnp.dot(h, w_ref[...], preferred_element_type=F32)
        carry[j] = u_scr[tm:tm + SUBLANES, :]
        out = b_ref[...]
        for t in range(CONV_WIDTH):
            shift = CONV_WIDTH - 1 - t
            out = out + c_ref[t:t + 1, :] * u_scr[pl.ds(SUBLANES - shift, tm), :]
        return out

    gate = up_conv(wg_ref, cg_ref, bg_ref, ug_scr, carry_g)
    val = up_conv(wv_ref, cv_ref, bv_ref, uv_scr, carry_v)
    a = (gate * jax.nn.sigmoid(gate) * val).astype(BF16)
    out_ref[...] += jnp.dot(a, wd_ref[...], preferred_element_type=F32)


def _ffn(h2, x1, w_up, w_conv, b_conv, w_down, *, seq_len):
    m, d = h2.shape
    d_ff = w_down.shape[0]
    tm, tf = 1024, 512
    nj = d_ff // tf
    kern = functools.partial(_ffn_kernel, tm=tm, tiles_per_seq=seq_len // tm)
    return pl.pallas_call(
        kern,
        grid=(m // tm, nj),
        in_specs=[
            pl.BlockSpec((tm, d), lambda i, j: (i, 0)),
            pl.BlockSpec((tm, d), lambda i, j: (i, 0), pipeline_mode=pl.Buffered(1)),
            pl.BlockSpec((d, tf), lambda i, j: (0, j)),
            pl.BlockSpec((d, tf), lambda i, j: (0, nj + j)),
            pl.BlockSpec((CONV_WIDTH, tf), lambda i, j: (0, j)),
            pl.BlockSpec((CONV_WIDTH, tf), lambda i, j: (0, nj + j)),
            pl.BlockSpec((1, tf), lambda i, j: (0, j)),
            pl.BlockSpec((1, tf), lambda i, j: (0, nj + j)),
            pl.BlockSpec((tf, d), lambda i, j: (j, 0)),
        ],
        out_specs=pl.BlockSpec((tm, d), lambda i, j: (i, 0)),
        out_shape=jax.ShapeDtypeStruct((m, d), F32),
        scratch_shapes=[
            pltpu.VMEM((tm + SUBLANES, tf), F32),
            pltpu.VMEM((tm + SUBLANES, tf), F32),
            pltpu.VMEM((nj, SUBLANES, tf), F32),
            pltpu.VMEM((nj, SUBLANES, tf), F32),
        ],
        compiler_params=_params(("arbitrary", "arbitrary")),
        name="conv_ffn",
    )(h2, x1, w_up, w_up, w_conv, w_conv, b_conv, b_conv, w_down)


def _layer(x, g_attn, w_in, b_forget, g_q_fox, g_k_fox, g_q_dil, g_k_dil,
           w_br_fox, w_br_dil, w_out, g_ffn, w_up, w_conv, b_conv, w_down):
    b, s, d = x.shape
    w_fox = N_HEADS_FOX * HEAD_DIM
    w_dil = N_HEADS_DIL * HEAD_DIM
    n_qkv = 3 * w_fox + 3 * w_dil
    f_lo, f_hi = 3 * w_fox, 3 * w_fox + N_HEADS_FOX

    ones = jnp.ones((w_fox,), F32)
    g_qk = jnp.concatenate([g_q_fox.reshape(-1), g_k_fox.reshape(-1), ones,
                            g_q_dil.reshape(-1), g_k_dil.reshape(-1), ones]).reshape(1, n_qkv)
    b_pad = jnp.pad(b_forget, (0, LANES - N_HEADS_FOX)).reshape(1, LANES)
    slopes = jnp.asarray(2.0 ** (-8.0 * np.arange(1, N_HEADS_DIL + 1) / N_HEADS_DIL), dtype=F32)

    x2 = x.reshape(b * s, d)
    qkv, gates, f_pre = _in_proj(x2, g_attn.reshape(1, d), w_in.T, g_qk,
                                 f_lo=f_lo, f_hi=f_hi, n_qkv=n_qkv, n_gate=2 * d)
    qkv = qkv.reshape(b, s, n_qkv)

    ccol, crow = _forget(f_pre.reshape(b, s, LANES), b_pad, n_heads=N_HEADS_FOX)
    crow4 = crow.reshape(b, N_HEADS_FOX, 1, s)
    o_a = _fox(qkv, ccol, crow4, q_off=0, k_off=N_HEADS_FOX, v_off=2 * N_HEADS_FOX,
               n_heads=N_HEADS_FOX, tq=512)
    base = 3 * N_HEADS_FOX
    o_b = _dilated(qkv, slopes, q_off=base, k_off=base + N_HEADS_DIL,
                   v_off=base + 2 * N_HEADS_DIL, n_heads=N_HEADS_DIL)

    x1, h2 = _mix(o_a.reshape(b * s, w_fox), o_b.reshape(b * s, w_dil), gates, x2,
                  w_br_fox.astype(BF16), w_br_dil.astype(BF16), w_out.astype(BF16),
                  g_ffn.reshape(1, d))
    out = _ffn(h2, x1, w_up.astype(BF16), w_conv, b_conv.reshape(1, -1), w_down.astype(BF16),
               seq_len=s)
    return out.reshape(b, s, d)


def kernel(x, g_attn, w_in, b_forget, g_q_fox, g_k_fox, g_q_dil, g_k_dil, w_br_fox, w_br_dil,
           w_out, g_ffn, w_up, w_conv, b_conv, w_down):
    for l in range(w_in.shape[0]):
        x = _layer(x, g_attn[l], w_in[l], b_forget[l], g_q_fox[l], g_k_fox[l], g_q_dil[l],
                   g_k_dil[l], w_br_fox[l], w_br_dil[l], w_out[l], g_ffn[l], w_up[l], w_conv[l],
                   b_conv[l], w_down[l])
    return x
```

```python
import functools

import numpy as np
import jax
import jax.numpy as jnp
from jax import lax
from jax.experimental import pallas as pl
from jax.experimental.pallas import tpu as pltpu

HEAD_DIM = 128
N_HEADS_FOX = 8
N_HEADS_DIL = 8
DIL_PATTERNS = ((128, 1), (512, 4), (2048, 16))
CONV_WIDTH = 3
EPS = 1e-6
NEG_INF = -1e30
LOG2E = float(np.log2(np.e))

F32 = jnp.float32
BF16 = jnp.bfloat16

LANES = 128
SUBLANES = 8
VMEM_LIMIT = 56 * 1024 * 1024

NT_DIMS = (((1,), (1,)), ((), ()))


def _params(sem):
    return pltpu.CompilerParams(dimension_semantics=sem, vmem_limit_bytes=VMEM_LIMIT)


def _in_proj_kernel(x_ref, g_ref, w_ref, wf_ref, gqk_ref, qkv_ref, gate_ref, f_ref, h_scr,
                    *, n_qkv_tiles, tiles_per_group, heads_per_tile):
    j = pl.program_id(1)

    def project(h, w_t):
        return lax.dot_general(h, w_t.astype(BF16), NT_DIMS, preferred_element_type=F32)

    @pl.when(j == 0)
    def _():
        x = x_ref[...]
        ms = jnp.mean(x * x, axis=-1, keepdims=True)
        h = (x * lax.rsqrt(ms + EPS) * g_ref[...]).astype(BF16)
        h_scr[...] = h
        wf = wf_ref[...]
        wf = jnp.concatenate([wf, jnp.zeros((LANES - wf.shape[0], wf.shape[1]), F32)], axis=0)
        f_ref[...] = project(h, wf)

    def store_qk(acc):
        for hh in range(heads_per_tile):
            sl = slice(hh * HEAD_DIM, (hh + 1) * HEAD_DIM)
            a = acc[:, sl]
            ms = jnp.mean(a * a, axis=-1, keepdims=True)
            qkv_ref[:, sl] = (a * lax.rsqrt(ms + EPS) * gqk_ref[:, sl]).astype(BF16)

    def store_v(acc):
        qkv_ref[...] = acc.astype(BF16)

    def store_gate(acc):
        gate_ref[...] = acc

    is_qkv = j < n_qkv_tiles
    is_qk = jnp.logical_and(is_qkv, (j % (3 * tiles_per_group)) < 2 * tiles_per_group)
    cases = (
        (is_qk, store_qk),
        (jnp.logical_and(is_qkv, jnp.logical_not(is_qk)), store_v),
        (jnp.logical_not(is_qkv), store_gate),
    )
    for cond, store in cases:
        pl.when(cond)(lambda store=store: store(project(h_scr[...], w_ref[...])))


def _in_proj(x2, g_attn, w_t, g_qk, *, f_lo, f_hi, n_qkv, n_gate):
    m, d = x2.shape
    tm, tn = 2048, 512
    assert f_lo % tn == 0 and f_hi % SUBLANES == 0
    n_qkv_tiles = n_qkv // tn
    n_tiles = (n_qkv + n_gate) // tn
    kern = functools.partial(
        _in_proj_kernel, n_qkv_tiles=n_qkv_tiles,
        tiles_per_group=(N_HEADS_FOX * HEAD_DIM) // tn, heads_per_tile=tn // HEAD_DIM)
    w_row = lambda j: pl.multiple_of(j * tn + jnp.where(j * tn >= f_lo, f_hi - f_lo, 0), SUBLANES)
    return pl.pallas_call(
        kern,
        grid=(m // tm, n_tiles),
        in_specs=[
            pl.BlockSpec((tm, d), lambda i, j: (i, 0), pipeline_mode=pl.Buffered(1)),
            pl.BlockSpec((1, d), lambda i, j: (0, 0)),
            pl.BlockSpec((pl.Element(tn), pl.Element(d)), lambda i, j: (w_row(j), 0)),
            pl.BlockSpec((pl.Element(f_hi - f_lo), pl.Element(d)), lambda i, j: (f_lo, 0)),
            pl.BlockSpec((1, tn), lambda i, j: (0, jnp.minimum(j, n_qkv_tiles - 1))),
        ],
        out_specs=[
            pl.BlockSpec((tm, tn), lambda i, j: (i, jnp.minimum(j, n_qkv_tiles - 1))),
            pl.BlockSpec((tm, tn), lambda i, j: (i, jnp.maximum(j - n_qkv_tiles, 0))),
            pl.BlockSpec((tm, LANES), lambda i, j: (i, 0)),
        ],
        out_shape=[
            jax.ShapeDtypeStruct((m, n_qkv), BF16),
            jax.ShapeDtypeStruct((m, n_gate), F32),
            jax.ShapeDtypeStruct((m, LANES), F32),
        ],
        scratch_shapes=[pltpu.VMEM((tm, d), BF16)],
        compiler_params=_params(("arbitrary", "arbitrary")),
        name="in_proj",
    )(x2, g_attn, w_t, w_t, g_qk)


def _forget_kernel(f_ref, b_ref, ccol_ref, crow_ref, *, n_heads):
    z = f_ref[0] + b_ref[...]
    c = jax.nn.log_sigmoid(z)
    s = c.shape[0]
    row = lax.broadcasted_iota(jnp.int32, c.shape, 0)
    k = 1
    while k < s:
        c = c + jnp.where(row >= k, pltpu.roll(c, k, axis=0), 0.0)
        k *= 2
    ccol_ref[0] = c
    crow_ref[0] = c.T[:n_heads, :]


def _forget(f_pre, b_pad, *, n_heads):
    b, s, _ = f_pre.shape
    return pl.pallas_call(
        functools.partial(_forget_kernel, n_heads=n_heads),
        grid=(b,),
        in_specs=[
            pl.BlockSpec((1, s, LANES), lambda i: (i, 0, 0)),
            pl.BlockSpec((1, LANES), lambda i: (0, 0)),
        ],
        out_specs=[
            pl.BlockSpec((1, s, LANES), lambda i: (i, 0, 0)),
            pl.BlockSpec((1, n_heads, s), lambda i: (i, 0, 0)),
        ],
        out_shape=[
            jax.ShapeDtypeStruct((b, s, LANES), F32),
            jax.ShapeDtypeStruct((b, n_heads, s), F32),
        ],
        compiler_params=_params(("arbitrary",)),
        name="forget_cumsum",
    )(f_pre, b_pad)


def _causal_sweep(q_ref, k_ref, v_ref, o_ref, tq, add_bias):
    s_len = q_ref.shape[1]
    n_q = s_len // tq
    c1 = LOG2E / np.sqrt(HEAD_DIM)

    def logits(qi):
        q0, kv = qi * tq, (qi + 1) * tq
        s = lax.dot_general(q_ref[0, q0:kv, :], k_ref[0, :kv, :], NT_DIMS,
                            preferred_element_type=F32) * c1
        tiles = [add_bias(s[:, kj * tq:(kj + 1) * tq], qi, kj) for kj in range(qi + 1)]
        z = jnp.concatenate(tiles, axis=1) if qi else tiles[0]
        return z, jnp.max(z, axis=1, keepdims=True)

    def weigh(qi, z, m):
        q0, kv = qi * tq, (qi + 1) * tq
        p = jnp.exp2(z - m)
        l = jnp.sum(p, axis=1, keepdims=True)
        o = jnp.dot(p.astype(BF16), v_ref[0, :kv, :], preferred_element_type=F32) / l
        o_ref[0, q0:kv, :] = o.astype(o_ref.dtype)

    pending = logits(0)
    for qi in range(1, n_q):
        upcoming = logits(qi)
        weigh(qi - 1, *pending)
        pending = upcoming
    weigh(n_q - 1, *pending)


def _fox_kernel(q_ref, k_ref, v_ref, ccol_ref, crow_ref, o_ref, *, tq):
    h = pl.program_id(1)
    s_len = q_ref.shape[1]
    lane = lax.broadcasted_iota(jnp.int32, (s_len, LANES), 1)
    cq = jnp.sum(jnp.where(lane == h, ccol_ref[0], 0.0), axis=1, keepdims=True) * LOG2E
    ck = crow_ref[0, 0] * LOG2E
    r = lax.broadcasted_iota(jnp.int32, (tq, tq), 0)
    c = lax.broadcasted_iota(jnp.int32, (tq, tq), 1)
    causal = c <= r

    def add_bias(tile, qi, kj):
        z = tile + cq[qi * tq:(qi + 1) * tq] - ck[:, kj * tq:(kj + 1) * tq]
        return jnp.where(causal, z, NEG_INF) if kj == qi else z

    _causal_sweep(q_ref, k_ref, v_ref, o_ref, tq, add_bias)


def _fox(qkv, ccol, crow4, *, q_off, k_off, v_off, n_heads, tq):
    b, s, _ = qkv.shape
    spec = lambda off: pl.BlockSpec((1, s, HEAD_DIM), lambda bi, h: (bi, 0, off + h))
    return pl.pallas_call(
        functools.partial(_fox_kernel, tq=tq),
        grid=(b, n_heads),
        in_specs=[
            spec(q_off), spec(k_off), spec(v_off),
            pl.BlockSpec((1, s, LANES), lambda bi, h: (bi, 0, 0)),
            pl.BlockSpec((1, 1, 1, s), lambda bi, h: (bi, h, 0, 0)),
        ],
        out_specs=spec(0),
        out_shape=jax.ShapeDtypeStruct((b, s, n_heads * HEAD_DIM), BF16),
        compiler_params=_params(("arbitrary", "arbitrary")),
        name="fox_attention",
    )(qkv, qkv, qkv, ccol, crow4)


def _dilated_kernel(slopes_ref, q_ref, k_ref, v_ref, o_ref, bias_scr, *, tq):
    h = pl.program_id(0)

    @pl.when(pl.program_id(1) == 0)
    def _():
        slope = slopes_ref[h]
        r = lax.broadcasted_iota(jnp.int32, (tq, tq), 0)
        c = lax.broadcasted_iota(jnp.int32, (tq, tq), 1)
        for d in range(bias_scr.shape[0]):
            delta = r - c + d * tq
            count = jnp.zeros((tq, tq), F32)
            for window, dil in DIL_PATTERNS:
                assert dil & (dil - 1) == 0
                member = jnp.logical_and((delta & (dil - 1)) == 0, delta <= window)
                count = count + jnp.where(member, 1.0, 0.0)
            valid = jnp.logical_and(delta >= 0, count > 0.0)
            bias = jnp.log(jnp.maximum(count, 1.0)) - slope * delta.astype(F32)
            bias_scr[d] = jnp.where(valid, bias, NEG_INF) * LOG2E

    _causal_sweep(q_ref, k_ref, v_ref, o_ref, tq, lambda tile, qi, kj: tile + bias_scr[qi - kj])


def _dilated(qkv, slopes, *, q_off, k_off, v_off, n_heads, tq):
    b, s, _ = qkv.shape
    spec = lambda off: pl.BlockSpec((1, s, HEAD_DIM), lambda h, bi, sl: (bi, 0, off + h))
    return pl.pallas_call(
        functools.partial(_dilated_kernel, tq=tq),
        grid_spec=pltpu.PrefetchScalarGridSpec(
            num_scalar_prefetch=1,
            grid=(n_heads, b),
            in_specs=[spec(q_off), spec(k_off), spec(v_off)],
            out_specs=spec(0),
            scratch_shapes=[pltpu.VMEM((s // tq, tq, tq), F32)],
        ),
        out_shape=jax.ShapeDtypeStruct((b, s, n_heads * HEAD_DIM), BF16),
        compiler_params=_params(("arbitrary", "arbitrary")),
        name="dilated_attention",
    )(slopes, qkv, qkv, qkv)


def _mix_kernel(oa_ref, ob_ref, ga_ref, gb_ref, x_ref, wa_ref, wb_ref, wo_ref, g_ref,
                x1_ref, h2_ref):
    ta = jnp.dot(oa_ref[...], wa_ref[...], preferred_element_type=F32)
    tb = jnp.dot(ob_ref[...], wb_ref[...], preferred_element_type=F32)
    merged = (jax.nn.sigmoid(ga_ref[...]) * ta + jax.nn.sigmoid(gb_ref[...]) * tb).astype(BF16)
    x1 = x_ref[...] + jnp.dot(merged, wo_ref[...], preferred_element_type=F32)
    x1_ref[...] = x1
    ms = jnp.mean(x1 * x1, axis=-1, keepdims=True)
    h2_ref[...] = (x1 * lax.rsqrt(ms + EPS) * g_ref[...]).astype(BF16)


def _mix(oa, ob, gates, x2, w_a, w_b, w_o, g_ffn):
    m, d = x2.shape
    wa_rows, wb_rows = w_a.shape[0], w_b.shape[0]
    tm = 256
    resident = lambda shape: pl.BlockSpec(shape, lambda i: (0, 0), pipeline_mode=pl.Buffered(1))
    return pl.pallas_call(
        _mix_kernel,
        grid=(m // tm,),
        in_specs=[
            pl.BlockSpec((tm, wa_rows), lambda i: (i, 0)),
            pl.BlockSpec((tm, wb_rows), lambda i: (i, 0)),
            pl.BlockSpec((tm, d), lambda i: (i, 0)),
            pl.BlockSpec((tm, d), lambda i: (i, 1)),
            pl.BlockSpec((tm, d), lambda i: (i, 0)),
            resident((wa_rows, d)),
            resident((wb_rows, d)),
            resident((d, d)),
            pl.BlockSpec((1, d), lambda i: (0, 0)),
        ],
        out_specs=[
            pl.BlockSpec((tm, d), lambda i: (i, 0)),
            pl.BlockSpec((tm, d), lambda i: (i, 0)),
        ],
        out_shape=[
            jax.ShapeDtypeStruct((m, d), F32),
            jax.ShapeDtypeStruct((m, d), BF16),
        ],
        compiler_params=_params(("arbitrary",)),
        name="branch_mix_out_proj",
    )(oa, ob, gates, gates, x2, w_a, w_b, w_o, g_ffn)


def _ffn_kernel(h_ref, x1_ref, wg_ref, wv_ref, cg_ref, cv_ref, bg_ref, bv_ref, wd_ref,
                out_ref, ug_scr, uv_scr, carry_g, carry_v, *, tm, tiles_per_seq):
    i = pl.program_id(0)
    j = pl.program_id(1)
    first = (i % tiles_per_seq) == 0

    @pl.when(jnp.logical_and(i == 0, j == 0))
    def _():
        carry_g[...] = jnp.zeros(carry_g.shape, F32)
        carry_v[...] = jnp.zeros(carry_v.shape, F32)

    @pl.when(j == 0)
    def _():
        out_ref[...] = x1_ref[...]

    h = h_ref[...]

    def up_conv(w_ref, c_ref, b_ref, u_scr, carry):
        prev = carry[j]
        u_scr[0:SUBLANES, :] = jnp.where(first, jnp.zeros_like(prev), prev)
        u_scr[SUBLANES:, :] = jnp.dot(h, w_ref[...], preferred_element_type=F32)
        carry[j] = u_scr[tm:tm + SUBLANES, :]
        out = b_ref[...]
        for t in range(CONV_WIDTH):
            shift = CONV_WIDTH - 1 - t
            out = out + c_ref[t:t + 1, :] * u_scr[pl.ds(SUBLANES - shift, tm), :]
        return out

    gate = up_conv(wg_ref, cg_ref, bg_ref, ug_scr, carry_g)
    val = up_conv(wv_ref, cv_ref, bv_ref, uv_scr, carry_v)
    a = (gate * jax.nn.sigmoid(gate) * val).astype(BF16)
    out_ref[...] += jnp.dot(a, wd_ref[...], preferred_element_type=F32)


def _ffn(h2, x1, w_up, w_conv, b_conv, w_down, *, seq_len):
    m, d = h2.shape
    d_ff = w_down.shape[0]
    tm, tf = 1024, 512
    nj = d_ff // tf
    kern = functools.partial(_ffn_kernel, tm=tm, tiles_per_seq=seq_len // tm)
    return pl.pallas_call(
        kern,
        grid=(m // tm, nj),
        in_specs=[
            pl.BlockSpec((tm, d), lambda i, j: (i, 0)),
            pl.BlockSpec((tm, d), lambda i, j: (i, 0), pipeline_mode=pl.Buffered(1)),
            pl.BlockSpec((d, tf), lambda i, j: (0, j)),
            pl.BlockSpec((d, tf), lambda i, j: (0, nj + j)),
            pl.BlockSpec((CONV_WIDTH, tf), lambda i, j: (0, j)),
            pl.BlockSpec((CONV_WIDTH, tf), lambda i, j: (0, nj + j)),
            pl.BlockSpec((1, tf), lambda i, j: (0, j)),
            pl.BlockSpec((1, tf), lambda i, j: (0, nj + j)),
            pl.BlockSpec((tf, d), lambda i, j: (j, 0)),
        ],
        out_specs=pl.BlockSpec((tm, d), lambda i, j: (i, 0)),
        out_shape=jax.ShapeDtypeStruct((m, d), F32),
        scratch_shapes=[
            pltpu.VMEM((tm + SUBLANES, tf), F32),
            pltpu.VMEM((tm + SUBLANES, tf), F32),
            pltpu.VMEM((nj, SUBLANES, tf), F32),
            pltpu.VMEM((nj, SUBLANES, tf), F32),
        ],
        compiler_params=_params(("arbitrary", "arbitrary")),
        name="conv_ffn",
    )(h2, x1, w_up, w_up, w_conv, w_conv, b_conv, b_conv, w_down)


def _layer(x, g_attn, w_in, b_forget, g_q_fox, g_k_fox, g_q_dil, g_k_dil,
           w_br_fox, w_br_dil, w_out, g_ffn, w_up, w_conv, b_conv, w_down):
    b, s, d = x.shape
    w_fox = N_HEADS_FOX * HEAD_DIM
    w_dil = N_HEADS_DIL * HEAD_DIM
    n_qkv = 3 * w_fox + 3 * w_dil
    f_lo, f_hi = 3 * w_fox, 3 * w_fox + N_HEADS_FOX

    ones = jnp.ones((w_fox,), F32)
    g_qk = jnp.concatenate([g_q_fox.reshape(-1), g_k_fox.reshape(-1), ones,
                            g_q_dil.reshape(-1), g_k_dil.reshape(-1), ones]).reshape(1, n_qkv)
    b_pad = jnp.pad(b_forget, (0, LANES - N_HEADS_FOX)).reshape(1, LANES)
    slopes = jnp.asarray(2.0 ** (-8.0 * np.arange(1, N_HEADS_DIL + 1) / N_HEADS_DIL), dtype=F32)

    x2 = x.reshape(b * s, d)
    qkv, gates, f_pre = _in_proj(x2, g_attn.reshape(1, d), w_in.T, g_qk,
                                 f_lo=f_lo, f_hi=f_hi, n_qkv=n_qkv, n_gate=2 * d)
    qkv = qkv.reshape(b, s, n_qkv)

    ccol, crow = _forget(f_pre.reshape(b, s, LANES), b_pad, n_heads=N_HEADS_FOX)
    crow4 = crow.reshape(b, N_HEADS_FOX, 1, s)
    o_a = _fox(qkv, ccol, crow4, q_off=0, k_off=N_HEADS_FOX, v_off=2 * N_HEADS_FOX,
               n_heads=N_HEADS_FOX, tq=256)
    base = 3 * N_HEADS_FOX
    o_b = _dilated(qkv, slopes, q_off=base, k_off=base + N_HEADS_DIL,
                   v_off=base + 2 * N_HEADS_DIL, n_heads=N_HEADS_DIL, tq=256)

    x1, h2 = _mix(o_a.reshape(b * s, w_fox), o_b.reshape(b * s, w_dil), gates, x2,
                  w_br_fox.astype(BF16), w_br_dil.astype(BF16), w_out.astype(BF16),
                  g_ffn.reshape(1, d))
    out = _ffn(h2, x1, w_up.astype(BF16), w_conv, b_conv.reshape(1, -1), w_down.astype(BF16),
               seq_len=s)
    return out.reshape(b, s, d)


def kernel(x, g_attn, w_in, b_forget, g_q_fox, g_k_fox, g_q_dil, g_k_dil, w_br_fox, w_br_dil,
           w_out, g_ffn, w_up, w_conv, b_conv, w_down):
    for l in range(w_in.shape[0]):
        x = _layer(x, g_attn[l], w_in[l], b_forget[l], g_q_fox[l], g_k_fox[l], g_q_dil[l],
                   g_k_dil[l], w_br_fox[l], w_br_dil[l], w_out[l], g_ffn[l], w_up[l], w_conv[l],
                   b_conv[l], w_down[l])
    return x
```

```python
import functools

import numpy as np
import jax
import jax.numpy as jnp
from jax import lax
from jax.experimental import pallas as pl
from jax.experimental.pallas import tpu as pltpu

HEAD_DIM = 128
N_HEADS_FOX = 8
N_HEADS_DIL = 8
DIL_PATTERNS = ((128, 1), (512, 4), (2048, 16))
CONV_WIDTH = 3
EPS = 1e-6
NEG_INF = -1e30
LOG2E = float(np.log2(np.e))

F32 = jnp.float32
BF16 = jnp.bfloat16

LANES = 128
SUBLANES = 8
BF16_ROWS = 2 * SUBLANES
VMEM_LIMIT = 56 * 1024 * 1024

NT_DIMS = (((1,), (1,)), ((), ()))


def _params(sem):
    return pltpu.CompilerParams(dimension_semantics=sem, vmem_limit_bytes=VMEM_LIMIT)


def _in_proj_kernel(x_ref, g_ref, w_ref, wf_ref, gqk_ref, qkv_ref, gate_ref, f_ref, h_scr,
                    *, n_qkv_tiles, tiles_per_group, heads_per_tile):
    j = pl.program_id(1)

    def project(h, w_t):
        return lax.dot_general(h, w_t.astype(BF16), NT_DIMS, preferred_element_type=F32)

    @pl.when(j == 0)
    def _():
        x = x_ref[...]
        ms = jnp.mean(x * x, axis=-1, keepdims=True)
        h = (x * lax.rsqrt(ms + EPS) * g_ref[...]).astype(BF16)
        h_scr[...] = h
        wf = wf_ref[...]
        wf = jnp.concatenate([wf, jnp.zeros((LANES - wf.shape[0], wf.shape[1]), F32)], axis=0)
        f_ref[...] = project(h, wf)

    def store_qk(acc):
        for hh in range(heads_per_tile):
            sl = slice(hh * HEAD_DIM, (hh + 1) * HEAD_DIM)
            a = acc[:, sl]
            ms = jnp.mean(a * a, axis=-1, keepdims=True)
            qkv_ref[:, sl] = (a * lax.rsqrt(ms + EPS) * gqk_ref[:, sl]).astype(BF16)

    def store_v(acc):
        qkv_ref[...] = acc.astype(BF16)

    def store_gate(acc):
        gate_ref[...] = acc

    is_qkv = j < n_qkv_tiles
    is_qk = jnp.logical_and(is_qkv, (j % (3 * tiles_per_group)) < 2 * tiles_per_group)
    cases = (
        (is_qk, store_qk),
        (jnp.logical_and(is_qkv, jnp.logical_not(is_qk)), store_v),
        (jnp.logical_not(is_qkv), store_gate),
    )
    for cond, store in cases:
        pl.when(cond)(lambda store=store: store(project(h_scr[...], w_ref[...])))


def _in_proj(x2, g_attn, w_t, g_qk, *, f_lo, f_hi, n_qkv, n_gate):
    m, d = x2.shape
    tm, tn = 1024, 1024
    assert f_lo % tn == 0 and f_hi % SUBLANES == 0
    n_qkv_tiles = n_qkv // tn
    n_tiles = (n_qkv + n_gate) // tn
    kern = functools.partial(
        _in_proj_kernel, n_qkv_tiles=n_qkv_tiles,
        tiles_per_group=(N_HEADS_FOX * HEAD_DIM) // tn, heads_per_tile=tn // HEAD_DIM)
    w_row = lambda j: pl.multiple_of(j * tn + jnp.where(j * tn >= f_lo, f_hi - f_lo, 0), SUBLANES)
    return pl.pallas_call(
        kern,
        grid=(m // tm, n_tiles),
        in_specs=[
            pl.BlockSpec((tm, d), lambda i, j: (i, 0), pipeline_mode=pl.Buffered(1)),
            pl.BlockSpec((1, d), lambda i, j: (0, 0)),
            pl.BlockSpec((pl.Element(tn), pl.Element(d)), lambda i, j: (w_row(j), 0)),
            pl.BlockSpec((pl.Element(f_hi - f_lo), pl.Element(d)), lambda i, j: (f_lo, 0)),
            pl.BlockSpec((1, tn), lambda i, j: (0, jnp.minimum(j, n_qkv_tiles - 1))),
        ],
        out_specs=[
            pl.BlockSpec((tm, tn), lambda i, j: (i, jnp.minimum(j, n_qkv_tiles - 1))),
            pl.BlockSpec((tm, tn), lambda i, j: (i, jnp.maximum(j - n_qkv_tiles, 0))),
            pl.BlockSpec((tm, LANES), lambda i, j: (i, 0)),
        ],
        out_shape=[
            jax.ShapeDtypeStruct((m, n_qkv), BF16),
            jax.ShapeDtypeStruct((m, n_gate), F32),
            jax.ShapeDtypeStruct((m, LANES), F32),
        ],
        scratch_shapes=[pltpu.VMEM((tm, d), BF16)],
        compiler_params=_params(("arbitrary", "arbitrary")),
        name="in_proj",
    )(x2, g_attn, w_t, w_t, g_qk)


def _forget_kernel(f_ref, b_ref, ccol_ref, crow_ref, *, n_heads):
    z = f_ref[0] + b_ref[...]
    c = jax.nn.log_sigmoid(z)
    s = c.shape[0]
    row = lax.broadcasted_iota(jnp.int32, c.shape, 0)
    k = 1
    while k < s:
        c = c + jnp.where(row >= k, pltpu.roll(c, k, axis=0), 0.0)
        k *= 2
    ccol_ref[0] = c
    crow_ref[0] = c.T[:n_heads, :]


def _forget(f_pre, b_pad, *, n_heads):
    b, s, _ = f_pre.shape
    return pl.pallas_call(
        functools.partial(_forget_kernel, n_heads=n_heads),
        grid=(b,),
        in_specs=[
            pl.BlockSpec((1, s, LANES), lambda i: (i, 0, 0)),
            pl.BlockSpec((1, LANES), lambda i: (0, 0)),
        ],
        out_specs=[
            pl.BlockSpec((1, s, LANES), lambda i: (i, 0, 0)),
            pl.BlockSpec((1, n_heads, s), lambda i: (i, 0, 0)),
        ],
        out_shape=[
            jax.ShapeDtypeStruct((b, s, LANES), F32),
            jax.ShapeDtypeStruct((b, n_heads, s), F32),
        ],
        compiler_params=_params(("arbitrary",)),
        name="forget_cumsum",
    )(f_pre, b_pad)


def _causal_sweep(q_ref, k_ref, v_ref, o_ref, tq, add_bias):
    s_len = q_ref.shape[1]
    n_q = s_len // tq
    c1 = LOG2E / np.sqrt(HEAD_DIM)

    def logits(qi):
        q0, kv = qi * tq, (qi + 1) * tq
        s = lax.dot_general(q_ref[0, q0:kv, :], k_ref[0, :kv, :], NT_DIMS,
                            preferred_element_type=F32) * c1
        tiles = [add_bias(s[:, kj * tq:(kj + 1) * tq], qi, kj) for kj in range(qi + 1)]
        z = jnp.concatenate(tiles, axis=1) if qi else tiles[0]
        return z, jnp.max(z, axis=1, keepdims=True)

    def weigh(qi, z, m):
        q0, kv = qi * tq, (qi + 1) * tq
        p = jnp.exp2(z - m)
        l = jnp.sum(p, axis=1, keepdims=True)
        o = jnp.dot(p.astype(BF16), v_ref[0, :kv, :], preferred_element_type=F32) / l
        o_ref[0, q0:kv, :] = o.astype(o_ref.dtype)

    pending = logits(0)
    for qi in range(1, n_q):
        upcoming = logits(qi)
        weigh(qi - 1, *pending)
        pending = upcoming
    weigh(n_q - 1, *pending)


def _cast_specs(weights, n_steps, step_index):
    in_specs, out_specs, out_shapes = [], [], []
    for w in weights:
        rows, cols = w.shape[0] // n_steps, w.shape[1]
        assert rows * n_steps == w.shape[0] and rows % BF16_ROWS == 0
        spec = pl.BlockSpec((rows, cols), lambda *idx: (step_index(*idx), 0))
        in_specs.append(spec)
        out_specs.append(spec)
        out_shapes.append(jax.ShapeDtypeStruct(w.shape, BF16))
    return in_specs, out_specs, out_shapes


def _cast_rows(in_refs, out_refs):
    for i_ref, o_ref in zip(in_refs, out_refs):
        o_ref[...] = i_ref[...].astype(BF16)


def _fox_kernel(q_ref, k_ref, v_ref, ccol_ref, crow_ref, *rest, tq):
    n_cast = len(rest) // 2
    o_ref = rest[n_cast]
    _cast_rows(rest[:n_cast], rest[n_cast + 1:])
    h = pl.program_id(1)
    s_len = q_ref.shape[1]
    lane = lax.broadcasted_iota(jnp.int32, (s_len, LANES), 1)
    cq = jnp.sum(jnp.where(lane == h, ccol_ref[0], 0.0), axis=1, keepdims=True) * LOG2E
    ck = crow_ref[0, 0] * LOG2E
    r = lax.broadcasted_iota(jnp.int32, (tq, tq), 0)
    c = lax.broadcasted_iota(jnp.int32, (tq, tq), 1)
    causal = c <= r

    def add_bias(tile, qi, kj):
        z = tile + cq[qi * tq:(qi + 1) * tq] - ck[:, kj * tq:(kj + 1) * tq]
        return jnp.where(causal, z, NEG_INF) if kj == qi else z

    _causal_sweep(q_ref, k_ref, v_ref, o_ref, tq, add_bias)


def _fox(qkv, ccol, crow4, cast_weights, *, q_off, k_off, v_off, n_heads, tq):
    b, s, _ = qkv.shape
    spec = lambda off: pl.BlockSpec((1, s, HEAD_DIM), lambda bi, h: (bi, 0, off + h))
    c_in, c_out, c_shapes = _cast_specs(cast_weights, b * n_heads, lambda bi, h: bi * n_heads + h)
    return pl.pallas_call(
        functools.partial(_fox_kernel, tq=tq),
        grid=(b, n_heads),
        in_specs=[
            spec(q_off), spec(k_off), spec(v_off),
            pl.BlockSpec((1, s, LANES), lambda bi, h: (bi, 0, 0)),
            pl.BlockSpec((1, 1, 1, s), lambda bi, h: (bi, h, 0, 0)),
        ] + c_in,
        out_specs=[spec(0)] + c_out,
        out_shape=[jax.ShapeDtypeStruct((b, s, n_heads * HEAD_DIM), BF16)] + c_shapes,
        compiler_params=_params(("arbitrary", "arbitrary")),
        name="fox_attention",
    )(qkv, qkv, qkv, ccol, crow4, *cast_weights)


def _dilated_kernel(slopes_ref, q_ref, k_ref, v_ref, *rest, tq):
    n_cast = (len(rest) - 2) // 2
    o_ref, bias_scr = rest[n_cast], rest[-1]
    _cast_rows(rest[:n_cast], rest[n_cast + 1:-1])
    h = pl.program_id(0)

    @pl.when(pl.program_id(1) == 0)
    def _():
        slope = slopes_ref[h]
        r = lax.broadcasted_iota(jnp.int32, (tq, tq), 0)
        c = lax.broadcasted_iota(jnp.int32, (tq, tq), 1)
        for d in range(bias_scr.shape[0]):
            delta = r - c + d * tq
            count = jnp.zeros((tq, tq), F32)
            for window, dil in DIL_PATTERNS:
                assert dil & (dil - 1) == 0
                member = jnp.logical_and((delta & (dil - 1)) == 0, delta <= window)
                count = count + jnp.where(member, 1.0, 0.0)
            valid = jnp.logical_and(delta >= 0, count > 0.0)
            bias = jnp.log(jnp.maximum(count, 1.0)) - slope * delta.astype(F32)
            bias_scr[d] = jnp.where(valid, bias, NEG_INF) * LOG2E

    _causal_sweep(q_ref, k_ref, v_ref, o_ref, tq, lambda tile, qi, kj: tile + bias_scr[qi - kj])


def _dilated(qkv, slopes, cast_weights, *, q_off, k_off, v_off, n_heads, tq):
    b, s, _ = qkv.shape
    spec = lambda off: pl.BlockSpec((1, s, HEAD_DIM), lambda h, bi, sl: (bi, 0, off + h))
    c_in, c_out, c_shapes = _cast_specs(cast_weights, b * n_heads, lambda h, bi, sl: h * b + bi)
    return pl.pallas_call(
        functools.partial(_dilated_kernel, tq=tq),
        grid_spec=pltpu.PrefetchScalarGridSpec(
            num_scalar_prefetch=1,
            grid=(n_heads, b),
            in_specs=[spec(q_off), spec(k_off), spec(v_off)] + c_in,
            out_specs=[spec(0)] + c_out,
            scratch_shapes=[pltpu.VMEM((s // tq, tq, tq), F32)],
        ),
        out_shape=[jax.ShapeDtypeStruct((b, s, n_heads * HEAD_DIM), BF16)] + c_shapes,
        compiler_params=_params(("arbitrary", "arbitrary")),
        name="dilated_attention",
    )(slopes, qkv, qkv, qkv, *cast_weights)


def _mix_kernel(oa_ref, ob_ref, ga_ref, gb_ref, x_ref, wa_ref, wb_ref, wo_ref, g_ref,
                x1_ref, h2_ref):
    ta = jnp.dot(oa_ref[...], wa_ref[...], preferred_element_type=F32)
    tb = jnp.dot(ob_ref[...], wb_ref[...], preferred_element_type=F32)
    merged = (jax.nn.sigmoid(ga_ref[...]) * ta + jax.nn.sigmoid(gb_ref[...]) * tb).astype(BF16)
    x1 = x_ref[...] + jnp.dot(merged, wo_ref[...], preferred_element_type=F32)
    x1_ref[...] = x1
    ms = jnp.mean(x1 * x1, axis=-1, keepdims=True)
    h2_ref[...] = (x1 * lax.rsqrt(ms + EPS) * g_ref[...]).astype(BF16)


def _mix(oa, ob, gates, x2, w_a, w_b, w_o, g_ffn):
    m, d = x2.shape
    wa_rows, wb_rows = w_a.shape[0], w_b.shape[0]
    tm = 256
    resident = lambda shape: pl.BlockSpec(shape, lambda i: (0, 0), pipeline_mode=pl.Buffered(1))
    return pl.pallas_call(
        _mix_kernel,
        grid=(m // tm,),
        in_specs=[
            pl.BlockSpec((tm, wa_rows), lambda i: (i, 0)),
            pl.BlockSpec((tm, wb_rows), lambda i: (i, 0)),
            pl.BlockSpec((tm, d), lambda i: (i, 0)),
            pl.BlockSpec((tm, d), lambda i: (i, 1)),
            pl.BlockSpec((tm, d), lambda i: (i, 0)),
            resident((wa_rows, d)),
            resident((wb_rows, d)),
            resident((d, d)),
            pl.BlockSpec((1, d), lambda i: (0, 0)),
        ],
        out_specs=[
            pl.BlockSpec((tm, d), lambda i: (i, 0)),
            pl.BlockSpec((tm, d), lambda i: (i, 0)),
        ],
        out_shape=[
            jax.ShapeDtypeStruct((m, d), F32),
            jax.ShapeDtypeStruct((m, d), BF16),
        ],
        compiler_params=_params(("arbitrary",)),
        name="branch_mix_out_proj",
    )(oa, ob, gates, gates, x2, w_a, w_b, w_o, g_ffn)


def _ffn_kernel(h_ref, x1_ref, wg_ref, wv_ref, cg_ref, cv_ref, bg_ref, bv_ref, wd_ref,
                out_ref, ug_scr, uv_scr, carry_g, carry_v, *, tm, tiles_per_seq):
    i = pl.program_id(0)
    j = pl.program_id(1)
    first = (i % tiles_per_seq) == 0

    @pl.when(jnp.logical_and(i == 0, j == 0))
    def _():
        carry_g[...] = jnp.zeros(carry_g.shape, F32)
        carry_v[...] = jnp.zeros(carry_v.shape, F32)

    @pl.when(j == 0)
    def _():
        out_ref[...] = x1_ref[...]

    h = h_ref[...]

    def up_conv(w_ref, c_ref, b_ref, u_scr, carry):
        prev = carry[j]
        u_scr[0:SUBLANES, :] = jnp.where(first, jnp.zeros_like(prev), prev)
        u_scr[SUBLANES:, :] = jnp.dot(h, w_ref[...], preferred_element_type=F32)
        carry[j] = u_scr[tm:tm + SUBLANES, :]
        out = b_ref[...]
        for t in range(CONV_WIDTH):
            shift = CONV_WIDTH - 1 - t
            out = out + c_ref[t:t + 1, :] * u_scr[pl.ds(SUBLANES - shift, tm), :]
        return out

    gate = up_conv(wg_ref, cg_ref, bg_ref, ug_scr, carry_g)
    val = up_conv(wv_ref, cv_ref, bv_ref, uv_scr, carry_v)
    a = (gate * jax.nn.sigmoid(gate) * val).astype(BF16)
    out_ref[...] += jnp.dot(a, wd_ref[...], preferred_element_type=F32)


def _ffn(h2, x1, w_up, w_conv, b_conv, w_down, *, seq_len):
    m, d = h2.shape
    d_ff = w_down.shape[0]
    tm, tf = 1024, 512
    nj = d_ff // tf
    kern = functools.partial(_ffn_kernel, tm=tm, tiles_per_seq=seq_len // tm)
    return pl.pallas_call(
        kern,
        grid=(m // tm, nj),
        in_specs=[
            pl.BlockSpec((tm, d), lambda i, j: (i, 0)),
            pl.BlockSpec((tm, d), lambda i, j: (i, 0), pipeline_mode=pl.Buffered(1)),
            pl.BlockSpec((d, tf), lambda i, j: (0, j)),
            pl.BlockSpec((d, tf), lambda i, j: (0, nj + j)),
            pl.BlockSpec((CONV_WIDTH, tf), lambda i, j: (0, j)),
            pl.BlockSpec((CONV_WIDTH, tf), lambda i, j: (0, nj + j)),
            pl.BlockSpec((1, tf), lambda i, j: (0, j)),
            pl.BlockSpec((1, tf), lambda i, j: (0, nj + j)),
            pl.BlockSpec((tf, d), lambda i, j: (j, 0)),
        ],
        out_specs=pl.BlockSpec((tm, d), lambda i, j: (i, 0)),
        out_shape=jax.ShapeDtypeStruct((m, d), F32),
        scratch_shapes=[
            pltpu.VMEM((tm + SUBLANES, tf), F32),
            pltpu.VMEM((tm + SUBLANES, tf), F32),
            pltpu.VMEM((nj, SUBLANES, tf), F32),
            pltpu.VMEM((nj, SUBLANES, tf), F32),
        ],
        compiler_params=_params(("arbitrary", "arbitrary")),
        name="conv_ffn",
    )(h2, x1, w_up, w_up, w_conv, w_conv, b_conv, b_conv, w_down)


def _layer(x, g_attn, w_in, b_forget, g_q_fox, g_k_fox, g_q_dil, g_k_dil,
           w_br_fox, w_br_dil, w_out, g_ffn, w_up, w_conv, b_conv, w_down):
    b, s, d = x.shape
    w_fox = N_HEADS_FOX * HEAD_DIM
    w_dil = N_HEADS_DIL * HEAD_DIM
    n_qkv = 3 * w_fox + 3 * w_dil
    f_lo, f_hi = 3 * w_fox, 3 * w_fox + N_HEADS_FOX

    ones = jnp.ones((w_fox,), F32)
    g_qk = jnp.concatenate([g_q_fox.reshape(-1), g_k_fox.reshape(-1), ones,
                            g_q_dil.reshape(-1), g_k_dil.reshape(-1), ones]).reshape(1, n_qkv)
    b_pad = jnp.pad(b_forget, (0, LANES - N_HEADS_FOX)).reshape(1, LANES)
    slopes = jnp.asarray(2.0 ** (-8.0 * np.arange(1, N_HEADS_DIL + 1) / N_HEADS_DIL), dtype=F32)

    x2 = x.reshape(b * s, d)
    qkv, gates, f_pre = _in_proj(x2, g_attn.reshape(1, d), w_in.T, g_qk,
                                 f_lo=f_lo, f_hi=f_hi, n_qkv=n_qkv, n_gate=2 * d)
    qkv = qkv.reshape(b, s, n_qkv)

    ccol, crow = _forget(f_pre.reshape(b, s, LANES), b_pad, n_heads=N_HEADS_FOX)
    crow4 = crow.reshape(b, N_HEADS_FOX, 1, s)
    o_a, w_up_b = _fox(qkv, ccol, crow4, [w_up], q_off=0, k_off=N_HEADS_FOX,
                       v_off=2 * N_HEADS_FOX, n_heads=N_HEADS_FOX, tq=256)
    base = 3 * N_HEADS_FOX
    o_b, w_down_b, w_out_b, w_br_fox_b, w_br_dil_b = _dilated(
        qkv, slopes, [w_down, w_out, w_br_fox, w_br_dil], q_off=base, k_off=base + N_HEADS_DIL,
        v_off=base + 2 * N_HEADS_DIL, n_heads=N_HEADS_DIL, tq=256)

    x1, h2 = _mix(o_a.reshape(b * s, w_fox), o_b.reshape(b * s, w_dil), gates, x2,
                  w_br_fox_b, w_br_dil_b, w_out_b, g_ffn.reshape(1, d))
    out = _ffn(h2, x1, w_up_b, w_conv, b_conv.reshape(1, -1), w_down_b, seq_len=s)
    return out.reshape(b, s, d)


def kernel(x, g_attn, w_in, b_forget, g_q_fox, g_k_fox, g_q_dil, g_k_dil, w_br_fox, w_br_dil,
           w_out, g_ffn, w_up, w_conv, b_conv, w_down):
    for l in range(w_in.shape[0]):
        x = _layer(x, g_attn[l], w_in[l], b_forget[l], g_q_fox[l], g_k_fox[l], g_q_dil[l],
                   g_k_dil[l], w_br_fox[l], w_br_dil[l], w_out[l], g_ffn[l], w_up[l], w_conv[l],
                   b_conv[l], w_down[l])
    return x
```

```python
import functools

import numpy as np
import jax
import jax.numpy as jnp
from jax import lax
from jax.experimental import pallas as pl
from jax.experimental.pallas import tpu as pltpu

HEAD_DIM = 128
N_HEADS_FOX = 8
N_HEADS_DIL = 8
DIL_PATTERNS = ((128, 1), (512, 4), (2048, 16))
CONV_WIDTH = 3
EPS = 1e-6
NEG_INF = -1e30
LOG2E = float(np.log2(np.e))

F32 = jnp.float32
BF16 = jnp.bfloat16

LANES = 128
SUBLANES = 8
BF16_ROWS = 2 * SUBLANES
VMEM_LIMIT = 56 * 1024 * 1024

NT_DIMS = (((1,), (1,)), ((), ()))


def _params(sem, vmem_limit=VMEM_LIMIT):
    return pltpu.CompilerParams(dimension_semantics=sem, vmem_limit_bytes=vmem_limit)


def _in_proj_kernel(x_ref, g_ref, w_ref, wf_ref, gqk_ref, qkv_ref, gate_ref, f_ref, h_scr,
                    *, n_qkv_tiles, tiles_per_group, heads_per_tile):
    j = pl.program_id(1)

    def project(h, w_t):
        return lax.dot_general(h, w_t.astype(BF16), NT_DIMS, preferred_element_type=F32)

    @pl.when(j == 0)
    def _():
        x = x_ref[...]
        ms = jnp.mean(x * x, axis=-1, keepdims=True)
        h = (x * lax.rsqrt(ms + EPS) * g_ref[...]).astype(BF16)
        h_scr[...] = h
        wf = wf_ref[...]
        wf = jnp.concatenate([wf, jnp.zeros((LANES - wf.shape[0], wf.shape[1]), F32)], axis=0)
        f_ref[...] = project(h, wf)

    def store_qk(acc):
        for hh in range(heads_per_tile):
            sl = slice(hh * HEAD_DIM, (hh + 1) * HEAD_DIM)
            a = acc[:, sl]
            ms = jnp.mean(a * a, axis=-1, keepdims=True)
            qkv_ref[:, sl] = (a * lax.rsqrt(ms + EPS) * gqk_ref[:, sl]).astype(BF16)

    def store_v(acc):
        qkv_ref[...] = acc.astype(BF16)

    def store_gate(acc):
        gate_ref[...] = acc

    is_qkv = j < n_qkv_tiles
    is_qk = jnp.logical_and(is_qkv, (j % (3 * tiles_per_group)) < 2 * tiles_per_group)
    cases = (
        (is_qk, store_qk),
        (jnp.logical_and(is_qkv, jnp.logical_not(is_qk)), store_v),
        (jnp.logical_not(is_qkv), store_gate),
    )
    for cond, store in cases:
        pl.when(cond)(lambda store=store: store(project(h_scr[...], w_ref[...])))


def _in_proj(x2, g_attn, w_t, g_qk, *, f_lo, f_hi, n_qkv, n_gate):
    m, d = x2.shape
    tm, tn = 1024, 1024
    assert f_lo % tn == 0 and f_hi % SUBLANES == 0
    n_qkv_tiles = n_qkv // tn
    n_tiles = (n_qkv + n_gate) // tn
    kern = functools.partial(
        _in_proj_kernel, n_qkv_tiles=n_qkv_tiles,
        tiles_per_group=(N_HEADS_FOX * HEAD_DIM) // tn, heads_per_tile=tn // HEAD_DIM)
    w_row = lambda j: pl.multiple_of(j * tn + jnp.where(j * tn >= f_lo, f_hi - f_lo, 0), SUBLANES)
    return pl.pallas_call(
        kern,
        grid=(m // tm, n_tiles),
        in_specs=[
            pl.BlockSpec((tm, d), lambda i, j: (i, 0)),
            pl.BlockSpec((1, d), lambda i, j: (0, 0)),
            pl.BlockSpec((pl.Element(tn), pl.Element(d)), lambda i, j: (w_row(j), 0)),
            pl.BlockSpec((pl.Element(f_hi - f_lo), pl.Element(d)), lambda i, j: (f_lo, 0)),
            pl.BlockSpec((1, tn), lambda i, j: (0, jnp.minimum(j, n_qkv_tiles - 1))),
        ],
        out_specs=[
            pl.BlockSpec((tm, tn), lambda i, j: (i, jnp.minimum(j, n_qkv_tiles - 1))),
            pl.BlockSpec((tm, tn), lambda i, j: (i, jnp.maximum(j - n_qkv_tiles, 0))),
            pl.BlockSpec((tm, LANES), lambda i, j: (i, 0)),
        ],
        out_shape=[
            jax.ShapeDtypeStruct((m, n_qkv), BF16),
            jax.ShapeDtypeStruct((m, n_gate), F32),
            jax.ShapeDtypeStruct((m, LANES), F32),
        ],
        scratch_shapes=[pltpu.VMEM((tm, d), BF16)],
        compiler_params=_params(("arbitrary", "arbitrary")),
        name="in_proj",
    )(x2, g_attn, w_t, w_t, g_qk)


def _forget_kernel(f_ref, b_ref, ccol_ref, crow_ref, *, n_heads):
    z = f_ref[0] + b_ref[...]
    c = jax.nn.log_sigmoid(z)
    s = c.shape[0]
    row = lax.broadcasted_iota(jnp.int32, c.shape, 0)
    k = 1
    while k < s:
        c = c + jnp.where(row >= k, pltpu.roll(c, k, axis=0), 0.0)
        k *= 2
    ccol_ref[0] = c
    crow_ref[0] = c.T[:n_heads, :]


def _forget(f_pre, b_pad, *, n_heads):
    b, s, _ = f_pre.shape
    return pl.pallas_call(
        functools.partial(_forget_kernel, n_heads=n_heads),
        grid=(b,),
        in_specs=[
            pl.BlockSpec((1, s, LANES), lambda i: (i, 0, 0)),
            pl.BlockSpec((1, LANES), lambda i: (0, 0)),
        ],
        out_specs=[
            pl.BlockSpec((1, s, LANES), lambda i: (i, 0, 0)),
            pl.BlockSpec((1, n_heads, s), lambda i: (i, 0, 0)),
        ],
        out_shape=[
            jax.ShapeDtypeStruct((b, s, LANES), F32),
            jax.ShapeDtypeStruct((b, n_heads, s), F32),
        ],
        compiler_params=_params(("arbitrary",)),
        name="forget_cumsum",
    )(f_pre, b_pad)


def _causal_sweep(q_ref, k_ref, v_ref, o_ref, tq, add_bias):
    s_len = q_ref.shape[1]
    n_q = s_len // tq
    c1 = LOG2E / np.sqrt(HEAD_DIM)

    def logits(qi):
        q0, kv = qi * tq, (qi + 1) * tq
        s = lax.dot_general(q_ref[0, q0:kv, :], k_ref[0, :kv, :], NT_DIMS,
                            preferred_element_type=F32) * c1
        tiles = [add_bias(s[:, kj * tq:(kj + 1) * tq], qi, kj) for kj in range(qi + 1)]
        z = jnp.concatenate(tiles, axis=1) if qi else tiles[0]
        return z, jnp.max(z, axis=1, keepdims=True)

    def weigh(qi, z, m):
        q0, kv = qi * tq, (qi + 1) * tq
        p = jnp.exp2(z - m)
        l = jnp.sum(p, axis=1, keepdims=True)
        o = jnp.dot(p.astype(BF16), v_ref[0, :kv, :], preferred_element_type=F32) / l
        o_ref[0, q0:kv, :] = o.astype(o_ref.dtype)

    pending = logits(0)
    for qi in range(1, n_q):
        upcoming = logits(qi)
        weigh(qi - 1, *pending)
        pending = upcoming
    weigh(n_q - 1, *pending)


def _cast_specs(weights, n_steps, step_index):
    in_specs, out_specs, out_shapes = [], [], []
    for w in weights:
        rows, cols = w.shape[0] // n_steps, w.shape[1]
        assert rows * n_steps == w.shape[0] and rows % BF16_ROWS == 0
        spec = pl.BlockSpec((rows, cols), lambda *idx: (step_index(*idx), 0))
        in_specs.append(spec)
        out_specs.append(spec)
        out_shapes.append(jax.ShapeDtypeStruct(w.shape, BF16))
    return in_specs, out_specs, out_shapes


def _cast_rows(in_refs, out_refs):
    for i_ref, o_ref in zip(in_refs, out_refs):
        o_ref[...] = i_ref[...].astype(BF16)


def _fox_kernel(q_ref, k_ref, v_ref, ccol_ref, crow_ref, *rest, tq):
    n_cast = len(rest) // 2
    o_ref = rest[n_cast]
    _cast_rows(rest[:n_cast], rest[n_cast + 1:])
    h = pl.program_id(1)
    s_len = q_ref.shape[1]
    lane = lax.broadcasted_iota(jnp.int32, (s_len, LANES), 1)
    cq = jnp.sum(jnp.where(lane == h, ccol_ref[0], 0.0), axis=1, keepdims=True) * LOG2E
    ck = crow_ref[0, 0] * LOG2E
    r = lax.broadcasted_iota(jnp.int32, (tq, tq), 0)
    c = lax.broadcasted_iota(jnp.int32, (tq, tq), 1)
    causal = c <= r

    def add_bias(tile, qi, kj):
        z = tile + cq[qi * tq:(qi + 1) * tq] - ck[:, kj * tq:(kj + 1) * tq]
        return jnp.where(causal, z, NEG_INF) if kj == qi else z

    _causal_sweep(q_ref, k_ref, v_ref, o_ref, tq, add_bias)


def _fox(qkv, ccol, crow4, cast_weights, *, q_off, k_off, v_off, n_heads, tq):
    b, s, _ = qkv.shape
    spec = lambda off: pl.BlockSpec((1, s, HEAD_DIM), lambda bi, h: (bi, 0, off + h))
    c_in, c_out, c_shapes = _cast_specs(cast_weights, b * n_heads, lambda bi, h: bi * n_heads + h)
    return pl.pallas_call(
        functools.partial(_fox_kernel, tq=tq),
        grid=(b, n_heads),
        in_specs=[
            spec(q_off), spec(k_off), spec(v_off),
            pl.BlockSpec((1, s, LANES), lambda bi, h: (bi, 0, 0)),
            pl.BlockSpec((1, 1, 1, s), lambda bi, h: (bi, h, 0, 0)),
        ] + c_in,
        out_specs=[spec(0)] + c_out,
        out_shape=[jax.ShapeDtypeStruct((b, s, n_heads * HEAD_DIM), BF16)] + c_shapes,
        compiler_params=_params(("arbitrary", "arbitrary")),
        name="fox_attention",
    )(qkv, qkv, qkv, ccol, crow4, *cast_weights)


def _dilated_kernel(slopes_ref, q_ref, k_ref, v_ref, *rest, tq):
    n_cast = (len(rest) - 2) // 2
    o_ref, bias_scr = rest[n_cast], rest[-1]
    _cast_rows(rest[:n_cast], rest[n_cast + 1:-1])
    h = pl.program_id(0)

    @pl.when(pl.program_id(1) == 0)
    def _():
        slope = slopes_ref[h]
        r = lax.broadcasted_iota(jnp.int32, (tq, tq), 0)
        c = lax.broadcasted_iota(jnp.int32, (tq, tq), 1)
        for d in range(bias_scr.shape[0]):
            delta = r - c + d * tq
            count = jnp.zeros((tq, tq), F32)
            for window, dil in DIL_PATTERNS:
                assert dil & (dil - 1) == 0
                member = jnp.logical_and((delta & (dil - 1)) == 0, delta <= window)
                count = count + jnp.where(member, 1.0, 0.0)
            valid = jnp.logical_and(delta >= 0, count > 0.0)
            bias = jnp.log(jnp.maximum(count, 1.0)) - slope * delta.astype(F32)
            bias_scr[d] = jnp.where(valid, bias, NEG_INF) * LOG2E

    _causal_sweep(q_ref, k_ref, v_ref, o_ref, tq, lambda tile, qi, kj: tile + bias_scr[qi - kj])


def _dilated(qkv, slopes, cast_weights, *, q_off, k_off, v_off, n_heads, tq):
    b, s, _ = qkv.shape
    spec = lambda off: pl.BlockSpec((1, s, HEAD_DIM), lambda h, bi, sl: (bi, 0, off + h))
    c_in, c_out, c_shapes = _cast_specs(cast_weights, b * n_heads, lambda h, bi, sl: h * b + bi)
    return pl.pallas_call(
        functools.partial(_dilated_kernel, tq=tq),
        grid_spec=pltpu.PrefetchScalarGridSpec(
            num_scalar_prefetch=1,
            grid=(n_heads, b),
            in_specs=[spec(q_off), spec(k_off), spec(v_off)] + c_in,
            out_specs=[spec(0)] + c_out,
            scratch_shapes=[pltpu.VMEM((s // tq, tq, tq), F32)],
        ),
        out_shape=[jax.ShapeDtypeStruct((b, s, n_heads * HEAD_DIM), BF16)] + c_shapes,
        compiler_params=_params(("arbitrary", "arbitrary")),
        name="dilated_attention",
    )(slopes, qkv, qkv, qkv, *cast_weights)


def _mix_kernel(oa_ref, ob_ref, ga_ref, gb_ref, x_ref, wa_ref, wb_ref, wo_ref, g_ref,
                x1_ref, h2_ref):
    ta = jnp.dot(oa_ref[...], wa_ref[...], preferred_element_type=F32)
    tb = jnp.dot(ob_ref[...], wb_ref[...], preferred_element_type=F32)
    merged = (jax.nn.sigmoid(ga_ref[...]) * ta + jax.nn.sigmoid(gb_ref[...]) * tb).astype(BF16)
    x1 = x_ref[...] + jnp.dot(merged, wo_ref[...], preferred_element_type=F32)
    x1_ref[...] = x1
    ms = jnp.mean(x1 * x1, axis=-1, keepdims=True)
    h2_ref[...] = (x1 * lax.rsqrt(ms + EPS) * g_ref[...]).astype(BF16)


def _mix(oa, ob, gates, x2, w_a, w_b, w_o, g_ffn):
    m, d = x2.shape
    wa_rows, wb_rows = w_a.shape[0], w_b.shape[0]
    tm = 256
    resident = lambda shape: pl.BlockSpec(shape, lambda i: (0, 0), pipeline_mode=pl.Buffered(1))
    return pl.pallas_call(
        _mix_kernel,
        grid=(m // tm,),
        in_specs=[
            pl.BlockSpec((tm, wa_rows), lambda i: (i, 0)),
            pl.BlockSpec((tm, wb_rows), lambda i: (i, 0)),
            pl.BlockSpec((tm, d), lambda i: (i, 0)),
            pl.BlockSpec((tm, d), lambda i: (i, 1)),
            pl.BlockSpec((tm, d), lambda i: (i, 0)),
            resident((wa_rows, d)),
            resident((wb_rows, d)),
            resident((d, d)),
            pl.BlockSpec((1, d), lambda i: (0, 0)),
        ],
        out_specs=[
            pl.BlockSpec((tm, d), lambda i: (i, 0)),
            pl.BlockSpec((tm, d), lambda i: (i, 0)),
        ],
        out_shape=[
            jax.ShapeDtypeStruct((m, d), F32),
            jax.ShapeDtypeStruct((m, d), BF16),
        ],
        compiler_params=_params(("arbitrary",)),
        name="branch_mix_out_proj",
    )(oa, ob, gates, gates, x2, w_a, w_b, w_o, g_ffn)


def _ffn_kernel(h_ref, x1_ref, wg_ref, wv_ref, cg_ref, cv_ref, bg_ref, bv_ref, wd_ref,
                out_ref, ug_scr, uv_scr, carry_g, carry_v, *, tm, tiles_per_seq):
    i = pl.program_id(0)
    j = pl.program_id(1)
    first = (i % tiles_per_seq) == 0

    @pl.when(jnp.logical_and(i == 0, j == 0))
    def _():
        carry_g[...] = jnp.zeros(carry_g.shape, F32)
        carry_v[...] = jnp.zeros(carry_v.shape, F32)

    @pl.when(j == 0)
    def _():
        out_ref[...] = x1_ref[...]

    h = h_ref[...]

    def up_conv(w_ref, c_ref, b_ref, u_scr, carry):
        prev = carry[j]
        u_scr[0:SUBLANES, :] = jnp.where(first, jnp.zeros_like(prev), prev)
        u_scr[SUBLANES:, :] = jnp.dot(h, w_ref[...], preferred_element_type=F32)
        carry[j] = u_scr[tm:tm + SUBLANES, :]
        out = b_ref[...]
        for t in range(CONV_WIDTH):
            shift = CONV_WIDTH - 1 - t
            out = out + c_ref[t:t + 1, :] * u_scr[pl.ds(SUBLANES - shift, tm), :]
        return out

    gate = up_conv(wg_ref, cg_ref, bg_ref, ug_scr, carry_g)
    val = up_conv(wv_ref, cv_ref, bv_ref, uv_scr, carry_v)
    a = (gate * jax.nn.sigmoid(gate) * val).astype(BF16)
    out_ref[...] += jnp.dot(a, wd_ref[...], preferred_element_type=F32)


def _ffn(h2, x1, w_up, w_conv, b_conv, w_down, *, seq_len):
    m, d = h2.shape
    d_ff = w_down.shape[0]
    tm, tf = 1024, 512
    nj = d_ff // tf
    kern = functools.partial(_ffn_kernel, tm=tm, tiles_per_seq=seq_len // tm)
    return pl.pallas_call(
        kern,
        grid=(m // tm, nj),
        in_specs=[
            pl.BlockSpec((tm, d), lambda i, j: (i, 0)),
            pl.BlockSpec((tm, d), lambda i, j: (i, 0)),
            pl.BlockSpec((d, tf), lambda i, j: (0, j)),
            pl.BlockSpec((d, tf), lambda i, j: (0, nj + j)),
            pl.BlockSpec((CONV_WIDTH, tf), lambda i, j: (0, j)),
            pl.BlockSpec((CONV_WIDTH, tf), lambda i, j: (0, nj + j)),
            pl.BlockSpec((1, tf), lambda i, j: (0, j)),
            pl.BlockSpec((1, tf), lambda i, j: (0, nj + j)),
            pl.BlockSpec((tf, d), lambda i, j: (j, 0)),
        ],
        out_specs=pl.BlockSpec((tm, d), lambda i, j: (i, 0)),
        out_shape=jax.ShapeDtypeStruct((m, d), F32),
        scratch_shapes=[
            pltpu.VMEM((tm + SUBLANES, tf), F32),
            pltpu.VMEM((tm + SUBLANES, tf), F32),
            pltpu.VMEM((nj, SUBLANES, tf), F32),
            pltpu.VMEM((nj, SUBLANES, tf), F32),
        ],
        compiler_params=_params(("arbitrary", "arbitrary"), 62 * 1024 * 1024),
        name="conv_ffn",
    )(h2, x1, w_up, w_up, w_conv, w_conv, b_conv, b_conv, w_down)


def _layer(x, g_attn, w_in, b_forget, g_q_fox, g_k_fox, g_q_dil, g_k_dil,
           w_br_fox, w_br_dil, w_out, g_ffn, w_up, w_conv, b_conv, w_down):
    b, s, d = x.shape
    w_fox = N_HEADS_FOX * HEAD_DIM
    w_dil = N_HEADS_DIL * HEAD_DIM
    n_qkv = 3 * w_fox + 3 * w_dil
    f_lo, f_hi = 3 * w_fox, 3 * w_fox + N_HEADS_FOX

    ones = jnp.ones((w_fox,), F32)
    g_qk = jnp.concatenate([g_q_fox.reshape(-1), g_k_fox.reshape(-1), ones,
                            g_q_dil.reshape(-1), g_k_dil.reshape(-1), ones]).reshape(1, n_qkv)
    b_pad = jnp.pad(b_forget, (0, LANES - N_HEADS_FOX)).reshape(1, LANES)
    slopes = jnp.asarray(2.0 ** (-8.0 * np.arange(1, N_HEADS_DIL + 1) / N_HEADS_DIL), dtype=F32)

    x2 = x.reshape(b * s, d)
    qkv, gates, f_pre = _in_proj(x2, g_attn.reshape(1, d), w_in.T, g_qk,
                                 f_lo=f_lo, f_hi=f_hi, n_qkv=n_qkv, n_gate=2 * d)
    qkv = qkv.reshape(b, s, n_qkv)

    ccol, crow = _forget(f_pre.reshape(b, s, LANES), b_pad, n_heads=N_HEADS_FOX)
    crow4 = crow.reshape(b, N_HEADS_FOX, 1, s)
    o_a, w_up_b = _fox(qkv, ccol, crow4, [w_up], q_off=0, k_off=N_HEADS_FOX,
                       v_off=2 * N_HEADS_FOX, n_heads=N_HEADS_FOX, tq=256)
    base = 3 * N_HEADS_FOX
    o_b, w_down_b, w_out_b, w_br_fox_b, w_br_dil_b = _dilated(
        qkv, slopes, [w_down, w_out, w_br_fox, w_br_dil], q_off=base, k_off=base + N_HEADS_DIL,
        v_off=base + 2 * N_HEADS_DIL, n_heads=N_HEADS_DIL, tq=256)

    x1, h2 = _mix(o_a.reshape(b * s, w_fox), o_b.reshape(b * s, w_dil), gates, x2,
                  w_br_fox_b, w_br_dil_b, w_out_b, g_ffn.reshape(1, d))
    out = _ffn(h2, x1, w_up_b, w_conv, b_conv.reshape(1, -1), w_down_b, seq_len=s)
    return out.reshape(b, s, d)


def kernel(x, g_attn, w_in, b_forget, g_q_fox, g_k_fox, g_q_dil, g_k_dil, w_br_fox, w_br_dil,
           w_out, g_ffn, w_up, w_conv, b_conv, w_down):
    for l in range(w_in.shape[0]):
        x = _layer(x, g_attn[l], w_in[l], b_forget[l], g_q_fox[l], g_k_fox[l], g_q_dil[l],
                   g_k_dil[l], w_br_fox[l], w_br_dil[l], w_out[l], g_ffn[l], w_up[l], w_conv[l],
                   b_conv[l], w_down[l])
    return x
```

```python
import functools

import numpy as np
import jax
import jax.numpy as jnp
from jax import lax
from jax.experimental import pallas as pl
from jax.experimental.pallas import tpu as pltpu

HEAD_DIM = 128
N_HEADS_FOX = 8
N_HEADS_DIL = 8
DIL_PATTERNS = ((128, 1), (512, 4), (2048, 16))
CONV_WIDTH = 3
EPS = 1e-6
NEG_INF = -1e30
LOG2E = float(np.log2(np.e))
SWEEP_AHEAD = 1
QK_SCALE2 =LOG2E / float(np.sqrt(HEAD_DIM))

F32 = jnp.float32
BF16 = jnp.bfloat16

LANES = 128
SUBLANES = 8
BF16_ROWS = 2 * SUBLANES
VMEM_LIMIT = 56 * 1024 * 1024

NT_DIMS = (((1,), (1,)), ((), ()))


def _params(sem, vmem_limit=VMEM_LIMIT):
    return pltpu.CompilerParams(dimension_semantics=sem, vmem_limit_bytes=vmem_limit)


def _in_proj_kernel(x_ref, g_ref, w_ref, wf_ref, gqk_ref, qkv_ref, gate_ref, f_ref, h_scr,
                    *, n_qkv_tiles, tiles_per_group, heads_per_tile):
    j = pl.program_id(1)

    def project(h, w_t):
        return lax.dot_general(h, w_t.astype(BF16), NT_DIMS, preferred_element_type=F32)

    @pl.when(j == 0)
    def _():
        x = x_ref[...]
        ms = jnp.mean(x * x, axis=-1, keepdims=True)
        h = (x * lax.rsqrt(ms + EPS) * g_ref[...]).astype(BF16)
        h_scr[...] = h
        wf = wf_ref[...]
        wf = jnp.concatenate([wf, jnp.zeros((LANES - wf.shape[0], wf.shape[1]), F32)], axis=0)
        f_ref[...] = project(h, wf)

    def store_qk(acc):
        for hh in range(heads_per_tile):
            sl = slice(hh * HEAD_DIM, (hh + 1) * HEAD_DIM)
            a = acc[:, sl]
            ms = jnp.mean(a * a, axis=-1, keepdims=True)
            qkv_ref[:, sl] = (a * lax.rsqrt(ms + EPS) * gqk_ref[:, sl]).astype(BF16)

    def store_v(acc):
        qkv_ref[...] = acc.astype(BF16)

    def store_gate(acc):
        gate_ref[...] = acc

    is_qkv = j < n_qkv_tiles
    is_qk = jnp.logical_and(is_qkv, (j % (3 * tiles_per_group)) < 2 * tiles_per_group)
    cases = (
        (is_qk, store_qk),
        (jnp.logical_and(is_qkv, jnp.logical_not(is_qk)), store_v),
        (jnp.logical_not(is_qkv), store_gate),
    )
    for cond, store in cases:
        pl.when(cond)(lambda store=store: store(project(h_scr[...], w_ref[...])))


def _in_proj(x2, g_attn, w_t, g_qk, *, f_lo, f_hi, n_qkv, n_gate):
    m, d = x2.shape
    tm, tn = 1024, 1024
    assert f_lo % tn == 0 and f_hi % SUBLANES == 0
    n_qkv_tiles = n_qkv // tn
    n_tiles = (n_qkv + n_gate) // tn
    kern = functools.partial(
        _in_proj_kernel, n_qkv_tiles=n_qkv_tiles,
        tiles_per_group=(N_HEADS_FOX * HEAD_DIM) // tn, heads_per_tile=tn // HEAD_DIM)
    w_row = lambda j: pl.multiple_of(j * tn + jnp.where(j * tn >= f_lo, f_hi - f_lo, 0), SUBLANES)
    return pl.pallas_call(
        kern,
        grid=(m // tm, n_tiles),
        in_specs=[
            pl.BlockSpec((tm, d), lambda i, j: (i, 0)),
            pl.BlockSpec((1, d), lambda i, j: (0, 0)),
            pl.BlockSpec((pl.Element(tn), pl.Element(d)), lambda i, j: (w_row(j), 0)),
            pl.BlockSpec((pl.Element(f_hi - f_lo), pl.Element(d)), lambda i, j: (f_lo, 0)),
            pl.BlockSpec((1, tn), lambda i, j: (0, jnp.minimum(j, n_qkv_tiles - 1))),
        ],
        out_specs=[
            pl.BlockSpec((tm, tn), lambda i, j: (i, jnp.minimum(j, n_qkv_tiles - 1))),
            pl.BlockSpec((tm, tn), lambda i, j: (i, jnp.maximum(j - n_qkv_tiles, 0))),
            pl.BlockSpec((tm, LANES), lambda i, j: (i, 0)),
        ],
        out_shape=[
            jax.ShapeDtypeStruct((m, n_qkv), BF16),
            jax.ShapeDtypeStruct((m, n_gate), F32),
            jax.ShapeDtypeStruct((m, LANES), F32),
        ],
        scratch_shapes=[pltpu.VMEM((tm, d), BF16)],
        compiler_params=_params(("arbitrary", "arbitrary")),
        name="in_proj",
    )(x2, g_attn, w_t, w_t, g_qk)


def _forget_kernel(f_ref, b_ref, ccol_ref, crow_ref, *, n_heads):
    z = f_ref[0] + b_ref[...]
    c = jax.nn.log_sigmoid(z)
    s = c.shape[0]
    row = lax.broadcasted_iota(jnp.int32, c.shape, 0)
    k = 1
    while k < s:
        c = c + jnp.where(row >= k, pltpu.roll(c, k, axis=0), 0.0)
        k *= 2
    ccol_ref[0] = c
    crow_ref[0] = c.T[:n_heads, :]


def _forget(f_pre, b_pad, *, n_heads):
    b, s, _ = f_pre.shape
    return pl.pallas_call(
        functools.partial(_forget_kernel, n_heads=n_heads),
        grid=(b,),
        in_specs=[
            pl.BlockSpec((1, s, LANES), lambda i: (i, 0, 0)),
            pl.BlockSpec((1, LANES), lambda i: (0, 0)),
        ],
        out_specs=[
            pl.BlockSpec((1, s, LANES), lambda i: (i, 0, 0)),
            pl.BlockSpec((1, n_heads, s), lambda i: (i, 0, 0)),
        ],
        out_shape=[
            jax.ShapeDtypeStruct((b, s, LANES), F32),
            jax.ShapeDtypeStruct((b, n_heads, s), F32),
        ],
        compiler_params=_params(("arbitrary",)),
        name="forget_cumsum",
    )(f_pre, b_pad)


def _causal_sweep(q_ref, k_ref, v_aug, o_ref, tq, add_bias):
    s_len = q_ref.shape[1]
    n_q = s_len // tq

    def logits(qi):
        q0, kv = qi * tq, (qi + 1) * tq
        s = lax.dot_general(q_ref[0, q0:kv, :], k_ref[0, :kv, :], NT_DIMS,
                            preferred_element_type=F32)
        tiles = [add_bias(s[:, kj * tq:(kj + 1) * tq], qi, kj) for kj in range(qi + 1)]
        z = jnp.concatenate(tiles, axis=1) if qi else tiles[0]
        return z, jnp.max(z, axis=1, keepdims=True)

    def weigh(qi, z, m):
        q0, kv = qi * tq, (qi + 1) * tq
        p = jnp.exp2(z - m).astype(BF16)
        o = jnp.dot(p, v_aug[0:kv, :], preferred_element_type=F32)
        o_ref[0, q0:kv, :] = (o[:, :HEAD_DIM] / o[:, HEAD_DIM:HEAD_DIM + 1]).astype(o_ref.dtype)

    pending = [logits(qi) for qi in range(min(SWEEP_AHEAD, n_q))]
    for qi in range(n_q):
        if qi + SWEEP_AHEAD < n_q:
            pending.append(logits(qi + SWEEP_AHEAD))
        weigh(qi, *pending.pop(0))


def _fill_v_aug(v_ref, v_aug):
    v_aug[:, :HEAD_DIM] = v_ref[0]
    lane = lax.broadcasted_iota(jnp.int32, (v_aug.shape[0], v_aug.shape[1] - HEAD_DIM), 1)
    v_aug[:, HEAD_DIM:] = jnp.where(lane == 0, 1.0, 0.0).astype(BF16)


def _cast_specs(weights, n_steps, step_index):
    in_specs, out_specs, out_shapes = [], [], []
    for w in weights:
        rows, cols = w.shape[0] // n_steps, w.shape[1]
        assert rows * n_steps == w.shape[0] and rows % BF16_ROWS == 0
        spec = pl.BlockSpec((rows, cols), lambda *idx: (step_index(*idx), 0))
        in_specs.append(spec)
        out_specs.append(spec)
        out_shapes.append(jax.ShapeDtypeStruct(w.shape, BF16))
    return in_specs, out_specs, out_shapes


def _cast_rows(in_refs, out_refs):
    for i_ref, o_ref in zip(in_refs, out_refs):
        o_ref[...] = i_ref[...].astype(BF16)


def _fox_kernel(q_ref, k_ref, v_ref, ccol_ref, crow_ref, *rest, tq):
    n_cast = (len(rest) - 2) // 2
    o_ref, v_aug = rest[n_cast], rest[-1]
    _cast_rows(rest[:n_cast], rest[n_cast + 1:-1])
    _fill_v_aug(v_ref, v_aug)
    h = pl.program_id(1)
    s_len = q_ref.shape[1]
    lane = lax.broadcasted_iota(jnp.int32, (s_len, LANES), 1)
    cq = jnp.sum(jnp.where(lane == h, ccol_ref[0], 0.0), axis=1, keepdims=True) * LOG2E
    ck = crow_ref[0, 0] * LOG2E
    r = lax.broadcasted_iota(jnp.int32, (tq, tq), 0)
    c = lax.broadcasted_iota(jnp.int32, (tq, tq), 1)
    causal = c <= r

    def add_bias(tile, qi, kj):
        z = tile + cq[qi * tq:(qi + 1) * tq] - ck[:, kj * tq:(kj + 1) * tq]
        return jnp.where(causal, z, NEG_INF) if kj == qi else z

    _causal_sweep(q_ref, k_ref, v_aug, o_ref, tq, add_bias)


def _fox(qkv, ccol, crow4, cast_weights, *, q_off, k_off, v_off, n_heads, tq):
    b, s, _ = qkv.shape
    spec = lambda off: pl.BlockSpec((1, s, HEAD_DIM), lambda bi, h: (bi, 0, off + h))
    c_in, c_out, c_shapes = _cast_specs(cast_weights, b * n_heads, lambda bi, h: bi * n_heads + h)
    return pl.pallas_call(
        functools.partial(_fox_kernel, tq=tq),
        grid=(b, n_heads),
        in_specs=[
            spec(q_off), spec(k_off), spec(v_off),
            pl.BlockSpec((1, s, LANES), lambda bi, h: (bi, 0, 0)),
            pl.BlockSpec((1, 1, 1, s), lambda bi, h: (bi, h, 0, 0)),
        ] + c_in,
        out_specs=[spec(0)] + c_out,
        out_shape=[jax.ShapeDtypeStruct((b, s, n_heads * HEAD_DIM), BF16)] + c_shapes,
        scratch_shapes=[pltpu.VMEM((s, 2 * HEAD_DIM), BF16)],
        compiler_params=_params(("arbitrary", "arbitrary")),
        name="fox_attention",
    )(qkv, qkv, qkv, ccol, crow4, *cast_weights)


def _dilated_kernel(slopes_ref, q_ref, k_ref, v_ref, *rest, tq):
    n_cast = (len(rest) - 3) // 2
    o_ref, bias_scr, v_aug = rest[n_cast], rest[-2], rest[-1]
    _cast_rows(rest[:n_cast], rest[n_cast + 1:-2])
    _fill_v_aug(v_ref, v_aug)
    h = pl.program_id(0)

    @pl.when(pl.program_id(1) == 0)
    def _():
        slope = slopes_ref[h]
        r = lax.broadcasted_iota(jnp.int32, (tq, tq), 0)
        c = lax.broadcasted_iota(jnp.int32, (tq, tq), 1)
        for d in range(bias_scr.shape[0]):
            delta = r - c + d * tq
            count = jnp.zeros((tq, tq), F32)
            for window, dil in DIL_PATTERNS:
                assert dil & (dil - 1) == 0
                member = jnp.logical_and((delta & (dil - 1)) == 0, delta <= window)
                count = count + jnp.where(member, 1.0, 0.0)
            valid = jnp.logical_and(delta >= 0, count > 0.0)
            bias = jnp.log(jnp.maximum(count, 1.0)) - slope * delta.astype(F32)
            bias_scr[d] = jnp.where(valid, bias, NEG_INF) * LOG2E

    _causal_sweep(q_ref, k_ref, v_aug, o_ref, tq, lambda tile, qi, kj: tile + bias_scr[qi - kj])


def _dilated(qkv, slopes, cast_weights, *, q_off, k_off, v_off, n_heads, tq):
    b, s, _ = qkv.shape
    spec = lambda off: pl.BlockSpec((1, s, HEAD_DIM), lambda h, bi, sl: (bi, 0, off + h))
    c_in, c_out, c_shapes = _cast_specs(cast_weights, b * n_heads, lambda h, bi, sl: h * b + bi)
    return pl.pallas_call(
        functools.partial(_dilated_kernel, tq=tq),
        grid_spec=pltpu.PrefetchScalarGridSpec(
            num_scalar_prefetch=1,
            grid=(n_heads, b),
            in_specs=[spec(q_off), spec(k_off), spec(v_off)] + c_in,
            out_specs=[spec(0)] + c_out,
            scratch_shapes=[pltpu.VMEM((s // tq, tq, tq), F32),
                            pltpu.VMEM((s, 2 * HEAD_DIM), BF16)],
        ),
        out_shape=[jax.ShapeDtypeStruct((b, s, n_heads * HEAD_DIM), BF16)] + c_shapes,
        compiler_params=_params(("arbitrary", "arbitrary")),
        name="dilated_attention",
    )(slopes, qkv, qkv, qkv, *cast_weights)


def _mix_kernel(oa_ref, ob_ref, ga_ref, gb_ref, x_ref, wa_ref, wb_ref, wo_ref, g_ref,
                x1_ref, h2_ref):
    ta = jnp.dot(oa_ref[...], wa_ref[...], preferred_element_type=F32)
    tb = jnp.dot(ob_ref[...], wb_ref[...], preferred_element_type=F32)
    merged = (jax.nn.sigmoid(ga_ref[...]) * ta + jax.nn.sigmoid(gb_ref[...]) * tb).astype(BF16)
    x1 = x_ref[...] + jnp.dot(merged, wo_ref[...], preferred_element_type=F32)
    x1_ref[...] = x1
    ms = jnp.mean(x1 * x1, axis=-1, keepdims=True)
    h2_ref[...] = (x1 * lax.rsqrt(ms + EPS) * g_ref[...]).astype(BF16)


def _mix(oa, ob, gates, x2, w_a, w_b, w_o, g_ffn):
    m, d = x2.shape
    wa_rows, wb_rows = w_a.shape[0], w_b.shape[0]
    tm = 256
    resident = lambda shape: pl.BlockSpec(shape, lambda i: (0, 0), pipeline_mode=pl.Buffered(1))
    return pl.pallas_call(
        _mix_kernel,
        grid=(m // tm,),
        in_specs=[
            pl.BlockSpec((tm, wa_rows), lambda i: (i, 0)),
            pl.BlockSpec((tm, wb_rows), lambda i: (i, 0)),
            pl.BlockSpec((tm, d), lambda i: (i, 0)),
            pl.BlockSpec((tm, d), lambda i: (i, 1)),
            pl.BlockSpec((tm, d), lambda i: (i, 0)),
            resident((wa_rows, d)),
            resident((wb_rows, d)),
            resident((d, d)),
            pl.BlockSpec((1, d), lambda i: (0, 0)),
        ],
        out_specs=[
            pl.BlockSpec((tm, d), lambda i: (i, 0)),
            pl.BlockSpec((tm, d), lambda i: (i, 0)),
        ],
        out_shape=[
            jax.ShapeDtypeStruct((m, d), F32),
            jax.ShapeDtypeStruct((m, d), BF16),
        ],
        compiler_params=_params(("arbitrary",)),
        name="branch_mix_out_proj",
    )(oa, ob, gates, gates, x2, w_a, w_b, w_o, g_ffn)


def _ffn_kernel(h_ref, x1_ref, wg_ref, wv_ref, cg_ref, cv_ref, bg_ref, bv_ref, wd_ref,
                out_ref, ug_scr, uv_scr, carry_g, carry_v, *, tm, tiles_per_seq):
    i = pl.program_id(0)
    j = pl.program_id(1)
    first = (i % tiles_per_seq) == 0

    @pl.when(jnp.logical_and(i == 0, j == 0))
    def _():
        carry_g[...] = jnp.zeros(carry_g.shape, F32)
        carry_v[...] = jnp.zeros(carry_v.shape, F32)

    @pl.when(j == 0)
    def _():
        out_ref[...] = x1_ref[...]

    h = h_ref[...]

    def up_conv(w_ref, c_ref, b_ref, u_scr, carry):
        prev = carry[j]
        u_scr[0:SUBLANES, :] = jnp.where(first, jnp.zeros_like(prev), prev)
        u_scr[SUBLANES:, :] = jnp.dot(h, w_ref[...], preferred_element_type=F32)
        carry[j] = u_scr[tm:tm + SUBLANES, :]
        out = b_ref[...]
        for t in range(CONV_WIDTH):
            shift = CONV_WIDTH - 1 - t
            out = out + c_ref[t:t + 1, :] * u_scr[pl.ds(SUBLANES - shift, tm), :]
        return out

    gate = up_conv(wg_ref, cg_ref, bg_ref, ug_scr, carry_g)
    val = up_conv(wv_ref, cv_ref, bv_ref, uv_scr, carry_v)
    a = (gate * jax.nn.sigmoid(gate) * val).astype(BF16)
    out_ref[...] += jnp.dot(a, wd_ref[...], preferred_element_type=F32)


def _ffn(h2, x1, w_up, w_conv, b_conv, w_down, *, seq_len):
    m, d = h2.shape
    d_ff = w_down.shape[0]
    tm, tf = 1024, 512
    nj = d_ff // tf
    kern = functools.partial(_ffn_kernel, tm=tm, tiles_per_seq=seq_len // tm)
    return pl.pallas_call(
        kern,
        grid=(m // tm, nj),
        in_specs=[
            pl.BlockSpec((tm, d), lambda i, j: (i, 0)),
            pl.BlockSpec((tm, d), lambda i, j: (i, 0)),
            pl.BlockSpec((d, tf), lambda i, j: (0, j)),
            pl.BlockSpec((d, tf), lambda i, j: (0, nj + j)),
            pl.BlockSpec((CONV_WIDTH, tf), lambda i, j: (0, j)),
            pl.BlockSpec((CONV_WIDTH, tf), lambda i, j: (0, nj + j)),
            pl.BlockSpec((1, tf), lambda i, j: (0, j)),
            pl.BlockSpec((1, tf), lambda i, j: (0, nj + j)),
            pl.BlockSpec((tf, d), lambda i, j: (j, 0)),
        ],
        out_specs=pl.BlockSpec((tm, d), lambda i, j: (i, 0)),
        out_shape=jax.ShapeDtypeStruct((m, d), F32),
        scratch_shapes=[
            pltpu.VMEM((tm + SUBLANES, tf), F32),
            pltpu.VMEM((tm + SUBLANES, tf), F32),
            pltpu.VMEM((nj, SUBLANES, tf), F32),
            pltpu.VMEM((nj, SUBLANES, tf), F32),
        ],
        compiler_params=_params(("arbitrary", "arbitrary"), 62 * 1024 * 1024),
        name="conv_ffn",
    )(h2, x1, w_up, w_up, w_conv, w_conv, b_conv, b_conv, w_down)


def _layer(x, g_attn, w_in, b_forget, g_q_fox, g_k_fox, g_q_dil, g_k_dil,
           w_br_fox, w_br_dil, w_out, g_ffn, w_up, w_conv, b_conv, w_down):
    b, s, d = x.shape
    w_fox = N_HEADS_FOX * HEAD_DIM
    w_dil = N_HEADS_DIL * HEAD_DIM
    n_qkv = 3 * w_fox + 3 * w_dil
    f_lo, f_hi = 3 * w_fox, 3 * w_fox + N_HEADS_FOX

    ones = jnp.ones((w_fox,), F32)
    g_qk = jnp.concatenate([g_q_fox.reshape(-1) * QK_SCALE2, g_k_fox.reshape(-1), ones,
                            g_q_dil.reshape(-1) * QK_SCALE2, g_k_dil.reshape(-1), ones]
                           ).reshape(1, n_qkv)
    b_pad = jnp.pad(b_forget, (0, LANES - N_HEADS_FOX)).reshape(1, LANES)
    slopes = jnp.asarray(2.0 ** (-8.0 * np.arange(1, N_HEADS_DIL + 1) / N_HEADS_DIL), dtype=F32)

    x2 = x.reshape(b * s, d)
    qkv, gates, f_pre = _in_proj(x2, g_attn.reshape(1, d), w_in.T, g_qk,
                                 f_lo=f_lo, f_hi=f_hi, n_qkv=n_qkv, n_gate=2 * d)
    qkv = qkv.reshape(b, s, n_qkv)

    ccol, crow = _forget(f_pre.reshape(b, s, LANES), b_pad, n_heads=N_HEADS_FOX)
    crow4 = crow.reshape(b, N_HEADS_FOX, 1, s)
    o_a, w_up_b = _fox(qkv, ccol, crow4, [w_up], q_off=0, k_off=N_HEADS_FOX,
                       v_off=2 * N_HEADS_FOX, n_heads=N_HEADS_FOX, tq=256)
    base = 3 * N_HEADS_FOX
    o_b, w_down_b, w_out_b, w_br_fox_b, w_br_dil_b = _dilated(
        qkv, slopes, [w_down, w_out, w_br_fox, w_br_dil], q_off=base, k_off=base + N_HEADS_DIL,
        v_off=base + 2 * N_HEADS_DIL, n_heads=N_HEADS_DIL, tq=256)

    x1, h2 = _mix(o_a.reshape(b * s, w_fox), o_b.reshape(b * s, w_dil), gates, x2,
                  w_br_fox_b, w_br_dil_b, w_out_b, g_ffn.reshape(1, d))
    out = _ffn(h2, x1, w_up_b, w_conv, b_conv.reshape(1, -1), w_down_b, seq_len=s)
    return out.reshape(b, s, d)


def kernel(x, g_attn, w_in, b_forget, g_q_fox, g_k_fox, g_q_dil, g_k_dil, w_br_fox, w_br_dil,
           w_out, g_ffn, w_up, w_conv, b_conv, w_down):
    for l in range(w_in.shape[0]):
        x = _layer(x, g_attn[l], w_in[l], b_forget[l], g_q_fox[l], g_k_fox[l], g_q_dil[l],
                   g_k_dil[l], w_br_fox[l], w_br_dil[l], w_out[l], g_ffn[l], w_up[l], w_conv[l],
                   b_conv[l], w_down[l])
    return x
```

```python
import functools

import numpy as np
import jax
import jax.numpy as jnp
from jax import lax
from jax.experimental import pallas as pl
from jax.experimental.pallas import tpu as pltpu

HEAD_DIM = 128
N_HEADS_FOX = 8
N_HEADS_DIL = 8
DIL_PATTERNS = ((128, 1), (512, 4), (2048, 16))
CONV_WIDTH = 3
EPS = 1e-6
NEG_INF = -1e30
LOG2E = float(np.log2(np.e))
QK_SCALE2 =LOG2E / float(np.sqrt(HEAD_DIM))

F32 = jnp.float32
BF16 = jnp.bfloat16

LANES = 128
SUBLANES = 8
BF16_ROWS = 2 * SUBLANES
VMEM_LIMIT = 56 * 1024 * 1024

NT_DIMS = (((1,), (1,)), ((), ()))


def _params(sem, vmem_limit=VMEM_LIMIT):
    return pltpu.CompilerParams(dimension_semantics=sem, vmem_limit_bytes=vmem_limit)


def _in_proj_kernel(x_ref, g_ref, w_ref, wf_ref, gqk_ref, qkv_ref, gate_ref, f_ref, h_scr,
                    *, n_qkv_tiles, tiles_per_group, heads_per_tile):
    j = pl.program_id(1)

    def project(h, w_t):
        return lax.dot_general(h, w_t.astype(BF16), NT_DIMS, preferred_element_type=F32)

    @pl.when(j == 0)
    def _():
        x = x_ref[...]
        ms = jnp.mean(x * x, axis=-1, keepdims=True)
        h = (x * lax.rsqrt(ms + EPS) * g_ref[...]).astype(BF16)
        h_scr[...] = h
        wf = wf_ref[...]
        wf = jnp.concatenate([wf, jnp.zeros((LANES - wf.shape[0], wf.shape[1]), F32)], axis=0)
        f_ref[...] = project(h, wf)

    def store_qk(acc):
        for hh in range(heads_per_tile):
            sl = slice(hh * HEAD_DIM, (hh + 1) * HEAD_DIM)
            a = acc[:, sl]
            ms = jnp.mean(a * a, axis=-1, keepdims=True)
            qkv_ref[:, sl] = (a * lax.rsqrt(ms + EPS) * gqk_ref[:, sl]).astype(BF16)

    def store_v(acc):
        qkv_ref[...] = acc.astype(BF16)

    def store_gate(acc):
        gate_ref[...] = acc

    is_qkv = j < n_qkv_tiles
    is_qk = jnp.logical_and(is_qkv, (j % (3 * tiles_per_group)) < 2 * tiles_per_group)
    cases = (
        (is_qk, store_qk),
        (jnp.logical_and(is_qkv, jnp.logical_not(is_qk)), store_v),
        (jnp.logical_not(is_qkv), store_gate),
    )
    for cond, store in cases:
        pl.when(cond)(lambda store=store: store(project(h_scr[...], w_ref[...])))


def _in_proj(x2, g_attn, w_t, g_qk, *, f_lo, f_hi, n_qkv, n_gate):
    m, d = x2.shape
    tm, tn = 1024, 1024
    assert f_lo % tn == 0 and f_hi % SUBLANES == 0
    n_qkv_tiles = n_qkv // tn
    n_tiles = (n_qkv + n_gate) // tn
    kern = functools.partial(
        _in_proj_kernel, n_qkv_tiles=n_qkv_tiles,
        tiles_per_group=(N_HEADS_FOX * HEAD_DIM) // tn, heads_per_tile=tn // HEAD_DIM)
    w_row = lambda j: pl.multiple_of(j * tn + jnp.where(j * tn >= f_lo, f_hi - f_lo, 0), SUBLANES)
    return pl.pallas_call(
        kern,
        grid=(m // tm, n_tiles),
        in_specs=[
            pl.BlockSpec((tm, d), lambda i, j: (i, 0)),
            pl.BlockSpec((1, d), lambda i, j: (0, 0)),
            pl.BlockSpec((pl.Element(tn), pl.Element(d)), lambda i, j: (w_row(j), 0)),
            pl.BlockSpec((pl.Element(f_hi - f_lo), pl.Element(d)), lambda i, j: (f_lo, 0)),
            pl.BlockSpec((1, tn), lambda i, j: (0, jnp.minimum(j, n_qkv_tiles - 1))),
        ],
        out_specs=[
            pl.BlockSpec((tm, tn), lambda i, j: (i, jnp.minimum(j, n_qkv_tiles - 1))),
            pl.BlockSpec((tm, tn), lambda i, j: (i, jnp.maximum(j - n_qkv_tiles, 0))),
            pl.BlockSpec((tm, LANES), lambda i, j: (i, 0)),
        ],
        out_shape=[
            jax.ShapeDtypeStruct((m, n_qkv), BF16),
            jax.ShapeDtypeStruct((m, n_gate), F32),
            jax.ShapeDtypeStruct((m, LANES), F32),
        ],
        scratch_shapes=[pltpu.VMEM((tm, d), BF16)],
        compiler_params=_params(("arbitrary", "arbitrary")),
        name="in_proj",
    )(x2, g_attn, w_t, w_t, g_qk)


def _forget_kernel(f_ref, b_ref, ccol_ref, crow_ref, *, n_heads):
    z = f_ref[0] + b_ref[...]
    c = jax.nn.log_sigmoid(z)
    s = c.shape[0]
    row = lax.broadcasted_iota(jnp.int32, c.shape, 0)
    k = 1
    while k < s:
        c = c + jnp.where(row >= k, pltpu.roll(c, k, axis=0), 0.0)
        k *= 2
    ccol_ref[0] = c
    crow_ref[0] = c.T[:n_heads, :]


def _forget(f_pre, b_pad, *, n_heads):
    b, s, _ = f_pre.shape
    return pl.pallas_call(
        functools.partial(_forget_kernel, n_heads=n_heads),
        grid=(b,),
        in_specs=[
            pl.BlockSpec((1, s, LANES), lambda i: (i, 0, 0)),
            pl.BlockSpec((1, LANES), lambda i: (0, 0)),
        ],
        out_specs=[
            pl.BlockSpec((1, s, LANES), lambda i: (i, 0, 0)),
            pl.BlockSpec((1, n_heads, s), lambda i: (i, 0, 0)),
        ],
        out_shape=[
            jax.ShapeDtypeStruct((b, s, LANES), F32),
            jax.ShapeDtypeStruct((b, n_heads, s), F32),
        ],
        compiler_params=_params(("arbitrary",)),
        name="forget_cumsum",
    )(f_pre, b_pad)


def _causal_sweeps(streams, tq):
    s_len = streams[0][0].shape[1]
    n_q = s_len // tq

    def logits(stream, qi):
        q_ref, k_ref, _, _, add_bias = stream
        q0, kv = qi * tq, (qi + 1) * tq
        s = lax.dot_general(q_ref[0, q0:kv, :], k_ref[0, :kv, :], NT_DIMS,
                            preferred_element_type=F32)
        tiles = [add_bias(s[:, kj * tq:(kj + 1) * tq], qi, kj) for kj in range(qi + 1)]
        z = jnp.concatenate(tiles, axis=1) if qi else tiles[0]
        return z, jnp.max(z, axis=1, keepdims=True)

    def weigh(stream, qi, z, m):
        _, _, v_aug, o_ref, _ = stream
        q0, kv = qi * tq, (qi + 1) * tq
        p = jnp.exp2(z - m).astype(BF16)
        o = jnp.dot(p, v_aug[0:kv, :], preferred_element_type=F32)
        o_ref[0, q0:kv, :] = (o[:, :HEAD_DIM] / o[:, HEAD_DIM:HEAD_DIM + 1]).astype(o_ref.dtype)

    pending = [logits(stream, 0) for stream in streams]
    for qi in range(n_q):
        for si, stream in enumerate(streams):
            upcoming = logits(stream, qi + 1) if qi + 1 < n_q else None
            weigh(stream, qi, *pending[si])
            pending[si] = upcoming


def _fill_v_aug(v_ref, v_aug):
    v_aug[:, :HEAD_DIM] = v_ref[0]
    lane = lax.broadcasted_iota(jnp.int32, (v_aug.shape[0], v_aug.shape[1] - HEAD_DIM), 1)
    v_aug[:, HEAD_DIM:] = jnp.where(lane == 0, 1.0, 0.0).astype(BF16)


def _cast_specs(weights, n_steps, step_index):
    in_specs, out_specs, out_shapes = [], [], []
    for w in weights:
        rows, cols = w.shape[0] // n_steps, w.shape[1]
        assert rows * n_steps == w.shape[0] and rows % BF16_ROWS == 0
        spec = pl.BlockSpec((rows, cols), lambda *idx: (step_index(*idx), 0))
        in_specs.append(spec)
        out_specs.append(spec)
        out_shapes.append(jax.ShapeDtypeStruct(w.shape, BF16))
    return in_specs, out_specs, out_shapes


def _attention_kernel(slopes_ref, qa_ref, ka_ref, va_ref, qb_ref, kb_ref, vb_ref, ccol_ref, crow_ref,
                      *rest, tq):
    n_cast = (len(rest) - 5) // 2
    oa_ref, ob_ref = rest[n_cast], rest[n_cast + 1]
    bias_scr, va_aug, vb_aug = rest[-3:]
    _fill_v_aug(va_ref, va_aug)
    _fill_v_aug(vb_ref, vb_aug)
    h = pl.program_id(0)
    r = lax.broadcasted_iota(jnp.int32, (tq, tq), 0)
    c = lax.broadcasted_iota(jnp.int32, (tq, tq), 1)

    @pl.when(pl.program_id(1) == 0)
    def _():
        slope = slopes_ref[h]
        for d in range(bias_scr.shape[0]):
            delta = r - c + d * tq
            count = jnp.zeros((tq, tq), F32)
            for window, dil in DIL_PATTERNS:
                assert dil & (dil - 1) == 0
                member = jnp.logical_and((delta & (dil - 1)) == 0, delta <= window)
                count = count + jnp.where(member, 1.0, 0.0)
            valid = jnp.logical_and(delta >= 0, count > 0.0)
            bias = jnp.log(jnp.maximum(count, 1.0)) - slope * delta.astype(F32)
            bias_scr[d] = jnp.where(valid, bias, NEG_INF) * LOG2E

    s_len = qa_ref.shape[1]
    lane = lax.broadcasted_iota(jnp.int32, (s_len, LANES), 1)
    cq = jnp.sum(jnp.where(lane == h, ccol_ref[0], 0.0), axis=1, keepdims=True) * LOG2E
    ck = crow_ref[0, 0] * LOG2E
    causal = c <= r

    def forget_bias(tile, qi, kj):
        z = tile + cq[qi * tq:(qi + 1) * tq] - ck[:, kj * tq:(kj + 1) * tq]
        return jnp.where(causal, z, NEG_INF) if kj == qi else z

    def distance_bias(tile, qi, kj):
        return tile + bias_scr[qi - kj]

    _causal_sweeps([(qa_ref, ka_ref, va_aug, oa_ref, forget_bias),
                    (qb_ref, kb_ref, vb_aug, ob_ref, distance_bias)], tq)

    for i_ref, o_ref in zip(rest[:n_cast], rest[n_cast + 2:-3]):
        o_ref[...] = i_ref[...].astype(BF16)


def _attention(qkv, ccol, crow4, slopes, cast_weights, *, n_heads, tq):
    b, s, _ = qkv.shape
    spec = lambda group: pl.BlockSpec((1, s, HEAD_DIM),
                                      lambda h, bi, sl: (bi, 0, group * n_heads + h))
    out_spec = pl.BlockSpec((1, s, HEAD_DIM), lambda h, bi, sl: (bi, 0, h))
    c_in, c_out, c_shapes = _cast_specs(cast_weights, b * n_heads, lambda h, bi, sl: h * b + bi)
    o_shape = jax.ShapeDtypeStruct((b, s, n_heads * HEAD_DIM), BF16)
    return pl.pallas_call(
        functools.partial(_attention_kernel, tq=tq),
        grid_spec=pltpu.PrefetchScalarGridSpec(
            num_scalar_prefetch=1,
            grid=(n_heads, b),
            in_specs=[spec(g) for g in range(6)] + [
                pl.BlockSpec((1, s, LANES), lambda h, bi, sl: (bi, 0, 0)),
                pl.BlockSpec((1, 1, 1, s), lambda h, bi, sl: (bi, h, 0, 0)),
            ] + c_in,
            out_specs=[out_spec, out_spec] + c_out,
            scratch_shapes=[pltpu.VMEM((s // tq, tq, tq), F32),
                            pltpu.VMEM((s, 2 * HEAD_DIM), BF16),
                            pltpu.VMEM((s, 2 * HEAD_DIM), BF16)],
        ),
        out_shape=[o_shape, o_shape] + c_shapes,
        compiler_params=_params(("arbitrary", "arbitrary")),
        name="attention",
    )(slopes, *([qkv] * 6), ccol, crow4, *cast_weights)


def _mix_kernel(oa_ref, ob_ref, ga_ref, gb_ref, x_ref, wa_ref, wb_ref, wo_ref, g_ref,
                x1_ref, h2_ref):
    ta = jnp.dot(oa_ref[...], wa_ref[...], preferred_element_type=F32)
    tb = jnp.dot(ob_ref[...], wb_ref[...], preferred_element_type=F32)
    merged = (jax.nn.sigmoid(ga_ref[...]) * ta + jax.nn.sigmoid(gb_ref[...]) * tb).astype(BF16)
    x1 = x_ref[...] + jnp.dot(merged, wo_ref[...], preferred_element_type=F32)
    x1_ref[...] = x1
    ms = jnp.mean(x1 * x1, axis=-1, keepdims=True)
    h2_ref[...] = (x1 * lax.rsqrt(ms + EPS) * g_ref[...]).astype(BF16)


def _mix(oa, ob, gates, x2, w_a, w_b, w_o, g_ffn):
    m, d = x2.shape
    wa_rows, wb_rows = w_a.shape[0], w_b.shape[0]
    tm = 256
    resident = lambda shape: pl.BlockSpec(shape, lambda i: (0, 0), pipeline_mode=pl.Buffered(1))
    return pl.pallas_call(
        _mix_kernel,
        grid=(m // tm,),
        in_specs=[
            pl.BlockSpec((tm, wa_rows), lambda i: (i, 0)),
            pl.BlockSpec((tm, wb_rows), lambda i: (i, 0)),
            pl.BlockSpec((tm, d), lambda i: (i, 0)),
            pl.BlockSpec((tm, d), lambda i: (i, 1)),
            pl.BlockSpec((tm, d), lambda i: (i, 0)),
            resident((wa_rows, d)),
            resident((wb_rows, d)),
            resident((d, d)),
            pl.BlockSpec((1, d), lambda i: (0, 0)),
        ],
        out_specs=[
            pl.BlockSpec((tm, d), lambda i: (i, 0)),
            pl.BlockSpec((tm, d), lambda i: (i, 0)),
        ],
        out_shape=[
            jax.ShapeDtypeStruct((m, d), F32),
            jax.ShapeDtypeStruct((m, d), BF16),
        ],
        compiler_params=_params(("arbitrary",)),
        name="branch_mix_out_proj",
    )(oa, ob, gates, gates, x2, w_a, w_b, w_o, g_ffn)


def _ffn_kernel(h_ref, x1_ref, wg_ref, wv_ref, cg_ref, cv_ref, bg_ref, bv_ref, wd_ref,
                out_ref, ug_scr, uv_scr, carry_g, carry_v, *, tm, tiles_per_seq):
    i = pl.program_id(0)
    j = pl.program_id(1)
    first = (i % tiles_per_seq) == 0

    @pl.when(jnp.logical_and(i == 0, j == 0))
    def _():
        carry_g[...] = jnp.zeros(carry_g.shape, F32)
        carry_v[...] = jnp.zeros(carry_v.shape, F32)

    @pl.when(j == 0)
    def _():
        out_ref[...] = x1_ref[...]

    h = h_ref[...]

    def up_conv(w_ref, c_ref, b_ref, u_scr, carry):
        prev = carry[j]
        u_scr[0:SUBLANES, :] = jnp.where(first, jnp.zeros_like(prev), prev)
        u_scr[SUBLANES:, :] = jnp.dot(h, w_ref[...], preferred_element_type=F32)
        carry[j] = u_scr[tm:tm + SUBLANES, :]
        out = b_ref[...]
        for t in range(CONV_WIDTH):
            shift = CONV_WIDTH - 1 - t
            out = out + c_ref[t:t + 1, :] * u_scr[pl.ds(SUBLANES - shift, tm), :]
        return out

    gate = up_conv(wg_ref, cg_ref, bg_ref, ug_scr, carry_g)
    val = up_conv(wv_ref, cv_ref, bv_ref, uv_scr, carry_v)
    a = (gate * jax.nn.sigmoid(gate) * val).astype(BF16)
    out_ref[...] += jnp.dot(a, wd_ref[...], preferred_element_type=F32)


def _ffn(h2, x1, w_up, w_conv, b_conv, w_down, *, seq_len):
    m, d = h2.shape
    d_ff = w_down.shape[0]
    tm, tf = 1024, 512
    nj = d_ff // tf
    kern = functools.partial(_ffn_kernel, tm=tm, tiles_per_seq=seq_len // tm)
    return pl.pallas_call(
        kern,
        grid=(m // tm, nj),
        in_specs=[
            pl.BlockSpec((tm, d), lambda i, j: (i, 0)),
            pl.BlockSpec((tm, d), lambda i, j: (i, 0)),
            pl.BlockSpec((d, tf), lambda i, j: (0, j)),
            pl.BlockSpec((d, tf), lambda i, j: (0, nj + j)),
            pl.BlockSpec((CONV_WIDTH, tf), lambda i, j: (0, j)),
            pl.BlockSpec((CONV_WIDTH, tf), lambda i, j: (0, nj + j)),
            pl.BlockSpec((1, tf), lambda i, j: (0, j)),
            pl.BlockSpec((1, tf), lambda i, j: (0, nj + j)),
            pl.BlockSpec((tf, d), lambda i, j: (j, 0)),
        ],
        out_specs=pl.BlockSpec((tm, d), lambda i, j: (i, 0)),
        out_shape=jax.ShapeDtypeStruct((m, d), F32),
        scratch_shapes=[
            pltpu.VMEM((tm + SUBLANES, tf), F32),
            pltpu.VMEM((tm + SUBLANES, tf), F32),
            pltpu.VMEM((nj, SUBLANES, tf), F32),
            pltpu.VMEM((nj, SUBLANES, tf), F32),
        ],
        compiler_params=_params(("arbitrary", "arbitrary"), 62 * 1024 * 1024),
        name="conv_ffn",
    )(h2, x1, w_up, w_up, w_conv, w_conv, b_conv, b_conv, w_down)


def _layer(x, g_attn, w_in, b_forget, g_q_fox, g_k_fox, g_q_dil, g_k_dil,
           w_br_fox, w_br_dil, w_out, g_ffn, w_up, w_conv, b_conv, w_down):
    b, s, d = x.shape
    w_fox = N_HEADS_FOX * HEAD_DIM
    w_dil = N_HEADS_DIL * HEAD_DIM
    n_qkv = 3 * w_fox + 3 * w_dil
    f_lo, f_hi = 3 * w_fox, 3 * w_fox + N_HEADS_FOX

    ones = jnp.ones((w_fox,), F32)
    g_qk = jnp.concatenate([g_q_fox.reshape(-1) * QK_SCALE2, g_k_fox.reshape(-1), ones,
                            g_q_dil.reshape(-1) * QK_SCALE2, g_k_dil.reshape(-1), ones]
                           ).reshape(1, n_qkv)
    b_pad = jnp.pad(b_forget, (0, LANES - N_HEADS_FOX)).reshape(1, LANES)
    slopes = jnp.asarray(2.0 ** (-8.0 * np.arange(1, N_HEADS_DIL + 1) / N_HEADS_DIL), dtype=F32)

    x2 = x.reshape(b * s, d)
    qkv, gates, f_pre = _in_proj(x2, g_attn.reshape(1, d), w_in.T, g_qk,
                                 f_lo=f_lo, f_hi=f_hi, n_qkv=n_qkv, n_gate=2 * d)
    qkv = qkv.reshape(b, s, n_qkv)

    ccol, crow = _forget(f_pre.reshape(b, s, LANES), b_pad, n_heads=N_HEADS_FOX)
    crow4 = crow.reshape(b, N_HEADS_FOX, 1, s)
    assert N_HEADS_FOX == N_HEADS_DIL
    o_a, o_b, w_up_b, w_down_b, w_out_b, w_br_fox_b, w_br_dil_b = _attention(
        qkv, ccol, crow4, slopes, [w_up, w_down, w_out, w_br_fox, w_br_dil],
        n_heads=N_HEADS_FOX, tq=256)

    x1, h2 = _mix(o_a.reshape(b * s, w_fox), o_b.reshape(b * s, w_dil), gates, x2,
                  w_br_fox_b, w_br_dil_b, w_out_b, g_ffn.reshape(1, d))
    out = _ffn(h2, x1, w_up_b, w_conv, b_conv.reshape(1, -1), w_down_b, seq_len=s)
    return out.reshape(b, s, d)


def kernel(x, g_attn, w_in, b_forget, g_q_fox, g_k_fox, g_q_dil, g_k_dil, w_br_fox, w_br_dil,
           w_out, g_ffn, w_up, w_conv, b_conv, w_down):
    for l in range(w_in.shape[0]):
        x = _layer(x, g_attn[l], w_in[l], b_forget[l], g_q_fox[l], g_k_fox[l], g_q_dil[l],
                   g_k_dil[l], w_br_fox[l], w_br_dil[l], w_out[l], g_ffn[l], w_up[l], w_conv[l],
                   b_conv[l], w_down[l])
    return x
```

```python
import functools

import numpy as np
import jax
import jax.numpy as jnp
from jax import lax
from jax.experimental import pallas as pl
from jax.experimental.pallas import tpu as pltpu

HEAD_DIM = 128
N_HEADS_FOX = 8
N_HEADS_DIL = 8
DIL_PATTERNS = ((128, 1), (512, 4), (2048, 16))
CONV_WIDTH = 3
PROJ_ROW_PARTS = 4
FFN_ROW_PARTS = 2
EPS = 1e-6
NEG_INF = -1e30
LOG2E = float(np.log2(np.e))
QK_SCALE2 =LOG2E / float(np.sqrt(HEAD_DIM))

F32 = jnp.float32
BF16 = jnp.bfloat16

LANES = 128
SUBLANES = 8
BF16_ROWS = 2 * SUBLANES
VMEM_LIMIT = 56 * 1024 * 1024

NT_DIMS = (((1,), (1,)), ((), ()))


def _params(sem, vmem_limit=VMEM_LIMIT):
    return pltpu.CompilerParams(dimension_semantics=sem, vmem_limit_bytes=vmem_limit)


def _in_proj_kernel(x_ref, g_ref, w_ref, wf_ref, gqk_ref, qkv_ref, gate_ref, f_ref, h_scr,
                    *, n_qkv_tiles, tiles_per_group, heads_per_tile):
    j = pl.program_id(1)

    def project(h, w_t):
        return lax.dot_general(h, w_t.astype(BF16), NT_DIMS, preferred_element_type=F32)

    @pl.when(j == 0)
    def _():
        x = x_ref[...]
        ms = jnp.mean(x * x, axis=-1, keepdims=True)
        h = (x * lax.rsqrt(ms + EPS) * g_ref[...]).astype(BF16)
        h_scr[...] = h
        wf = wf_ref[...]
        wf = jnp.concatenate([wf, jnp.zeros((LANES - wf.shape[0], wf.shape[1]), F32)], axis=0)
        f_ref[...] = project(h, wf)

    def store_qk(acc, rows):
        for hh in range(heads_per_tile):
            sl = slice(hh * HEAD_DIM, (hh + 1) * HEAD_DIM)
            a = acc[:, sl]
            ms = jnp.mean(a * a, axis=-1, keepdims=True)
            qkv_ref[rows, sl] = (a * lax.rsqrt(ms + EPS) * gqk_ref[:, sl]).astype(BF16)

    def store_v(acc, rows):
        qkv_ref[rows, :] = acc.astype(BF16)

    def store_gate(acc, rows):
        gate_ref[rows, :] = acc

    def tile(store):
        w = w_ref[...].astype(BF16)
        part = h_scr.shape[0] // PROJ_ROW_PARTS
        for r0 in range(0, h_scr.shape[0], part):
            rows = slice(r0, r0 + part)
            store(lax.dot_general(h_scr[rows, :], w, NT_DIMS, preferred_element_type=F32), rows)

    is_qkv = j < n_qkv_tiles
    is_qk = jnp.logical_and(is_qkv, (j % (3 * tiles_per_group)) < 2 * tiles_per_group)
    cases = (
        (is_qk, store_qk),
        (jnp.logical_and(is_qkv, jnp.logical_not(is_qk)), store_v),
        (jnp.logical_not(is_qkv), store_gate),
    )
    for cond, store in cases:
        pl.when(cond)(functools.partial(tile, store))


def _in_proj(x2, g_attn, w_t, g_qk, *, f_lo, f_hi, n_qkv, n_gate):
    m, d = x2.shape
    tm, tn = 1024, 1024
    assert f_lo % tn == 0 and f_hi % SUBLANES == 0
    n_qkv_tiles = n_qkv // tn
    n_tiles = (n_qkv + n_gate) // tn
    kern = functools.partial(
        _in_proj_kernel, n_qkv_tiles=n_qkv_tiles,
        tiles_per_group=(N_HEADS_FOX * HEAD_DIM) // tn, heads_per_tile=tn // HEAD_DIM)
    w_row = lambda j: pl.multiple_of(j * tn + jnp.where(j * tn >= f_lo, f_hi - f_lo, 0), SUBLANES)
    return pl.pallas_call(
        kern,
        grid=(m // tm, n_tiles),
        in_specs=[
            pl.BlockSpec((tm, d), lambda i, j: (i, 0)),
            pl.BlockSpec((1, d), lambda i, j: (0, 0)),
            pl.BlockSpec((pl.Element(tn), pl.Element(d)), lambda i, j: (w_row(j), 0)),
            pl.BlockSpec((pl.Element(f_hi - f_lo), pl.Element(d)), lambda i, j: (f_lo, 0)),
            pl.BlockSpec((1, tn), lambda i, j: (0, jnp.minimum(j, n_qkv_tiles - 1))),
        ],
        out_specs=[
            pl.BlockSpec((tm, tn), lambda i, j: (i, jnp.minimum(j, n_qkv_tiles - 1))),
            pl.BlockSpec((tm, tn), lambda i, j: (i, jnp.maximum(j - n_qkv_tiles, 0))),
            pl.BlockSpec((tm, LANES), lambda i, j: (i, 0)),
        ],
        out_shape=[
            jax.ShapeDtypeStruct((m, n_qkv), BF16),
            jax.ShapeDtypeStruct((m, n_gate), F32),
            jax.ShapeDtypeStruct((m, LANES), F32),
        ],
        scratch_shapes=[pltpu.VMEM((tm, d), BF16)],
        compiler_params=_params(("arbitrary", "arbitrary")),
        name="in_proj",
    )(x2, g_attn, w_t, w_t, g_qk)


def _forget_kernel(f_ref, b_ref, ccol_ref, crow_ref, *, n_heads):
    z = f_ref[0] + b_ref[...]
    c = jax.nn.log_sigmoid(z)
    s = c.shape[0]
    row = lax.broadcasted_iota(jnp.int32, c.shape, 0)
    k = 1
    while k < s:
        c = c + jnp.where(row >= k, pltpu.roll(c, k, axis=0), 0.0)
        k *= 2
    ccol_ref[0] = c
    crow_ref[0] = c.T[:n_heads, :]


def _forget(f_pre, b_pad, *, n_heads):
    b, s, _ = f_pre.shape
    return pl.pallas_call(
        functools.partial(_forget_kernel, n_heads=n_heads),
        grid=(b,),
        in_specs=[
            pl.BlockSpec((1, s, LANES), lambda i: (i, 0, 0)),
            pl.BlockSpec((1, LANES), lambda i: (0, 0)),
        ],
        out_specs=[
            pl.BlockSpec((1, s, LANES), lambda i: (i, 0, 0)),
            pl.BlockSpec((1, n_heads, s), lambda i: (i, 0, 0)),
        ],
        out_shape=[
            jax.ShapeDtypeStruct((b, s, LANES), F32),
            jax.ShapeDtypeStruct((b, n_heads, s), F32),
        ],
        compiler_params=_params(("arbitrary",)),
        name="forget_cumsum",
    )(f_pre, b_pad)


def _causal_sweeps(streams, tq):
    s_len = streams[0][0].shape[1]
    n_q = s_len // tq

    def logits(stream, qi):
        q_ref, k_ref, _, _, add_bias = stream
        q0, kv = qi * tq, (qi + 1) * tq
        s = lax.dot_general(q_ref[0, q0:kv, :], k_ref[0, :kv, :], NT_DIMS,
                            preferred_element_type=F32)
        tiles = [add_bias(s[:, kj * tq:(kj + 1) * tq], qi, kj) for kj in range(qi + 1)]
        z = jnp.concatenate(tiles, axis=1) if qi else tiles[0]
        return z, jnp.max(z, axis=1, keepdims=True)

    def weigh(stream, qi, z, m):
        _, _, v_aug, o_ref, _ = stream
        q0, kv = qi * tq, (qi + 1) * tq
        p = jnp.exp2(z - m).astype(BF16)
        o = jnp.dot(p, v_aug[0:kv, :], preferred_element_type=F32)
        o_ref[0, q0:kv, :] = (o[:, :HEAD_DIM] / o[:, HEAD_DIM:HEAD_DIM + 1]).astype(o_ref.dtype)

    pending = [logits(stream, 0) for stream in streams]
    for qi in range(n_q):
        for si, stream in enumerate(streams):
            upcoming = logits(stream, qi + 1) if qi + 1 < n_q else None
            weigh(stream, qi, *pending[si])
            pending[si] = upcoming


def _fill_v_aug(v_ref, v_aug):
    v_aug[:, :HEAD_DIM] = v_ref[0]
    lane = lax.broadcasted_iota(jnp.int32, (v_aug.shape[0], v_aug.shape[1] - HEAD_DIM), 1)
    v_aug[:, HEAD_DIM:] = jnp.where(lane == 0, 1.0, 0.0).astype(BF16)


def _cast_specs(weights, n_steps, step_index):
    in_specs, out_specs, out_shapes = [], [], []
    for w in weights:
        rows, cols = w.shape[0] // n_steps, w.shape[1]
        assert rows * n_steps == w.shape[0] and rows % BF16_ROWS == 0
        spec = pl.BlockSpec((rows, cols), lambda *idx: (step_index(*idx), 0))
        in_specs.append(spec)
        out_specs.append(spec)
        out_shapes.append(jax.ShapeDtypeStruct(w.shape, BF16))
    return in_specs, out_specs, out_shapes


def _attention_kernel(slopes_ref, qa_ref, ka_ref, va_ref, qb_ref, kb_ref, vb_ref, ccol_ref, crow_ref,
                      *rest, tq):
    n_cast = (len(rest) - 5) // 2
    oa_ref, ob_ref = rest[n_cast], rest[n_cast + 1]
    bias_scr, va_aug, vb_aug = rest[-3:]
    _fill_v_aug(va_ref, va_aug)
    _fill_v_aug(vb_ref, vb_aug)
    h = pl.program_id(0)
    r = lax.broadcasted_iota(jnp.int32, (tq, tq), 0)
    c = lax.broadcasted_iota(jnp.int32, (tq, tq), 1)

    @pl.when(pl.program_id(1) == 0)
    def _():
        slope = slopes_ref[h]
        for d in range(bias_scr.shape[0]):
            delta = r - c + d * tq
            count = jnp.zeros((tq, tq), F32)
            for window, dil in DIL_PATTERNS:
                assert dil & (dil - 1) == 0
                member = jnp.logical_and((delta & (dil - 1)) == 0, delta <= window)
                count = count + jnp.where(member, 1.0, 0.0)
            valid = jnp.logical_and(delta >= 0, count > 0.0)
            bias = jnp.log(jnp.maximum(count, 1.0)) - slope * delta.astype(F32)
            bias_scr[d] = jnp.where(valid, bias, NEG_INF) * LOG2E

    s_len = qa_ref.shape[1]
    lane = lax.broadcasted_iota(jnp.int32, (s_len, LANES), 1)
    cq = jnp.sum(jnp.where(lane == h, ccol_ref[0], 0.0), axis=1, keepdims=True) * LOG2E
    ck = crow_ref[0, 0] * LOG2E
    causal = c <= r

    def forget_bias(tile, qi, kj):
        z = tile + cq[qi * tq:(qi + 1) * tq] - ck[:, kj * tq:(kj + 1) * tq]
        return jnp.where(causal, z, NEG_INF) if kj == qi else z

    def distance_bias(tile, qi, kj):
        return tile + bias_scr[qi - kj]

    _causal_sweeps([(qa_ref, ka_ref, va_aug, oa_ref, forget_bias),
                    (qb_ref, kb_ref, vb_aug, ob_ref, distance_bias)], tq)

    for i_ref, o_ref in zip(rest[:n_cast], rest[n_cast + 2:-3]):
        o_ref[...] = i_ref[...].astype(BF16)


def _attention(qkv, ccol, crow4, slopes, cast_weights, *, n_heads, tq):
    b, s, _ = qkv.shape
    spec = lambda group: pl.BlockSpec((1, s, HEAD_DIM),
                                      lambda h, bi, sl: (bi, 0, group * n_heads + h))
    out_spec = pl.BlockSpec((1, s, HEAD_DIM), lambda h, bi, sl: (bi, 0, h))
    c_in, c_out, c_shapes = _cast_specs(cast_weights, b * n_heads, lambda h, bi, sl: h * b + bi)
    o_shape = jax.ShapeDtypeStruct((b, s, n_heads * HEAD_DIM), BF16)
    return pl.pallas_call(
        functools.partial(_attention_kernel, tq=tq),
        grid_spec=pltpu.PrefetchScalarGridSpec(
            num_scalar_prefetch=1,
            grid=(n_heads, b),
            in_specs=[spec(g) for g in range(6)] + [
                pl.BlockSpec((1, s, LANES), lambda h, bi, sl: (bi, 0, 0)),
                pl.BlockSpec((1, 1, 1, s), lambda h, bi, sl: (bi, h, 0, 0)),
            ] + c_in,
            out_specs=[out_spec, out_spec] + c_out,
            scratch_shapes=[pltpu.VMEM((s // tq, tq, tq), F32),
                            pltpu.VMEM((s, 2 * HEAD_DIM), BF16),
                            pltpu.VMEM((s, 2 * HEAD_DIM), BF16)],
        ),
        out_shape=[o_shape, o_shape] + c_shapes,
        compiler_params=_params(("arbitrary", "arbitrary")),
        name="attention",
    )(slopes, *([qkv] * 6), ccol, crow4, *cast_weights)


def _mix_kernel(oa_ref, ob_ref, ga_ref, gb_ref, x_ref, wa_ref, wb_ref, wo_ref, g_ref,
                x1_ref, h2_ref):
    ta = jnp.dot(oa_ref[...], wa_ref[...], preferred_element_type=F32)
    tb = jnp.dot(ob_ref[...], wb_ref[...], preferred_element_type=F32)
    merged = (jax.nn.sigmoid(ga_ref[...]) * ta + jax.nn.sigmoid(gb_ref[...]) * tb).astype(BF16)
    x1 = x_ref[...] + jnp.dot(merged, wo_ref[...], preferred_element_type=F32)
    x1_ref[...] = x1
    ms = jnp.mean(x1 * x1, axis=-1, keepdims=True)
    h2_ref[...] = (x1 * lax.rsqrt(ms + EPS) * g_ref[...]).astype(BF16)


def _mix(oa, ob, gates, x2, w_a, w_b, w_o, g_ffn):
    m, d = x2.shape
    wa_rows, wb_rows = w_a.shape[0], w_b.shape[0]
    tm = 256
    resident = lambda shape: pl.BlockSpec(shape, lambda i: (0, 0), pipeline_mode=pl.Buffered(1))
    return pl.pallas_call(
        _mix_kernel,
        grid=(m // tm,),
        in_specs=[
            pl.BlockSpec((tm, wa_rows), lambda i: (i, 0)),
            pl.BlockSpec((tm, wb_rows), lambda i: (i, 0)),
            pl.BlockSpec((tm, d), lambda i: (i, 0)),
            pl.BlockSpec((tm, d), lambda i: (i, 1)),
            pl.BlockSpec((tm, d), lambda i: (i, 0)),
            resident((wa_rows, d)),
            resident((wb_rows, d)),
            resident((d, d)),
            pl.BlockSpec((1, d), lambda i: (0, 0)),
        ],
        out_specs=[
            pl.BlockSpec((tm, d), lambda i: (i, 0)),
            pl.BlockSpec((tm, d), lambda i: (i, 0)),
        ],
        out_shape=[
            jax.ShapeDtypeStruct((m, d), F32),
            jax.ShapeDtypeStruct((m, d), BF16),
        ],
        compiler_params=_params(("arbitrary",)),
        name="branch_mix_out_proj",
    )(oa, ob, gates, gates, x2, w_a, w_b, w_o, g_ffn)


def _ffn_kernel(h_ref, x1_ref, wg_ref, wv_ref, cg_ref, cv_ref, bg_ref, bv_ref, wd_ref,
                out_ref, ug_scr, uv_scr, carry_g, carry_v, *, tm, tiles_per_seq):
    i = pl.program_id(0)
    j = pl.program_id(1)
    first = (i % tiles_per_seq) == 0

    @pl.when(jnp.logical_and(i == 0, j == 0))
    def _():
        carry_g[...] = jnp.zeros(carry_g.shape, F32)
        carry_v[...] = jnp.zeros(carry_v.shape, F32)

    @pl.when(j == 0)
    def _():
        out_ref[...] = x1_ref[...]

    rows = tm // FFN_ROW_PARTS

    def up(w_ref, u_scr, r0):
        u_scr[SUBLANES + r0:SUBLANES + r0 + rows, :] = jnp.dot(
            h_ref[r0:r0 + rows, :], w_ref[...], preferred_element_type=F32)

    def conv(c_ref, b_ref, u_scr, r0):
        out = b_ref[...]
        for t in range(CONV_WIDTH):
            shift = CONV_WIDTH - 1 - t
            out = out + c_ref[t:t + 1, :] * u_scr[pl.ds(SUBLANES - shift + r0, rows), :]
        return out

    for u_scr, carry in ((ug_scr, carry_g), (uv_scr, carry_v)):
        prev = carry[j]
        u_scr[0:SUBLANES, :] = jnp.where(first, jnp.zeros_like(prev), prev)
    for r0 in range(0, tm, rows):
        up(wg_ref, ug_scr, r0)
        up(wv_ref, uv_scr, r0)
    carry_g[j] = ug_scr[tm:tm + SUBLANES, :]
    carry_v[j] = uv_scr[tm:tm + SUBLANES, :]
    for r0 in range(0, tm, rows):
        gate = conv(cg_ref, bg_ref, ug_scr, r0)
        val = conv(cv_ref, bv_ref, uv_scr, r0)
        a = (gate * jax.nn.sigmoid(gate) * val).astype(BF16)
        out_ref[r0:r0 + rows, :] += jnp.dot(a, wd_ref[...], preferred_element_type=F32)


def _ffn(h2, x1, w_up, w_conv, b_conv, w_down, *, seq_len):
    m, d = h2.shape
    d_ff = w_down.shape[0]
    tm, tf = 1024, 512
    nj = d_ff // tf
    kern = functools.partial(_ffn_kernel, tm=tm, tiles_per_seq=seq_len // tm)
    return pl.pallas_call(
        kern,
        grid=(m // tm, nj),
        in_specs=[
            pl.BlockSpec((tm, d), lambda i, j: (i, 0)),
            pl.BlockSpec((tm, d), lambda i, j: (i, 0)),
            pl.BlockSpec((d, tf), lambda i, j: (0, j)),
            pl.BlockSpec((d, tf), lambda i, j: (0, nj + j)),
            pl.BlockSpec((CONV_WIDTH, tf), lambda i, j: (0, j)),
            pl.BlockSpec((CONV_WIDTH, tf), lambda i, j: (0, nj + j)),
            pl.BlockSpec((1, tf), lambda i, j: (0, j)),
            pl.BlockSpec((1, tf), lambda i, j: (0, nj + j)),
            pl.BlockSpec((tf, d), lambda i, j: (j, 0)),
        ],
        out_specs=pl.BlockSpec((tm, d), lambda i, j: (i, 0)),
        out_shape=jax.ShapeDtypeStruct((m, d), F32),
        scratch_shapes=[
            pltpu.VMEM((tm + SUBLANES, tf), F32),
            pltpu.VMEM((tm + SUBLANES, tf), F32),
            pltpu.VMEM((nj, SUBLANES, tf), F32),
            pltpu.VMEM((nj, SUBLANES, tf), F32),
        ],
        compiler_params=_params(("arbitrary", "arbitrary"), 62 * 1024 * 1024),
        name="conv_ffn",
    )(h2, x1, w_up, w_up, w_conv, w_conv, b_conv, b_conv, w_down)


def _layer(x, g_attn, w_in, b_forget, g_q_fox, g_k_fox, g_q_dil, g_k_dil,
           w_br_fox, w_br_dil, w_out, g_ffn, w_up, w_conv, b_conv, w_down):
    b, s, d = x.shape
    w_fox = N_HEADS_FOX * HEAD_DIM
    w_dil = N_HEADS_DIL * HEAD_DIM
    n_qkv = 3 * w_fox + 3 * w_dil
    f_lo, f_hi = 3 * w_fox, 3 * w_fox + N_HEADS_FOX

    ones = jnp.ones((w_fox,), F32)
    g_qk = jnp.concatenate([g_q_fox.reshape(-1) * QK_SCALE2, g_k_fox.reshape(-1), ones,
                            g_q_dil.reshape(-1) * QK_SCALE2, g_k_dil.reshape(-1), ones]
                           ).reshape(1, n_qkv)
    b_pad = jnp.pad(b_forget, (0, LANES - N_HEADS_FOX)).reshape(1, LANES)
    slopes = jnp.asarray(2.0 ** (-8.0 * np.arange(1, N_HEADS_DIL + 1) / N_HEADS_DIL), dtype=F32)

    x2 = x.reshape(b * s, d)
    qkv, gates, f_pre = _in_proj(x2, g_attn.reshape(1, d), w_in.T, g_qk,
                                 f_lo=f_lo, f_hi=f_hi, n_qkv=n_qkv, n_gate=2 * d)
    qkv = qkv.reshape(b, s, n_qkv)

    ccol, crow = _forget(f_pre.reshape(b, s, LANES), b_pad, n_heads=N_HEADS_FOX)
    crow4 = crow.reshape(b, N_HEADS_FOX, 1, s)
    assert N_HEADS_FOX == N_HEADS_DIL
    o_a, o_b, w_up_b, w_down_b, w_out_b, w_br_fox_b, w_br_dil_b = _attention(
        qkv, ccol, crow4, slopes, [w_up, w_down, w_out, w_br_fox, w_br_dil],
        n_heads=N_HEADS_FOX, tq=256)

    x1, h2 = _mix(o_a.reshape(b * s, w_fox), o_b.reshape(b * s, w_dil), gates, x2,
                  w_br_fox_b, w_br_dil_b, w_out_b, g_ffn.reshape(1, d))
    out = _ffn(h2, x1, w_up_b, w_conv, b_conv.reshape(1, -1), w_down_b, seq_len=s)
    return out.reshape(b, s, d)


def kernel(x, g_attn, w_in, b_forget, g_q_fox, g_k_fox, g_q_dil, g_k_dil, w_br_fox, w_br_dil,
           w_out, g_ffn, w_up, w_conv, b_conv, w_down):
    for l in range(w_in.shape[0]):
        x = _layer(x, g_attn[l], w_in[l], b_forget[l], g_q_fox[l], g_k_fox[l], g_q_dil[l],
                   g_k_dil[l], w_br_fox[l], w_br_dil[l], w_out[l], g_ffn[l], w_up[l], w_conv[l],
                   b_conv[l], w_down[l])
    return x
```

```python
import functools

import numpy as np
import jax
import jax.numpy as jnp
from jax import lax
from jax.experimental import pallas as pl
from jax.experimental.pallas import tpu as pltpu

HEAD_DIM = 128
N_HEADS_FOX = 8
N_HEADS_DIL = 8
DIL_PATTERNS = ((128, 1), (512, 4), (2048, 16))
CONV_WIDTH = 3
PROJ_ROW_PARTS = 4
FFN_ROW_PARTS = 2
EPS = 1e-6
NEG_INF = -1e30
LOG2E = float(np.log2(np.e))
QK_SCALE2 =LOG2E / float(np.sqrt(HEAD_DIM))

F32 = jnp.float32
BF16 = jnp.bfloat16

LANES = 128
SUBLANES = 8
BF16_ROWS = 2 * SUBLANES
VMEM_LIMIT = 56 * 1024 * 1024

NT_DIMS = (((1,), (1,)), ((), ()))


def _params(sem, vmem_limit=VMEM_LIMIT):
    return pltpu.CompilerParams(dimension_semantics=sem, vmem_limit_bytes=vmem_limit)


def _in_proj_kernel(x_ref, g_ref, w_ref, wf_ref, gqk_ref, qkv_ref, gate_ref, f_ref, h_scr,
                    *, n_qkv_tiles, tiles_per_group, heads_per_tile):
    j = pl.program_id(1)

    def project(h, w_t):
        return lax.dot_general(h, w_t.astype(BF16), NT_DIMS, preferred_element_type=F32)

    @pl.when(j == 0)
    def _():
        x = x_ref[...]
        ms = jnp.mean(x * x, axis=-1, keepdims=True)
        h = (x * lax.rsqrt(ms + EPS) * g_ref[...]).astype(BF16)
        h_scr[...] = h
        wf = wf_ref[...]
        wf = jnp.concatenate([wf, jnp.zeros((LANES - wf.shape[0], wf.shape[1]), F32)], axis=0)
        f_ref[...] = project(h, wf)

    def store_qk(acc, rows):
        for hh in range(heads_per_tile):
            sl = slice(hh * HEAD_DIM, (hh + 1) * HEAD_DIM)
            a = acc[:, sl]
            ms = jnp.mean(a * a, axis=-1, keepdims=True)
            qkv_ref[rows, sl] = (a * lax.rsqrt(ms + EPS) * gqk_ref[:, sl]).astype(BF16)

    def store_v(acc, rows):
        qkv_ref[rows, :] = acc.astype(BF16)

    def store_gate(acc, rows):
        gate_ref[rows, :] = acc

    def tile(store):
        w = w_ref[...].astype(BF16)
        part = h_scr.shape[0] // PROJ_ROW_PARTS
        for r0 in range(0, h_scr.shape[0], part):
            rows = slice(r0, r0 + part)
            store(lax.dot_general(h_scr[rows, :], w, NT_DIMS, preferred_element_type=F32), rows)

    is_qkv = j < n_qkv_tiles
    is_qk = jnp.logical_and(is_qkv, (j % (3 * tiles_per_group)) < 2 * tiles_per_group)
    cases = (
        (is_qk, store_qk),
        (jnp.logical_and(is_qkv, jnp.logical_not(is_qk)), store_v),
        (jnp.logical_not(is_qkv), store_gate),
    )
    for cond, store in cases:
        pl.when(cond)(functools.partial(tile, store))


def _in_proj(x2, g_attn, w_t, g_qk, *, f_lo, f_hi, n_qkv, n_gate):
    m, d = x2.shape
    tm, tn = 2048, 512
    assert f_lo % tn == 0 and f_hi % SUBLANES == 0
    n_qkv_tiles = n_qkv // tn
    n_tiles = (n_qkv + n_gate) // tn
    kern = functools.partial(
        _in_proj_kernel, n_qkv_tiles=n_qkv_tiles,
        tiles_per_group=(N_HEADS_FOX * HEAD_DIM) // tn, heads_per_tile=tn // HEAD_DIM)
    w_row = lambda j: pl.multiple_of(j * tn + jnp.where(j * tn >= f_lo, f_hi - f_lo, 0), SUBLANES)
    return pl.pallas_call(
        kern,
        grid=(m // tm, n_tiles),
        in_specs=[
            pl.BlockSpec((tm, d), lambda i, j: (i, 0), pipeline_mode=pl.Buffered(1)),
            pl.BlockSpec((1, d), lambda i, j: (0, 0)),
            pl.BlockSpec((pl.Element(tn), pl.Element(d)), lambda i, j: (w_row(j), 0)),
            pl.BlockSpec((pl.Element(f_hi - f_lo), pl.Element(d)), lambda i, j: (f_lo, 0)),
            pl.BlockSpec((1, tn), lambda i, j: (0, jnp.minimum(j, n_qkv_tiles - 1))),
        ],
        out_specs=[
            pl.BlockSpec((tm, tn), lambda i, j: (i, jnp.minimum(j, n_qkv_tiles - 1))),
            pl.BlockSpec((tm, tn), lambda i, j: (i, jnp.maximum(j - n_qkv_tiles, 0))),
            pl.BlockSpec((tm, LANES), lambda i, j: (i, 0)),
        ],
        out_shape=[
            jax.ShapeDtypeStruct((m, n_qkv), BF16),
            jax.ShapeDtypeStruct((m, n_gate), F32),
            jax.ShapeDtypeStruct((m, LANES), F32),
        ],
        scratch_shapes=[pltpu.VMEM((tm, d), BF16)],
        compiler_params=_params(("arbitrary", "arbitrary")),
        name="in_proj",
    )(x2, g_attn, w_t, w_t, g_qk)


def _forget_kernel(f_ref, b_ref, ccol_ref, crow_ref, *, n_heads):
    z = f_ref[0] + b_ref[...]
    c = jax.nn.log_sigmoid(z)
    s = c.shape[0]
    row = lax.broadcasted_iota(jnp.int32, c.shape, 0)
    k = 1
    while k < s:
        c = c + jnp.where(row >= k, pltpu.roll(c, k, axis=0), 0.0)
        k *= 2
    ccol_ref[0] = c
    crow_ref[0] = c.T[:n_heads, :]


def _forget(f_pre, b_pad, *, n_heads):
    b, s, _ = f_pre.shape
    return pl.pallas_call(
        functools.partial(_forget_kernel, n_heads=n_heads),
        grid=(b,),
        in_specs=[
            pl.BlockSpec((1, s, LANES), lambda i: (i, 0, 0)),
            pl.BlockSpec((1, LANES), lambda i: (0, 0)),
        ],
        out_specs=[
            pl.BlockSpec((1, s, LANES), lambda i: (i, 0, 0)),
            pl.BlockSpec((1, n_heads, s), lambda i: (i, 0, 0)),
        ],
        out_shape=[
            jax.ShapeDtypeStruct((b, s, LANES), F32),
            jax.ShapeDtypeStruct((b, n_heads, s), F32),
        ],
        compiler_params=_params(("arbitrary",)),
        name="forget_cumsum",
    )(f_pre, b_pad)


def _causal_sweeps(streams, tq):
    s_len = streams[0][0].shape[1]
    n_q = s_len // tq

    def logits(stream, qi):
        q_ref, k_ref, _, _, add_bias = stream
        q0, kv = qi * tq, (qi + 1) * tq
        s = lax.dot_general(q_ref[0, q0:kv, :], k_ref[0, :kv, :], NT_DIMS,
                            preferred_element_type=F32)
        tiles = [add_bias(s[:, kj * tq:(kj + 1) * tq], qi, kj) for kj in range(qi + 1)]
        z = jnp.concatenate(tiles, axis=1) if qi else tiles[0]
        return z, jnp.max(z, axis=1, keepdims=True)

    def weigh(stream, qi, z, m):
        _, _, v_aug, o_ref, _ = stream
        q0, kv = qi * tq, (qi + 1) * tq
        p = jnp.exp2(z - m).astype(BF16)
        o = jnp.dot(p, v_aug[0:kv, :], preferred_element_type=F32)
        o_ref[0, q0:kv, :] = (o[:, :HEAD_DIM] / o[:, HEAD_DIM:HEAD_DIM + 1]).astype(o_ref.dtype)

    pending = [logits(stream, 0) for stream in streams]
    for qi in range(n_q):
        for si, stream in enumerate(streams):
            upcoming = logits(stream, qi + 1) if qi + 1 < n_q else None
            weigh(stream, qi, *pending[si])
            pending[si] = upcoming


def _fill_v_aug(v_ref, v_aug):
    v_aug[:, :HEAD_DIM] = v_ref[0]
    lane = lax.broadcasted_iota(jnp.int32, (v_aug.shape[0], v_aug.shape[1] - HEAD_DIM), 1)
    v_aug[:, HEAD_DIM:] = jnp.where(lane == 0, 1.0, 0.0).astype(BF16)


def _cast_specs(weights, n_steps, step_index):
    in_specs, out_specs, out_shapes = [], [], []
    for w in weights:
        rows, cols = w.shape[0] // n_steps, w.shape[1]
        assert rows * n_steps == w.shape[0] and rows % BF16_ROWS == 0
        spec = pl.BlockSpec((rows, cols), lambda *idx: (step_index(*idx), 0))
        in_specs.append(spec)
        out_specs.append(spec)
        out_shapes.append(jax.ShapeDtypeStruct(w.shape, BF16))
    return in_specs, out_specs, out_shapes


def _attention_kernel(slopes_ref, qa_ref, ka_ref, va_ref, qb_ref, kb_ref, vb_ref, ccol_ref, crow_ref,
                      *rest, tq):
    n_cast = (len(rest) - 5) // 2
    oa_ref, ob_ref = rest[n_cast], rest[n_cast + 1]
    bias_scr, va_aug, vb_aug = rest[-3:]
    _fill_v_aug(va_ref, va_aug)
    _fill_v_aug(vb_ref, vb_aug)
    h = pl.program_id(0)
    r = lax.broadcasted_iota(jnp.int32, (tq, tq), 0)
    c = lax.broadcasted_iota(jnp.int32, (tq, tq), 1)

    @pl.when(pl.program_id(1) == 0)
    def _():
        slope = slopes_ref[h]
        for d in range(bias_scr.shape[0]):
            delta = r - c + d * tq
            count = jnp.zeros((tq, tq), F32)
            for window, dil in DIL_PATTERNS:
                assert dil & (dil - 1) == 0
                member = jnp.logical_and((delta & (dil - 1)) == 0, delta <= window)
                count = count + jnp.where(member, 1.0, 0.0)
            valid = jnp.logical_and(delta >= 0, count > 0.0)
            bias = jnp.log(jnp.maximum(count, 1.0)) - slope * delta.astype(F32)
            bias_scr[d] = jnp.where(valid, bias, NEG_INF) * LOG2E

    s_len = qa_ref.shape[1]
    lane = lax.broadcasted_iota(jnp.int32, (s_len, LANES), 1)
    cq = jnp.sum(jnp.where(lane == h, ccol_ref[0], 0.0), axis=1, keepdims=True) * LOG2E
    ck = crow_ref[0, 0] * LOG2E
    causal = c <= r

    def forget_bias(tile, qi, kj):
        z = tile + cq[qi * tq:(qi + 1) * tq] - ck[:, kj * tq:(kj + 1) * tq]
        return jnp.where(causal, z, NEG_INF) if kj == qi else z

    def distance_bias(tile, qi, kj):
        return tile + bias_scr[qi - kj]

    _causal_sweeps([(qa_ref, ka_ref, va_aug, oa_ref, forget_bias),
                    (qb_ref, kb_ref, vb_aug, ob_ref, distance_bias)], tq)

    for i_ref, o_ref in zip(rest[:n_cast], rest[n_cast + 2:-3]):
        o_ref[...] = i_ref[...].astype(BF16)


def _attention(qkv, ccol, crow4, slopes, cast_weights, *, n_heads, tq):
    b, s, _ = qkv.shape
    spec = lambda group: pl.BlockSpec((1, s, HEAD_DIM),
                                      lambda h, bi, sl: (bi, 0, group * n_heads + h))
    out_spec = pl.BlockSpec((1, s, HEAD_DIM), lambda h, bi, sl: (bi, 0, h))
    c_in, c_out, c_shapes = _cast_specs(cast_weights, b * n_heads, lambda h, bi, sl: h * b + bi)
    o_shape = jax.ShapeDtypeStruct((b, s, n_heads * HEAD_DIM), BF16)
    return pl.pallas_call(
        functools.partial(_attention_kernel, tq=tq),
        grid_spec=pltpu.PrefetchScalarGridSpec(
            num_scalar_prefetch=1,
            grid=(n_heads, b),
            in_specs=[spec(g) for g in range(6)] + [
                pl.BlockSpec((1, s, LANES), lambda h, bi, sl: (bi, 0, 0)),
                pl.BlockSpec((1, 1, 1, s), lambda h, bi, sl: (bi, h, 0, 0)),
            ] + c_in,
            out_specs=[out_spec, out_spec] + c_out,
            scratch_shapes=[pltpu.VMEM((s // tq, tq, tq), F32),
                            pltpu.VMEM((s, 2 * HEAD_DIM), BF16),
                            pltpu.VMEM((s, 2 * HEAD_DIM), BF16)],
        ),
        out_shape=[o_shape, o_shape] + c_shapes,
        compiler_params=_params(("arbitrary", "arbitrary")),
        name="attention",
    )(slopes, *([qkv] * 6), ccol, crow4, *cast_weights)


def _mix_kernel(oa_ref, ob_ref, ga_ref, gb_ref, x_ref, wa_ref, wb_ref, wo_ref, g_ref,
                x1_ref, h2_ref):
    ta = jnp.dot(oa_ref[...], wa_ref[...], preferred_element_type=F32)
    tb = jnp.dot(ob_ref[...], wb_ref[...], preferred_element_type=F32)
    merged = (jax.nn.sigmoid(ga_ref[...]) * ta + jax.nn.sigmoid(gb_ref[...]) * tb).astype(BF16)
    x1 = x_ref[...] + jnp.dot(merged, wo_ref[...], preferred_element_type=F32)
    x1_ref[...] = x1
    ms = jnp.mean(x1 * x1, axis=-1, keepdims=True)
    h2_ref[...] = (x1 * lax.rsqrt(ms + EPS) * g_ref[...]).astype(BF16)


def _mix(oa, ob, gates, x2, w_a, w_b, w_o, g_ffn):
    m, d = x2.shape
    wa_rows, wb_rows = w_a.shape[0], w_b.shape[0]
    tm = 256
    resident = lambda shape: pl.BlockSpec(shape, lambda i: (0, 0), pipeline_mode=pl.Buffered(1))
    return pl.pallas_call(
        _mix_kernel,
        grid=(m // tm,),
        in_specs=[
            pl.BlockSpec((tm, wa_rows), lambda i: (i, 0)),
            pl.BlockSpec((tm, wb_rows), lambda i: (i, 0)),
            pl.BlockSpec((tm, d), lambda i: (i, 0)),
            pl.BlockSpec((tm, d), lambda i: (i, 1)),
            pl.BlockSpec((tm, d), lambda i: (i, 0)),
            resident((wa_rows, d)),
            resident((wb_rows, d)),
            resident((d, d)),
            pl.BlockSpec((1, d), lambda i: (0, 0)),
        ],
        out_specs=[
            pl.BlockSpec((tm, d), lambda i: (i, 0)),
            pl.BlockSpec((tm, d), lambda i: (i, 0)),
        ],
        out_shape=[
            jax.ShapeDtypeStruct((m, d), F32),
            jax.ShapeDtypeStruct((m, d), BF16),
        ],
        compiler_params=_params(("arbitrary",)),
        name="branch_mix_out_proj",
    )(oa, ob, gates, gates, x2, w_a, w_b, w_o, g_ffn)


def _ffn_kernel(h_ref, x1_ref, wg_ref, wv_ref, cg_ref, cv_ref, bg_ref, bv_ref, wd_ref,
                out_ref, ug_scr, uv_scr, carry_g, carry_v, *, tm, tiles_per_seq):
    i = pl.program_id(0)
    j = pl.program_id(1)
    first = (i % tiles_per_seq) == 0

    @pl.when(jnp.logical_and(i == 0, j == 0))
    def _():
        carry_g[...] = jnp.zeros(carry_g.shape, F32)
        carry_v[...] = jnp.zeros(carry_v.shape, F32)

    @pl.when(j == 0)
    def _():
        out_ref[...] = x1_ref[...]

    rows = tm // FFN_ROW_PARTS

    def up(w_ref, u_scr, r0):
        u_scr[SUBLANES + r0:SUBLANES + r0 + rows, :] = jnp.dot(
            h_ref[r0:r0 + rows, :], w_ref[...], preferred_element_type=F32)

    def conv(c_ref, b_ref, u_scr, r0):
        out = b_ref[...]
        for t in range(CONV_WIDTH):
            shift = CONV_WIDTH - 1 - t
            out = out + c_ref[t:t + 1, :] * u_scr[pl.ds(SUBLANES - shift + r0, rows), :]
        return out

    for u_scr, carry in ((ug_scr, carry_g), (uv_scr, carry_v)):
        prev = carry[j]
        u_scr[0:SUBLANES, :] = jnp.where(first, jnp.zeros_like(prev), prev)
    for r0 in range(0, tm, rows):
        up(wg_ref, ug_scr, r0)
        up(wv_ref, uv_scr, r0)
    carry_g[j] = ug_scr[tm:tm + SUBLANES, :]
    carry_v[j] = uv_scr[tm:tm + SUBLANES, :]
    for r0 in range(0, tm, rows):
        gate = conv(cg_ref, bg_ref, ug_scr, r0)
        val = conv(cv_ref, bv_ref, uv_scr, r0)
        a = (gate * jax.nn.sigmoid(gate) * val).astype(BF16)
        out_ref[r0:r0 + rows, :] += jnp.dot(a, wd_ref[...], preferred_element_type=F32)


def _ffn(h2, x1, w_up, w_conv, b_conv, w_down, *, seq_len):
    m, d = h2.shape
    d_ff = w_down.shape[0]
    tm, tf = 1024, 512
    nj = d_ff // tf
    kern = functools.partial(_ffn_kernel, tm=tm, tiles_per_seq=seq_len // tm)
    return pl.pallas_call(
        kern,
        grid=(m // tm, nj),
        in_specs=[
            pl.BlockSpec((tm, d), lambda i, j: (i, 0)),
            pl.BlockSpec((tm, d), lambda i, j: (i, 0)),
            pl.BlockSpec((d, tf), lambda i, j: (0, j)),
            pl.BlockSpec((d, tf), lambda i, j: (0, nj + j)),
            pl.BlockSpec((CONV_WIDTH, tf), lambda i, j: (0, j)),
            pl.BlockSpec((CONV_WIDTH, tf), lambda i, j: (0, nj + j)),
            pl.BlockSpec((1, tf), lambda i, j: (0, j)),
            pl.BlockSpec((1, tf), lambda i, j: (0, nj + j)),
            pl.BlockSpec((tf, d), lambda i, j: (j, 0)),
        ],
        out_specs=pl.BlockSpec((tm, d), lambda i, j: (i, 0)),
        out_shape=jax.ShapeDtypeStruct((m, d), F32),
        scratch_shapes=[
            pltpu.VMEM((tm + SUBLANES, tf), F32),
            pltpu.VMEM((tm + SUBLANES, tf), F32),
            pltpu.VMEM((nj, SUBLANES, tf), F32),
            pltpu.VMEM((nj, SUBLANES, tf), F32),
        ],
        compiler_params=_params(("arbitrary", "arbitrary"), 62 * 1024 * 1024),
        name="conv_ffn",
    )(h2, x1, w_up, w_up, w_conv, w_conv, b_conv, b_conv, w_down)


def _layer(x, g_attn, w_in, b_forget, g_q_fox, g_k_fox, g_q_dil, g_k_dil,
           w_br_fox, w_br_dil, w_out, g_ffn, w_up, w_conv, b_conv, w_down):
    b, s, d = x.shape
    w_fox = N_HEADS_FOX * HEAD_DIM
    w_dil = N_HEADS_DIL * HEAD_DIM
    n_qkv = 3 * w_fox + 3 * w_dil
    f_lo, f_hi = 3 * w_fox, 3 * w_fox + N_HEADS_FOX

    ones = jnp.ones((w_fox,), F32)
    g_qk = jnp.concatenate([g_q_fox.reshape(-1) * QK_SCALE2, g_k_fox.reshape(-1), ones,
                            g_q_dil.reshape(-1) * QK_SCALE2, g_k_dil.reshape(-1), ones]
                           ).reshape(1, n_qkv)
    b_pad = jnp.pad(b_forget, (0, LANES - N_HEADS_FOX)).reshape(1, LANES)
    slopes = jnp.asarray(2.0 ** (-8.0 * np.arange(1, N_HEADS_DIL + 1) / N_HEADS_DIL), dtype=F32)

    x2 = x.reshape(b * s, d)
    qkv, gates, f_pre = _in_proj(x2, g_attn.reshape(1, d), w_in.T, g_qk,
                                 f_lo=f_lo, f_hi=f_hi, n_qkv=n_qkv, n_gate=2 * d)
    qkv = qkv.reshape(b, s, n_qkv)

    ccol, crow = _forget(f_pre.reshape(b, s, LANES), b_pad, n_heads=N_HEADS_FOX)
    crow4 = crow.reshape(b, N_HEADS_FOX, 1, s)
    assert N_HEADS_FOX == N_HEADS_DIL
    o_a, o_b, w_up_b, w_down_b, w_out_b, w_br_fox_b, w_br_dil_b = _attention(
        qkv, ccol, crow4, slopes, [w_up, w_down, w_out, w_br_fox, w_br_dil],
        n_heads=N_HEADS_FOX, tq=256)

    x1, h2 = _mix(o_a.reshape(b * s, w_fox), o_b.reshape(b * s, w_dil), gates, x2,
                  w_br_fox_b, w_br_dil_b, w_out_b, g_ffn.reshape(1, d))
    out = _ffn(h2, x1, w_up_b, w_conv, b_conv.reshape(1, -1), w_down_b, seq_len=s)
    return out.reshape(b, s, d)


def kernel(x, g_attn, w_in, b_forget, g_q_fox, g_k_fox, g_q_dil, g_k_dil, w_br_fox, w_br_dil,
           w_out, g_ffn, w_up, w_conv, b_conv, w_down):
    for l in range(w_in.shape[0]):
        x = _layer(x, g_attn[l], w_in[l], b_forget[l], g_q_fox[l], g_k_fox[l], g_q_dil[l],
                   g_k_dil[l], w_br_fox[l], w_br_dil[l], w_out[l], g_ffn[l], w_up[l], w_conv[l],
                   b_conv[l], w_down[l])
    return x
```

```python
import functools

import numpy as np
import jax
import jax.numpy as jnp
from jax import lax
from jax.experimental import pallas as pl
from jax.experimental.pallas import tpu as pltpu

HEAD_DIM = 128
N_HEADS_FOX = 8
N_HEADS_DIL = 8
DIL_PATTERNS = ((128, 1), (512, 4), (2048, 16))
CONV_WIDTH = 3
PROJ_ROW_PARTS = 4
FFN_ROW_PARTS = 2
EPS = 1e-6
NEG_INF = -1e30
LOG2E = float(np.log2(np.e))
QK_SCALE2 =LOG2E / float(np.sqrt(HEAD_DIM))

F32 = jnp.float32
BF16 = jnp.bfloat16

LANES = 128
SUBLANES = 8
BF16_ROWS = 2 * SUBLANES
VMEM_LIMIT = 56 * 1024 * 1024

NT_DIMS = (((1,), (1,)), ((), ()))


def _params(sem, vmem_limit=VMEM_LIMIT):
    return pltpu.CompilerParams(dimension_semantics=sem, vmem_limit_bytes=vmem_limit)


def _in_proj_kernel(x_ref, g_ref, w_ref, wf_ref, gqk_ref, qkv_ref, gate_ref, f_ref, h_scr,
                    *, n_qkv_tiles, tiles_per_group, heads_per_tile):
    s = pl.program_id(1)
    j = s - 1
    half = x_ref.shape[0]

    def project(h, w_t):
        return lax.dot_general(h, w_t.astype(BF16), NT_DIMS, preferred_element_type=F32)

    @pl.when(s <= 1)
    def _():
        rows = pl.ds(pl.multiple_of(s * half, half), half)
        x = x_ref[...]
        ms = jnp.mean(x * x, axis=-1, keepdims=True)
        h = (x * lax.rsqrt(ms + EPS) * g_ref[...]).astype(BF16)
        h_scr[rows, :] = h
        wf = wf_ref[...]
        wf = jnp.concatenate([wf, jnp.zeros((LANES - wf.shape[0], wf.shape[1]), F32)], axis=0)
        f_ref[rows, :] = project(h, wf)

    def store_qk(acc, rows):
        for hh in range(heads_per_tile):
            sl = slice(hh * HEAD_DIM, (hh + 1) * HEAD_DIM)
            a = acc[:, sl]
            ms = jnp.mean(a * a, axis=-1, keepdims=True)
            qkv_ref[rows, sl] = (a * lax.rsqrt(ms + EPS) * gqk_ref[:, sl]).astype(BF16)

    def store_v(acc, rows):
        qkv_ref[rows, :] = acc.astype(BF16)

    def store_gate(acc, rows):
        gate_ref[rows, :] = acc

    def tile(store):
        w = w_ref[...].astype(BF16)
        part = h_scr.shape[0] // PROJ_ROW_PARTS
        for r0 in range(0, h_scr.shape[0], part):
            rows = slice(r0, r0 + part)
            store(lax.dot_general(h_scr[rows, :], w, NT_DIMS, preferred_element_type=F32), rows)

    active = s >= 1
    is_qkv = jnp.logical_and(active, j < n_qkv_tiles)
    is_qk = jnp.logical_and(is_qkv, (j % (3 * tiles_per_group)) < 2 * tiles_per_group)
    cases = (
        (is_qk, store_qk),
        (jnp.logical_and(is_qkv, jnp.logical_not(is_qk)), store_v),
        (jnp.logical_and(active, j >= n_qkv_tiles), store_gate),
    )
    for cond, store in cases:
        pl.when(cond)(functools.partial(tile, store))


def _in_proj(x2, g_attn, w_t, g_qk, *, f_lo, f_hi, n_qkv, n_gate):
    m, d = x2.shape
    tm, tn = 2048, 512
    assert f_lo % tn == 0 and f_hi % SUBLANES == 0
    n_qkv_tiles = n_qkv // tn
    n_tiles = (n_qkv + n_gate) // tn
    kern = functools.partial(
        _in_proj_kernel, n_qkv_tiles=n_qkv_tiles,
        tiles_per_group=(N_HEADS_FOX * HEAD_DIM) // tn, heads_per_tile=tn // HEAD_DIM)
    tile = lambda s: jnp.maximum(s - 1, 0)
    qkv_tile = lambda s: jnp.minimum(tile(s), n_qkv_tiles - 1)
    w_row = lambda j: pl.multiple_of(j * tn + jnp.where(j * tn >= f_lo, f_hi - f_lo, 0), SUBLANES)
    return pl.pallas_call(
        kern,
        grid=(m // tm, n_tiles + 1),
        in_specs=[
            pl.BlockSpec((tm // 2, d), lambda i, s: (2 * i + jnp.minimum(s, 1), 0)),
            pl.BlockSpec((1, d), lambda i, s: (0, 0)),
            pl.BlockSpec((pl.Element(tn), pl.Element(d)), lambda i, s: (w_row(tile(s)), 0)),
            pl.BlockSpec((pl.Element(f_hi - f_lo), pl.Element(d)), lambda i, s: (f_lo, 0)),
            pl.BlockSpec((1, tn), lambda i, s: (0, qkv_tile(s))),
        ],
        out_specs=[
            pl.BlockSpec((tm, tn), lambda i, s: (i, qkv_tile(s))),
            pl.BlockSpec((tm, tn), lambda i, s: (i, jnp.maximum(tile(s) - n_qkv_tiles, 0))),
            pl.BlockSpec((tm, LANES), lambda i, s: (i, 0)),
        ],
        out_shape=[
            jax.ShapeDtypeStruct((m, n_qkv), BF16),
            jax.ShapeDtypeStruct((m, n_gate), F32),
            jax.ShapeDtypeStruct((m, LANES), F32),
        ],
        scratch_shapes=[pltpu.VMEM((tm, d), BF16)],
        compiler_params=_params(("arbitrary", "arbitrary")),
        name="in_proj",
    )(x2, g_attn, w_t, w_t, g_qk)


def _forget_kernel(f_ref, b_ref, ccol_ref, crow_ref, *, n_heads):
    z = f_ref[0] + b_ref[...]
    c = jax.nn.log_sigmoid(z)
    s = c.shape[0]
    row = lax.broadcasted_iota(jnp.int32, c.shape, 0)
    k = 1
    while k < s:
        c = c + jnp.where(row >= k, pltpu.roll(c, k, axis=0), 0.0)
        k *= 2
    ccol_ref[0] = c
    crow_ref[0] = c.T[:n_heads, :]


def _forget(f_pre, b_pad, *, n_heads):
    b, s, _ = f_pre.shape
    return pl.pallas_call(
        functools.partial(_forget_kernel, n_heads=n_heads),
        grid=(b,),
        in_specs=[
            pl.BlockSpec((1, s, LANES), lambda i: (i, 0, 0)),
            pl.BlockSpec((1, LANES), lambda i: (0, 0)),
        ],
        out_specs=[
            pl.BlockSpec((1, s, LANES), lambda i: (i, 0, 0)),
            pl.BlockSpec((1, n_heads, s), lambda i: (i, 0, 0)),
        ],
        out_shape=[
            jax.ShapeDtypeStruct((b, s, LANES), F32),
            jax.ShapeDtypeStruct((b, n_heads, s), F32),
        ],
        compiler_params=_params(("arbitrary",)),
        name="forget_cumsum",
    )(f_pre, b_pad)


def _causal_sweeps(streams, tq):
    s_len = streams[0][0].shape[1]
    n_q = s_len // tq

    def logits(stream, qi):
        q_ref, k_ref, _, _, add_bias = stream
        q0, kv = qi * tq, (qi + 1) * tq
        s = lax.dot_general(q_ref[0, q0:kv, :], k_ref[0, :kv, :], NT_DIMS,
                            preferred_element_type=F32)
        tiles = [add_bias(s[:, kj * tq:(kj + 1) * tq], qi, kj) for kj in range(qi + 1)]
        z = jnp.concatenate(tiles, axis=1) if qi else tiles[0]
        return z, jnp.max(z, axis=1, keepdims=True)

    def weigh(stream, qi, z, m):
        _, _, v_aug, o_ref, _ = stream
        q0, kv = qi * tq, (qi + 1) * tq
        p = jnp.exp2(z - m).astype(BF16)
        o = jnp.dot(p, v_aug[0:kv, :], preferred_element_type=F32)
        o_ref[0, q0:kv, :] = (o[:, :HEAD_DIM] / o[:, HEAD_DIM:HEAD_DIM + 1]).astype(o_ref.dtype)

    pending = [logits(stream, 0) for stream in streams]
    for qi in range(n_q):
        for si, stream in enumerate(streams):
            upcoming = logits(stream, qi + 1) if qi + 1 < n_q else None
            weigh(stream, qi, *pending[si])
            pending[si] = upcoming


def _fill_v_aug(v_ref, v_aug):
    v_aug[:, :HEAD_DIM] = v_ref[0]
    lane = lax.broadcasted_iota(jnp.int32, (v_aug.shape[0], v_aug.shape[1] - HEAD_DIM), 1)
    v_aug[:, HEAD_DIM:] = jnp.where(lane == 0, 1.0, 0.0).astype(BF16)


def _cast_specs(weights, n_steps, step_index):
    in_specs, out_specs, out_shapes = [], [], []
    for w in weights:
        rows, cols = w.shape[0] // n_steps, w.shape[1]
        assert rows * n_steps == w.shape[0] and rows % BF16_ROWS == 0
        spec = pl.BlockSpec((rows, cols), lambda *idx: (step_index(*idx), 0))
        in_specs.append(spec)
        out_specs.append(spec)
        out_shapes.append(jax.ShapeDtypeStruct(w.shape, BF16))
    return in_specs, out_specs, out_shapes


def _attention_kernel(slopes_ref, qa_ref, ka_ref, va_ref, qb_ref, kb_ref, vb_ref, ccol_ref, crow_ref,
                      *rest, tq):
    n_cast = (len(rest) - 5) // 2
    oa_ref, ob_ref = rest[n_cast], rest[n_cast + 1]
    bias_scr, va_aug, vb_aug = rest[-3:]
    _fill_v_aug(va_ref, va_aug)
    _fill_v_aug(vb_ref, vb_aug)
    h = pl.program_id(0)
    r = lax.broadcasted_iota(jnp.int32, (tq, tq), 0)
    c = lax.broadcasted_iota(jnp.int32, (tq, tq), 1)

    @pl.when(pl.program_id(1) == 0)
    def _():
        slope = slopes_ref[h]
        for d in range(bias_scr.shape[0]):
            delta = r - c + d * tq
            count = jnp.zeros((tq, tq), F32)
            for window, dil in DIL_PATTERNS:
                assert dil & (dil - 1) == 0
                member = jnp.logical_and((delta & (dil - 1)) == 0, delta <= window)
                count = count + jnp.where(member, 1.0, 0.0)
            valid = jnp.logical_and(delta >= 0, count > 0.0)
            bias = jnp.log(jnp.maximum(count, 1.0)) - slope * delta.astype(F32)
            bias_scr[d] = jnp.where(valid, bias, NEG_INF) * LOG2E

    s_len = qa_ref.shape[1]
    lane = lax.broadcasted_iota(jnp.int32, (s_len, LANES), 1)
    cq = jnp.sum(jnp.where(lane == h, ccol_ref[0], 0.0), axis=1, keepdims=True) * LOG2E
    ck = crow_ref[0, 0] * LOG2E
    causal = c <= r

    def forget_bias(tile, qi, kj):
        z = tile + cq[qi * tq:(qi + 1) * tq] - ck[:, kj * tq:(kj + 1) * tq]
        return jnp.where(causal, z, NEG_INF) if kj == qi else z

    def distance_bias(tile, qi, kj):
        return tile + bias_scr[qi - kj]

    _causal_sweeps([(qa_ref, ka_ref, va_aug, oa_ref, forget_bias),
                    (qb_ref, kb_ref, vb_aug, ob_ref, distance_bias)], tq)

    for i_ref, o_ref in zip(rest[:n_cast], rest[n_cast + 2:-3]):
        o_ref[...] = i_ref[...].astype(BF16)


def _attention(qkv, ccol, crow4, slopes, cast_weights, *, n_heads, tq):
    b, s, _ = qkv.shape
    spec = lambda group: pl.BlockSpec((1, s, HEAD_DIM),
                                      lambda h, bi, sl: (bi, 0, group * n_heads + h))
    out_spec = pl.BlockSpec((1, s, HEAD_DIM), lambda h, bi, sl: (bi, 0, h))
    c_in, c_out, c_shapes = _cast_specs(cast_weights, b * n_heads, lambda h, bi, sl: h * b + bi)
    o_shape = jax.ShapeDtypeStruct((b, s, n_heads * HEAD_DIM), BF16)
    return pl.pallas_call(
        functools.partial(_attention_kernel, tq=tq),
        grid_spec=pltpu.PrefetchScalarGridSpec(
            num_scalar_prefetch=1,
            grid=(n_heads, b),
            in_specs=[spec(g) for g in range(6)] + [
                pl.BlockSpec((1, s, LANES), lambda h, bi, sl: (bi, 0, 0)),
                pl.BlockSpec((1, 1, 1, s), lambda h, bi, sl: (bi, h, 0, 0)),
            ] + c_in,
            out_specs=[out_spec, out_spec] + c_out,
            scratch_shapes=[pltpu.VMEM((s // tq, tq, tq), F32),
                            pltpu.VMEM((s, 2 * HEAD_DIM), BF16),
                            pltpu.VMEM((s, 2 * HEAD_DIM), BF16)],
        ),
        out_shape=[o_shape, o_shape] + c_shapes,
        compiler_params=_params(("arbitrary", "arbitrary")),
        name="attention",
    )(slopes, *([qkv] * 6), ccol, crow4, *cast_weights)


def _mix_kernel(oa_ref, ob_ref, ga_ref, gb_ref, x_ref, wa_ref, wb_ref, wo_ref, g_ref,
                x1_ref, h2_ref):
    ta = jnp.dot(oa_ref[...], wa_ref[...], preferred_element_type=F32)
    tb = jnp.dot(ob_ref[...], wb_ref[...], preferred_element_type=F32)
    merged = (jax.nn.sigmoid(ga_ref[...]) * ta + jax.nn.sigmoid(gb_ref[...]) * tb).astype(BF16)
    x1 = x_ref[...] + jnp.dot(merged, wo_ref[...], preferred_element_type=F32)
    x1_ref[...] = x1
    ms = jnp.mean(x1 * x1, axis=-1, keepdims=True)
    h2_ref[...] = (x1 * lax.rsqrt(ms + EPS) * g_ref[...]).astype(BF16)


def _mix(oa, ob, gates, x2, w_a, w_b, w_o, g_ffn):
    m, d = x2.shape
    wa_rows, wb_rows = w_a.shape[0], w_b.shape[0]
    tm = 256
    resident = lambda shape: pl.BlockSpec(shape, lambda i: (0, 0), pipeline_mode=pl.Buffered(1))
    return pl.pallas_call(
        _mix_kernel,
        grid=(m // tm,),
        in_specs=[
            pl.BlockSpec((tm, wa_rows), lambda i: (i, 0)),
            pl.BlockSpec((tm, wb_rows), lambda i: (i, 0)),
            pl.BlockSpec((tm, d), lambda i: (i, 0)),
            pl.BlockSpec((tm, d), lambda i: (i, 1)),
            pl.BlockSpec((tm, d), lambda i: (i, 0)),
            resident((wa_rows, d)),
            resident((wb_rows, d)),
            resident((d, d)),
            pl.BlockSpec((1, d), lambda i: (0, 0)),
        ],
        out_specs=[
            pl.BlockSpec((tm, d), lambda i: (i, 0)),
            pl.BlockSpec((tm, d), lambda i: (i, 0)),
        ],
        out_shape=[
            jax.ShapeDtypeStruct((m, d), F32),
            jax.ShapeDtypeStruct((m, d), BF16),
        ],
        compiler_params=_params(("arbitrary",)),
        name="branch_mix_out_proj",
    )(oa, ob, gates, gates, x2, w_a, w_b, w_o, g_ffn)


def _ffn_kernel(h_ref, x1_ref, wg_ref, wv_ref, cg_ref, cv_ref, bg_ref, bv_ref, wd_ref,
                out_ref, ug_scr, uv_scr, carry_g, carry_v, *, tm, tiles_per_seq):
    i = pl.program_id(0)
    j = pl.program_id(1)
    first = (i % tiles_per_seq) == 0

    @pl.when(jnp.logical_and(i == 0, j == 0))
    def _():
        carry_g[...] = jnp.zeros(carry_g.shape, F32)
        carry_v[...] = jnp.zeros(carry_v.shape, F32)

    @pl.when(j == 0)
    def _():
        out_ref[...] = x1_ref[...]

    rows = tm // FFN_ROW_PARTS

    def up(w_ref, u_scr, r0):
        u_scr[SUBLANES + r0:SUBLANES + r0 + rows, :] = jnp.dot(
            h_ref[r0:r0 + rows, :], w_ref[...], preferred_element_type=F32)

    def conv(c_ref, b_ref, u_scr, r0):
        out = b_ref[...]
        for t in range(CONV_WIDTH):
            shift = CONV_WIDTH - 1 - t
            out = out + c_ref[t:t + 1, :] * u_scr[pl.ds(SUBLANES - shift + r0, rows), :]
        return out

    for u_scr, carry in ((ug_scr, carry_g), (uv_scr, carry_v)):
        prev = carry[j]
        u_scr[0:SUBLANES, :] = jnp.where(first, jnp.zeros_like(prev), prev)
    for r0 in range(0, tm, rows):
        up(wg_ref, ug_scr, r0)
        up(wv_ref, uv_scr, r0)
    carry_g[j] = ug_scr[tm:tm + SUBLANES, :]
    carry_v[j] = uv_scr[tm:tm + SUBLANES, :]
    for r0 in range(0, tm, rows):
        gate = conv(cg_ref, bg_ref, ug_scr, r0)
        val = conv(cv_ref, bv_ref, uv_scr, r0)
        a = (gate * jax.nn.sigmoid(gate) * val).astype(BF16)
        out_ref[r0:r0 + rows, :] += jnp.dot(a, wd_ref[...], preferred_element_type=F32)


def _ffn(h2, x1, w_up, w_conv, b_conv, w_down, *, seq_len):
    m, d = h2.shape
    d_ff = w_down.shape[0]
    tm, tf = 1024, 512
    nj = d_ff // tf
    kern = functools.partial(_ffn_kernel, tm=tm, tiles_per_seq=seq_len // tm)
    return pl.pallas_call(
        kern,
        grid=(m // tm, nj),
        in_specs=[
            pl.BlockSpec((tm, d), lambda i, j: (i, 0)),
            pl.BlockSpec((tm, d), lambda i, j: (i, 0)),
            pl.BlockSpec((d, tf), lambda i, j: (0, j)),
            pl.BlockSpec((d, tf), lambda i, j: (0, nj + j)),
            pl.BlockSpec((CONV_WIDTH, tf), lambda i, j: (0, j)),
            pl.BlockSpec((CONV_WIDTH, tf), lambda i, j: (0, nj + j)),
            pl.BlockSpec((1, tf), lambda i, j: (0, j)),
            pl.BlockSpec((1, tf), lambda i, j: (0, nj + j)),
            pl.BlockSpec((tf, d), lambda i, j: (j, 0)),
        ],
        out_specs=pl.BlockSpec((tm, d), lambda i, j: (i, 0)),
        out_shape=jax.ShapeDtypeStruct((m, d), F32),
        scratch_shapes=[
            pltpu.VMEM((tm + SUBLANES, tf), F32),
            pltpu.VMEM((tm + SUBLANES, tf), F32),
            pltpu.VMEM((nj, SUBLANES, tf), F32),
            pltpu.VMEM((nj, SUBLANES, tf), F32),
        ],
        compiler_params=_params(("arbitrary", "arbitrary"), 62 * 1024 * 1024),
        name="conv_ffn",
    )(h2, x1, w_up, w_up, w_conv, w_conv, b_conv, b_conv, w_down)


def _layer(x, g_attn, w_in, b_forget, g_q_fox, g_k_fox, g_q_dil, g_k_dil,
           w_br_fox, w_br_dil, w_out, g_ffn, w_up, w_conv, b_conv, w_down):
    b, s, d = x.shape
    w_fox = N_HEADS_FOX * HEAD_DIM
    w_dil = N_HEADS_DIL * HEAD_DIM
    n_qkv = 3 * w_fox + 3 * w_dil
    f_lo, f_hi = 3 * w_fox, 3 * w_fox + N_HEADS_FOX

    ones = jnp.ones((w_fox,), F32)
    g_qk = jnp.concatenate([g_q_fox.reshape(-1) * QK_SCALE2, g_k_fox.reshape(-1), ones,
                            g_q_dil.reshape(-1) * QK_SCALE2, g_k_dil.reshape(-1), ones]
                           ).reshape(1, n_qkv)
    b_pad = jnp.pad(b_forget, (0, LANES - N_HEADS_FOX)).reshape(1, LANES)
    slopes = jnp.asarray(2.0 ** (-8.0 * np.arange(1, N_HEADS_DIL + 1) / N_HEADS_DIL), dtype=F32)

    x2 = x.reshape(b * s, d)
    qkv, gates, f_pre = _in_proj(x2, g_attn.reshape(1, d), w_in.T, g_qk,
                                 f_lo=f_lo, f_hi=f_hi, n_qkv=n_qkv, n_gate=2 * d)
    qkv = qkv.reshape(b, s, n_qkv)

    ccol, crow = _forget(f_pre.reshape(b, s, LANES), b_pad, n_heads=N_HEADS_FOX)
    crow4 = crow.reshape(b, N_HEADS_FOX, 1, s)
    assert N_HEADS_FOX == N_HEADS_DIL
    o_a, o_b, w_up_b, w_down_b, w_out_b, w_br_fox_b, w_br_dil_b = _attention(
        qkv, ccol, crow4, slopes, [w_up, w_down, w_out, w_br_fox, w_br_dil],
        n_heads=N_HEADS_FOX, tq=256)

    x1, h2 = _mix(o_a.reshape(b * s, w_fox), o_b.reshape(b * s, w_dil), gates, x2,
                  w_br_fox_b, w_br_dil_b, w_out_b, g_ffn.reshape(1, d))
    out = _ffn(h2, x1, w_up_b, w_conv, b_conv.reshape(1, -1), w_down_b, seq_len=s)
    return out.reshape(b, s, d)


def kernel(x, g_attn, w_in, b_forget, g_q_fox, g_k_fox, g_q_dil, g_k_dil, w_br_fox, w_br_dil,
           w_out, g_ffn, w_up, w_conv, b_conv, w_down):
    for l in range(w_in.shape[0]):
        x = _layer(x, g_attn[l], w_in[l], b_forget[l], g_q_fox[l], g_k_fox[l], g_q_dil[l],
                   g_k_dil[l], w_br_fox[l], w_br_dil[l], w_out[l], g_ffn[l], w_up[l], w_conv[l],
                   b_conv[l], w_down[l])
    return x
```

```python
import functools

import numpy as np
import jax
import jax.numpy as jnp
from jax import lax
from jax.experimental import pallas as pl
from jax.experimental.pallas import tpu as pltpu

HEAD_DIM = 128
N_HEADS_FOX = 8
N_HEADS_DIL = 8
DIL_PATTERNS = ((128, 1), (512, 4), (2048, 16))
CONV_WIDTH = 3
PROJ_ROW_PARTS = 4
FFN_ROW_PARTS = 2
EPS = 1e-6
NEG_INF = -1e30
LOG2E = float(np.log2(np.e))
QK_SCALE2 =LOG2E / float(np.sqrt(HEAD_DIM))

F32 = jnp.float32
BF16 = jnp.bfloat16

LANES = 128
SUBLANES = 8
BF16_ROWS = 2 * SUBLANES
VMEM_LIMIT = 56 * 1024 * 1024

NT_DIMS = (((1,), (1,)), ((), ()))


def _params(sem, vmem_limit=VMEM_LIMIT):
    return pltpu.CompilerParams(dimension_semantics=sem, vmem_limit_bytes=vmem_limit)


def _in_proj_kernel(x_ref, g_ref, w_ref, wf_ref, gqk_ref, qkv_ref, gate_ref, f_ref, h_scr,
                    *, n_qkv_tiles, tiles_per_group, heads_per_tile):
    s = pl.program_id(1)
    j = s - 1
    half = x_ref.shape[0]

    def project(h, w_t):
        return lax.dot_general(h, w_t.astype(BF16), NT_DIMS, preferred_element_type=F32)

    @pl.when(s <= 1)
    def _():
        rows = pl.ds(pl.multiple_of(s * half, half), half)
        x = x_ref[...]
        ms = jnp.mean(x * x, axis=-1, keepdims=True)
        h = (x * lax.rsqrt(ms + EPS) * g_ref[...]).astype(BF16)
        h_scr[rows, :] = h
        wf = wf_ref[...]
        wf = jnp.concatenate([wf, jnp.zeros((LANES - wf.shape[0], wf.shape[1]), F32)], axis=0)
        f_ref[rows, :] = project(h, wf)

    def store_qk(acc, rows):
        for hh in range(heads_per_tile):
            sl = slice(hh * HEAD_DIM, (hh + 1) * HEAD_DIM)
            a = acc[:, sl]
            ms = jnp.mean(a * a, axis=-1, keepdims=True)
            qkv_ref[rows, sl] = (a * lax.rsqrt(ms + EPS) * gqk_ref[:, sl]).astype(BF16)

    def store_v(acc, rows):
        qkv_ref[rows, :] = acc.astype(BF16)

    def store_gate(acc, rows):
        gate_ref[rows, :] = acc

    def tile(store):
        w = w_ref[...].astype(BF16)
        part = h_scr.shape[0] // PROJ_ROW_PARTS
        for r0 in range(0, h_scr.shape[0], part):
            rows = slice(r0, r0 + part)
            store(lax.dot_general(h_scr[rows, :], w, NT_DIMS, preferred_element_type=F32), rows)

    active = s >= 1
    is_qkv = jnp.logical_and(active, j < n_qkv_tiles)
    is_qk = jnp.logical_and(is_qkv, (j % (3 * tiles_per_group)) < 2 * tiles_per_group)
    cases = (
        (is_qk, store_qk),
        (jnp.logical_and(is_qkv, jnp.logical_not(is_qk)), store_v),
        (jnp.logical_and(active, j >= n_qkv_tiles), store_gate),
    )
    for cond, store in cases:
        pl.when(cond)(functools.partial(tile, store))


def _in_proj(x2, g_attn, w_t, g_qk, *, f_lo, f_hi, n_qkv, n_gate):
    m, d = x2.shape
    tm, tn = 2048, 512
    assert f_lo % tn == 0 and f_hi % SUBLANES == 0
    n_qkv_tiles = n_qkv // tn
    n_tiles = (n_qkv + n_gate) // tn
    kern = functools.partial(
        _in_proj_kernel, n_qkv_tiles=n_qkv_tiles,
        tiles_per_group=(N_HEADS_FOX * HEAD_DIM) // tn, heads_per_tile=tn // HEAD_DIM)
    tile = lambda s: jnp.maximum(s - 1, 0)
    qkv_tile = lambda s: jnp.minimum(tile(s), n_qkv_tiles - 1)
    w_row = lambda j: pl.multiple_of(j * tn + jnp.where(j * tn >= f_lo, f_hi - f_lo, 0), SUBLANES)
    return pl.pallas_call(
        kern,
        grid=(m // tm, n_tiles + 1),
        in_specs=[
            pl.BlockSpec((tm // 2, d), lambda i, s: (2 * i + jnp.minimum(s, 1), 0)),
            pl.BlockSpec((1, d), lambda i, s: (0, 0)),
            pl.BlockSpec((pl.Element(tn), pl.Element(d)), lambda i, s: (w_row(tile(s)), 0)),
            pl.BlockSpec((pl.Element(f_hi - f_lo), pl.Element(d)), lambda i, s: (f_lo, 0)),
            pl.BlockSpec((1, tn), lambda i, s: (0, qkv_tile(s))),
        ],
        out_specs=[
            pl.BlockSpec((tm, tn), lambda i, s: (i, qkv_tile(s))),
            pl.BlockSpec((tm, tn), lambda i, s: (i, jnp.maximum(tile(s) - n_qkv_tiles, 0))),
            pl.BlockSpec((tm, LANES), lambda i, s: (i, 0)),
        ],
        out_shape=[
            jax.ShapeDtypeStruct((m, n_qkv), BF16),
            jax.ShapeDtypeStruct((m, n_gate), F32),
            jax.ShapeDtypeStruct((m, LANES), F32),
        ],
        scratch_shapes=[pltpu.VMEM((tm, d), BF16)],
        compiler_params=_params(("arbitrary", "arbitrary")),
        name="in_proj",
    )(x2, g_attn, w_t, w_t, g_qk)


def _forget_kernel(f_ref, b_ref, ccol_ref, crow_ref, *, n_heads):
    z = f_ref[0] + b_ref[...]
    c = jax.nn.log_sigmoid(z)
    s = c.shape[0]
    row = lax.broadcasted_iota(jnp.int32, c.shape, 0)
    k = 1
    while k < s:
        c = c + jnp.where(row >= k, pltpu.roll(c, k, axis=0), 0.0)
        k *= 2
    ccol_ref[0] = c
    crow_ref[0] = c.T[:n_heads, :]


def _forget(f_pre, b_pad, *, n_heads):
    b, s, _ = f_pre.shape
    return pl.pallas_call(
        functools.partial(_forget_kernel, n_heads=n_heads),
        grid=(b,),
        in_specs=[
            pl.BlockSpec((1, s, LANES), lambda i: (i, 0, 0)),
            pl.BlockSpec((1, LANES), lambda i: (0, 0)),
        ],
        out_specs=[
            pl.BlockSpec((1, s, LANES), lambda i: (i, 0, 0)),
            pl.BlockSpec((1, n_heads, s), lambda i: (i, 0, 0)),
        ],
        out_shape=[
            jax.ShapeDtypeStruct((b, s, LANES), F32),
            jax.ShapeDtypeStruct((b, n_heads, s), F32),
        ],
        compiler_params=_params(("arbitrary",)),
        name="forget_cumsum",
    )(f_pre, b_pad)


def _causal_sweeps(streams, tq):
    s_len = streams[0][0].shape[1]
    n_q = s_len // tq

    def logits(stream, qi):
        q_ref, k_ref, _, _, add_bias = stream
        q0, kv = qi * tq, (qi + 1) * tq
        s = lax.dot_general(q_ref[0, q0:kv, :], k_ref[0, :kv, :], NT_DIMS,
                            preferred_element_type=F32)
        tiles = [add_bias(s[:, kj * tq:(kj + 1) * tq], qi, kj) for kj in range(qi + 1)]
        z = jnp.concatenate(tiles, axis=1) if qi else tiles[0]
        return z, jnp.max(z, axis=1, keepdims=True)

    def weigh(stream, qi, z, m):
        _, _, v_aug, o_ref, _ = stream
        q0, kv = qi * tq, (qi + 1) * tq
        p = jnp.exp2(z - m).astype(BF16)
        o = jnp.dot(p, v_aug[0:kv, :], preferred_element_type=F32)
        o_ref[0, q0:kv, :] = (o[:, :HEAD_DIM] / o[:, HEAD_DIM:HEAD_DIM + 1]).astype(o_ref.dtype)

    pending = [logits(stream, 0) for stream in streams]
    for qi in range(n_q):
        for si, stream in enumerate(streams):
            upcoming = logits(stream, qi + 1) if qi + 1 < n_q else None
            weigh(stream, qi, *pending[si])
            pending[si] = upcoming


def _fill_v_aug(v_ref, v_aug):
    v_aug[:, :HEAD_DIM] = v_ref[0]
    lane = lax.broadcasted_iota(jnp.int32, (v_aug.shape[0], v_aug.shape[1] - HEAD_DIM), 1)
    v_aug[:, HEAD_DIM:] = jnp.where(lane == 0, 1.0, 0.0).astype(BF16)


def _cast_specs(weights, n_steps, step_index):
    in_specs, out_specs, out_shapes = [], [], []
    for w in weights:
        rows, cols = w.shape[0] // n_steps, w.shape[1]
        assert rows * n_steps == w.shape[0] and rows % BF16_ROWS == 0
        spec = pl.BlockSpec((rows, cols), lambda *idx: (step_index(*idx), 0))
        in_specs.append(spec)
        out_specs.append(spec)
        out_shapes.append(jax.ShapeDtypeStruct(w.shape, BF16))
    return in_specs, out_specs, out_shapes


def _attention_kernel(slopes_ref, qa_ref, ka_ref, va_ref, qb_ref, kb_ref, vb_ref, ccol_ref, crow_ref,
                      *rest, tq):
    n_cast = (len(rest) - 5) // 2
    oa_ref, ob_ref = rest[n_cast], rest[n_cast + 1]
    bias_scr, va_aug, vb_aug = rest[-3:]
    _fill_v_aug(va_ref, va_aug)
    _fill_v_aug(vb_ref, vb_aug)
    h = pl.program_id(0)
    r = lax.broadcasted_iota(jnp.int32, (tq, tq), 0)
    c = lax.broadcasted_iota(jnp.int32, (tq, tq), 1)

    @pl.when(pl.program_id(1) == 0)
    def _():
        slope = slopes_ref[h]
        for d in range(bias_scr.shape[0]):
            delta = r - c + d * tq
            count = jnp.zeros((tq, tq), F32)
            for window, dil in DIL_PATTERNS:
                assert dil & (dil - 1) == 0
                member = jnp.logical_and((delta & (dil - 1)) == 0, delta <= window)
                count = count + jnp.where(member, 1.0, 0.0)
            valid = jnp.logical_and(delta >= 0, count > 0.0)
            bias = jnp.log(jnp.maximum(count, 1.0)) - slope * delta.astype(F32)
            bias_scr[d] = jnp.where(valid, bias, NEG_INF) * LOG2E

    s_len = qa_ref.shape[1]
    lane = lax.broadcasted_iota(jnp.int32, (s_len, LANES), 1)
    cq = jnp.sum(jnp.where(lane == h, ccol_ref[0], 0.0), axis=1, keepdims=True) * LOG2E
    ck = crow_ref[0, 0] * LOG2E
    causal = c <= r

    def forget_bias(tile, qi, kj):
        z = tile + cq[qi * tq:(qi + 1) * tq] - ck[:, kj * tq:(kj + 1) * tq]
        return jnp.where(causal, z, NEG_INF) if kj == qi else z

    def distance_bias(tile, qi, kj):
        return tile + bias_scr[qi - kj]

    _causal_sweeps([(qa_ref, ka_ref, va_aug, oa_ref, forget_bias),
                    (qb_ref, kb_ref, vb_aug, ob_ref, distance_bias)], tq)

    for i_ref, o_ref in zip(rest[:n_cast], rest[n_cast + 2:-3]):
        o_ref[...] = i_ref[...].astype(BF16)


def _attention(qkv, ccol, crow4, slopes, cast_weights, *, n_heads, tq):
    b, s, _ = qkv.shape
    spec = lambda group: pl.BlockSpec((1, s, HEAD_DIM),
                                      lambda h, bi, sl: (bi, 0, group * n_heads + h))
    out_spec = pl.BlockSpec((1, s, HEAD_DIM), lambda h, bi, sl: (bi, 0, h))
    c_in, c_out, c_shapes = _cast_specs(cast_weights, b * n_heads, lambda h, bi, sl: h * b + bi)
    o_shape = jax.ShapeDtypeStruct((b, s, n_heads * HEAD_DIM), BF16)
    return pl.pallas_call(
        functools.partial(_attention_kernel, tq=tq),
        grid_spec=pltpu.PrefetchScalarGridSpec(
            num_scalar_prefetch=1,
            grid=(n_heads, b),
            in_specs=[spec(g) for g in range(6)] + [
                pl.BlockSpec((1, s, LANES), lambda h, bi, sl: (bi, 0, 0)),
                pl.BlockSpec((1, 1, 1, s), lambda h, bi, sl: (bi, h, 0, 0)),
            ] + c_in,
            out_specs=[out_spec, out_spec] + c_out,
            scratch_shapes=[pltpu.VMEM((s // tq, tq, tq), F32),
                            pltpu.VMEM((s, 2 * HEAD_DIM), BF16),
                            pltpu.VMEM((s, 2 * HEAD_DIM), BF16)],
        ),
        out_shape=[o_shape, o_shape] + c_shapes,
        compiler_params=_params(("arbitrary", "arbitrary")),
        name="attention",
    )(slopes, *([qkv] * 6), ccol, crow4, *cast_weights)


def _mix_kernel(oa_ref, ob_ref, ga_ref, gb_ref, x_ref, wa_ref, wb_ref, wo_ref, g_ref,
                x1_ref, h2_ref):
    ta = jnp.dot(oa_ref[...], wa_ref[...], preferred_element_type=F32)
    tb = jnp.dot(ob_ref[...], wb_ref[...], preferred_element_type=F32)
    merged = (jax.nn.sigmoid(ga_ref[...]) * ta + jax.nn.sigmoid(gb_ref[...]) * tb).astype(BF16)
    x1 = x_ref[...] + jnp.dot(merged, wo_ref[...], preferred_element_type=F32)
    x1_ref[...] = x1
    ms = jnp.mean(x1 * x1, axis=-1, keepdims=True)
    h2_ref[...] = (x1 * lax.rsqrt(ms + EPS) * g_ref[...]).astype(BF16)


def _mix(oa, ob, gates, x2, w_a, w_b, w_o, g_ffn):
    m, d = x2.shape
    wa_rows, wb_rows = w_a.shape[0], w_b.shape[0]
    tm = 256
    resident = lambda shape: pl.BlockSpec(shape, lambda i: (0, 0), pipeline_mode=pl.Buffered(1))
    return pl.pallas_call(
        _mix_kernel,
        grid=(m // tm,),
        in_specs=[
            pl.BlockSpec((tm, wa_rows), lambda i: (i, 0)),
            pl.BlockSpec((tm, wb_rows), lambda i: (i, 0)),
            pl.BlockSpec((tm, d), lambda i: (i, 0)),
            pl.BlockSpec((tm, d), lambda i: (i, 1)),
            pl.BlockSpec((tm, d), lambda i: (i, 0)),
            resident((wa_rows, d)),
            resident((wb_rows, d)),
            resident((d, d)),
            pl.BlockSpec((1, d), lambda i: (0, 0)),
        ],
        out_specs=[
            pl.BlockSpec((tm, d), lambda i: (i, 0)),
            pl.BlockSpec((tm, d), lambda i: (i, 0)),
        ],
        out_shape=[
            jax.ShapeDtypeStruct((m, d), F32),
            jax.ShapeDtypeStruct((m, d), BF16),
        ],
        compiler_params=_params(("arbitrary",)),
        name="branch_mix_out_proj",
    )(oa, ob, gates, gates, x2, w_a, w_b, w_o, g_ffn)


def _ffn_kernel(h_ref, x1_ref, wg_ref, wv_ref, cg_ref, cv_ref, bg_ref, bv_ref, wd_ref,
                out_ref, ug_scr, uv_scr, carry_g, carry_v, *, tm, tiles_per_seq):
    i = pl.program_id(0)
    j = pl.program_id(1)
    first = (i % tiles_per_seq) == 0

    @pl.when(jnp.logical_and(i == 0, j == 0))
    def _():
        carry_g[...] = jnp.zeros(carry_g.shape, F32)
        carry_v[...] = jnp.zeros(carry_v.shape, F32)

    @pl.when(j == 0)
    def _():
        out_ref[...] = x1_ref[...]

    rows = tm // FFN_ROW_PARTS

    def up(w_ref, u_scr, r0):
        u_scr[SUBLANES + r0:SUBLANES + r0 + rows, :] = jnp.dot(
            h_ref[r0:r0 + rows, :], w_ref[...], preferred_element_type=F32)

    def conv(c_ref, b_ref, u_scr, r0):
        out = b_ref[...]
        for t in range(CONV_WIDTH):
            shift = CONV_WIDTH - 1 - t
            out = out + c_ref[t:t + 1, :] * u_scr[pl.ds(SUBLANES - shift + r0, rows), :]
        return out

    for u_scr, carry in ((ug_scr, carry_g), (uv_scr, carry_v)):
        prev = carry[j]
        u_scr[0:SUBLANES, :] = jnp.where(first, jnp.zeros_like(prev), prev)
    for r0 in range(0, tm, rows):
        up(wg_ref, ug_scr, r0)
        up(wv_ref, uv_scr, r0)
    carry_g[j] = ug_scr[tm:tm + SUBLANES, :]
    carry_v[j] = uv_scr[tm:tm + SUBLANES, :]
    for r0 in range(0, tm, rows):
        gate = conv(cg_ref, bg_ref, ug_scr, r0)
        val = conv(cv_ref, bv_ref, uv_scr, r0)
        a = (gate * jax.nn.sigmoid(gate) * val).astype(BF16)
        out_ref[r0:r0 + rows, :] += jnp.dot(a, wd_ref[...], preferred_element_type=F32)


def _ffn(h2, x1, w_up, w_conv, b_conv, w_down, *, seq_len):
    m, d = h2.shape
    d_ff = w_down.shape[0]
    tm, tf = 1024, 512
    nj = d_ff // tf
    kern = functools.partial(_ffn_kernel, tm=tm, tiles_per_seq=seq_len // tm)
    return pl.pallas_call(
        kern,
        grid=(m // tm, nj),
        in_specs=[
            pl.BlockSpec((tm, d), lambda i, j: (i, 0)),
            pl.BlockSpec((tm, d), lambda i, j: (i, 0)),
            pl.BlockSpec((d, tf), lambda i, j: (0, j)),
            pl.BlockSpec((d, tf), lambda i, j: (0, nj + j)),
            pl.BlockSpec((CONV_WIDTH, tf), lambda i, j: (0, j)),
            pl.BlockSpec((CONV_WIDTH, tf), lambda i, j: (0, nj + j)),
            pl.BlockSpec((1, tf), lambda i, j: (0, j)),
            pl.BlockSpec((1, tf), lambda i, j: (0, nj + j)),
            pl.BlockSpec((tf, d), lambda i, j: (j, 0)),
        ],
        out_specs=pl.BlockSpec((tm, d), lambda i, j: (i, 0)),
        out_shape=jax.ShapeDtypeStruct((m, d), F32),
        scratch_shapes=[
            pltpu.VMEM((tm + SUBLANES, tf), F32),
            pltpu.VMEM((tm + SUBLANES, tf), F32),
            pltpu.VMEM((nj, SUBLANES, tf), F32),
            pltpu.VMEM((nj, SUBLANES, tf), F32),
        ],
        compiler_params=_params(("arbitrary", "arbitrary"), 62 * 1024 * 1024),
        name="conv_ffn",
    )(h2, x1, w_up, w_up, w_conv, w_conv, b_conv, b_conv, w_down)


def _layer(x, g_attn, w_in, b_forget, g_q_fox, g_k_fox, g_q_dil, g_k_dil,
           w_br_fox, w_br_dil, w_out, g_ffn, w_up, w_conv, b_conv, w_down):
    b, s, d = x.shape
    w_fox = N_HEADS_FOX * HEAD_DIM
    w_dil = N_HEADS_DIL * HEAD_DIM
    n_qkv = 3 * w_fox + 3 * w_dil
    f_lo, f_hi = 3 * w_fox, 3 * w_fox + N_HEADS_FOX

    ones = jnp.ones((w_fox,), F32)
    g_qk = jnp.concatenate([g_q_fox.reshape(-1) * QK_SCALE2, g_k_fox.reshape(-1), ones,
                            g_q_dil.reshape(-1) * QK_SCALE2, g_k_dil.reshape(-1), ones]
                           ).reshape(1, n_qkv)
    b_pad = jnp.pad(b_forget, (0, LANES - N_HEADS_FOX)).reshape(1, LANES)
    slopes = jnp.asarray(2.0 ** (-8.0 * np.arange(1, N_HEADS_DIL + 1) / N_HEADS_DIL), dtype=F32)

    x2 = x.reshape(b * s, d)
    qkv, gates, f_pre = _in_proj(x2, g_attn.reshape(1, d), w_in.T, g_qk,
                                 f_lo=f_lo, f_hi=f_hi, n_qkv=n_qkv, n_gate=2 * d)
    qkv = qkv.reshape(b, s, n_qkv)

    ccol, crow = _forget(f_pre.reshape(b, s, LANES), b_pad, n_heads=N_HEADS_FOX)
    crow4 = crow.reshape(b, N_HEADS_FOX, 1, s)
    assert N_HEADS_FOX == N_HEADS_DIL
    o_a, o_b, w_up_b, w_down_b, w_out_b, w_br_fox_b, w_br_dil_b = _attention(
        qkv, ccol, crow4, slopes, [w_up, w_down, w_out, w_br_fox, w_br_dil],
        n_heads=N_HEADS_FOX, tq=128)

    x1, h2 = _mix(o_a.reshape(b * s, w_fox), o_b.reshape(b * s, w_dil), gates, x2,
                  w_br_fox_b, w_br_dil_b, w_out_b, g_ffn.reshape(1, d))
    out = _ffn(h2, x1, w_up_b, w_conv, b_conv.reshape(1, -1), w_down_b, seq_len=s)
    return out.reshape(b, s, d)


def kernel(x, g_attn, w_in, b_forget, g_q_fox, g_k_fox, g_q_dil, g_k_dil, w_br_fox, w_br_dil,
           w_out, g_ffn, w_up, w_conv, b_conv, w_down):
    for l in range(w_in.shape[0]):
        x = _layer(x, g_attn[l], w_in[l], b_forget[l], g_q_fox[l], g_k_fox[l], g_q_dil[l],
                   g_k_dil[l], w_br_fox[l], w_br_dil[l], w_out[l], g_ffn[l], w_up[l], w_conv[l],
                   b_conv[l], w_down[l])
    return x
```

```python
import functools

import numpy as np
import jax
import jax.numpy as jnp
from jax import lax
from jax.experimental import pallas as pl
from jax.experimental.pallas import tpu as pltpu

HEAD_DIM = 128
N_HEADS_FOX = 8
N_HEADS_DIL = 8
DIL_PATTERNS = ((128, 1), (512, 4), (2048, 16))
CONV_WIDTH = 3
PROJ_ROW_PARTS = 4
FFN_ROW_PARTS = 2
EPS = 1e-6
NEG_INF = -1e30
LOG2E = float(np.log2(np.e))
QK_SCALE2 =LOG2E / float(np.sqrt(HEAD_DIM))

F32 = jnp.float32
BF16 = jnp.bfloat16

LANES = 128
SUBLANES = 8
BF16_ROWS = 2 * SUBLANES
VMEM_LIMIT = 56 * 1024 * 1024

NT_DIMS = (((1,), (1,)), ((), ()))


def _params(sem, vmem_limit=VMEM_LIMIT):
    return pltpu.CompilerParams(dimension_semantics=sem, vmem_limit_bytes=vmem_limit)


def _in_proj_kernel(x_ref, g_ref, w_ref, wf_ref, gqk_ref, qkv_ref, gate_ref, f_ref, h_scr,
                    *, n_qkv_tiles, tiles_per_group, heads_per_tile):
    s = pl.program_id(1)
    j = s - 1
    half = x_ref.shape[0]

    def project(h, w_t):
        return lax.dot_general(h, w_t.astype(BF16), NT_DIMS, preferred_element_type=F32)

    @pl.when(s <= 1)
    def _():
        rows = pl.ds(pl.multiple_of(s * half, half), half)
        x = x_ref[...]
        ms = jnp.mean(x * x, axis=-1, keepdims=True)
        h = (x * lax.rsqrt(ms + EPS) * g_ref[...]).astype(BF16)
        h_scr[rows, :] = h
        wf = wf_ref[...]
        wf = jnp.concatenate([wf, jnp.zeros((LANES - wf.shape[0], wf.shape[1]), F32)], axis=0)
        f_ref[rows, :] = project(h, wf)

    def store_qk(acc, rows):
        for hh in range(heads_per_tile):
            sl = slice(hh * HEAD_DIM, (hh + 1) * HEAD_DIM)
            a = acc[:, sl]
            ms = jnp.mean(a * a, axis=-1, keepdims=True)
            qkv_ref[rows, sl] = (a * lax.rsqrt(ms + EPS) * gqk_ref[:, sl]).astype(BF16)

    def store_v(acc, rows):
        qkv_ref[rows, :] = acc.astype(BF16)

    def store_gate(acc, rows):
        gate_ref[rows, :] = acc

    def tile(store):
        w = w_ref[...].astype(BF16)
        part = h_scr.shape[0] // PROJ_ROW_PARTS
        for r0 in range(0, h_scr.shape[0], part):
            rows = slice(r0, r0 + part)
            store(lax.dot_general(h_scr[rows, :], w, NT_DIMS, preferred_element_type=F32), rows)

    active = s >= 1
    is_qkv = jnp.logical_and(active, j < n_qkv_tiles)
    is_qk = jnp.logical_and(is_qkv, (j % (3 * tiles_per_group)) < 2 * tiles_per_group)
    cases = (
        (is_qk, store_qk),
        (jnp.logical_and(is_qkv, jnp.logical_not(is_qk)), store_v),
        (jnp.logical_and(active, j >= n_qkv_tiles), store_gate),
    )
    for cond, store in cases:
        pl.when(cond)(functools.partial(tile, store))


def _in_proj(x2, g_attn, w_t, g_qk, *, f_lo, f_hi, n_qkv, n_gate):
    m, d = x2.shape
    tm, tn = 2048, 512
    assert f_lo % tn == 0 and f_hi % SUBLANES == 0
    n_qkv_tiles = n_qkv // tn
    n_tiles = (n_qkv + n_gate) // tn
    kern = functools.partial(
        _in_proj_kernel, n_qkv_tiles=n_qkv_tiles,
        tiles_per_group=(N_HEADS_FOX * HEAD_DIM) // tn, heads_per_tile=tn // HEAD_DIM)
    tile = lambda s: jnp.maximum(s - 1, 0)
    qkv_tile = lambda s: jnp.minimum(tile(s), n_qkv_tiles - 1)
    w_row = lambda j: pl.multiple_of(j * tn + jnp.where(j * tn >= f_lo, f_hi - f_lo, 0), SUBLANES)
    return pl.pallas_call(
        kern,
        grid=(m // tm, n_tiles + 1),
        in_specs=[
            pl.BlockSpec((tm // 2, d), lambda i, s: (2 * i + jnp.minimum(s, 1), 0)),
            pl.BlockSpec((1, d), lambda i, s: (0, 0)),
            pl.BlockSpec((pl.Element(tn), pl.Element(d)), lambda i, s: (w_row(tile(s)), 0)),
            pl.BlockSpec((pl.Element(f_hi - f_lo), pl.Element(d)), lambda i, s: (f_lo, 0)),
            pl.BlockSpec((1, tn), lambda i, s: (0, qkv_tile(s))),
        ],
        out_specs=[
            pl.BlockSpec((tm, tn), lambda i, s: (i, qkv_tile(s))),
            pl.BlockSpec((tm, tn), lambda i, s: (i, jnp.maximum(tile(s) - n_qkv_tiles, 0))),
            pl.BlockSpec((tm, LANES), lambda i, s: (i, 0)),
        ],
        out_shape=[
            jax.ShapeDtypeStruct((m, n_qkv), BF16),
            jax.ShapeDtypeStruct((m, n_gate), F32),
            jax.ShapeDtypeStruct((m, LANES), F32),
        ],
        scratch_shapes=[pltpu.VMEM((tm, d), BF16)],
        compiler_params=_params(("arbitrary", "arbitrary")),
        name="in_proj",
    )(x2, g_attn, w_t, w_t, g_qk)


def _forget_kernel(f_ref, b_ref, ccol_ref, crow_ref, *, n_heads):
    z = f_ref[0] + b_ref[...]
    c = jax.nn.log_sigmoid(z)
    s = c.shape[0]
    row = lax.broadcasted_iota(jnp.int32, c.shape, 0)
    k = 1
    while k < s:
        c = c + jnp.where(row >= k, pltpu.roll(c, k, axis=0), 0.0)
        k *= 2
    ccol_ref[0] = c
    crow_ref[0] = c.T[:n_heads, :]


def _forget(f_pre, b_pad, *, n_heads):
    b, s, _ = f_pre.shape
    return pl.pallas_call(
        functools.partial(_forget_kernel, n_heads=n_heads),
        grid=(b,),
        in_specs=[
            pl.BlockSpec((1, s, LANES), lambda i: (i, 0, 0)),
            pl.BlockSpec((1, LANES), lambda i: (0, 0)),
        ],
        out_specs=[
            pl.BlockSpec((1, s, LANES), lambda i: (i, 0, 0)),
            pl.BlockSpec((1, n_heads, s), lambda i: (i, 0, 0)),
        ],
        out_shape=[
            jax.ShapeDtypeStruct((b, s, LANES), F32),
            jax.ShapeDtypeStruct((b, n_heads, s), F32),
        ],
        compiler_params=_params(("arbitrary",)),
        name="forget_cumsum",
    )(f_pre, b_pad)


def _causal_sweeps(streams, tq):
    s_len = streams[0][0].shape[1]
    n_q = s_len // tq

    def scores(stream, qi):
        q_ref, k_ref = stream[:2]
        q0, kv = qi * tq, (qi + 1) * tq
        return lax.dot_general(q_ref[0, q0:kv, :], k_ref[0, :kv, :], NT_DIMS,
                               preferred_element_type=F32)

    def biased(stream, qi, s):
        add_bias = stream[4]
        tiles = [add_bias(s[:, kj * tq:(kj + 1) * tq], qi, kj) for kj in range(qi + 1)]
        z = jnp.concatenate(tiles, axis=1) if qi else tiles[0]
        return z, jnp.max(z, axis=1, keepdims=True)

    def weights(z, m):
        return jnp.exp2(z - m).astype(BF16)

    def weighted(stream, qi, p):
        v_aug, o_ref = stream[2:4]
        q0, kv = qi * tq, (qi + 1) * tq
        o = jnp.dot(p, v_aug[0:kv, :], preferred_element_type=F32)
        o_ref[0, q0:kv, :] = (o[:, :HEAD_DIM] / o[:, HEAD_DIM:HEAD_DIM + 1]).astype(o_ref.dtype)

    logit = [biased(st, 0, scores(st, 0)) for st in streams]
    for qi in range(n_q):
        last = qi + 1 == n_q
        raw = [None if last else scores(st, qi + 1) for st in streams]
        probs = [weights(*zm) for zm in logit]
        for st, p in zip(streams, probs):
            weighted(st, qi, p)
        logit = [None if last else biased(st, qi + 1, s) for st, s in zip(streams, raw)]


def _fill_v_aug(v_ref, v_aug):
    v_aug[:, :HEAD_DIM] = v_ref[0]
    lane = lax.broadcasted_iota(jnp.int32, (v_aug.shape[0], v_aug.shape[1] - HEAD_DIM), 1)
    v_aug[:, HEAD_DIM:] = jnp.where(lane == 0, 1.0, 0.0).astype(BF16)


def _cast_specs(weights, n_steps, step_index):
    in_specs, out_specs, out_shapes = [], [], []
    for w in weights:
        rows, cols = w.shape[0] // n_steps, w.shape[1]
        assert rows * n_steps == w.shape[0] and rows % BF16_ROWS == 0
        spec = pl.BlockSpec((rows, cols), lambda *idx: (step_index(*idx), 0))
        in_specs.append(spec)
        out_specs.append(spec)
        out_shapes.append(jax.ShapeDtypeStruct(w.shape, BF16))
    return in_specs, out_specs, out_shapes


def _attention_kernel(slopes_ref, qa_ref, ka_ref, va_ref, qb_ref, kb_ref, vb_ref, ccol_ref, crow_ref,
                      *rest, tq):
    n_cast = (len(rest) - 5) // 2
    oa_ref, ob_ref = rest[n_cast], rest[n_cast + 1]
    bias_scr, va_aug, vb_aug = rest[-3:]
    _fill_v_aug(va_ref, va_aug)
    _fill_v_aug(vb_ref, vb_aug)
    h = pl.program_id(0)
    r = lax.broadcasted_iota(jnp.int32, (tq, tq), 0)
    c = lax.broadcasted_iota(jnp.int32, (tq, tq), 1)

    @pl.when(pl.program_id(1) == 0)
    def _():
        slope = slopes_ref[h]
        for d in range(bias_scr.shape[0]):
            delta = r - c + d * tq
            count = jnp.zeros((tq, tq), F32)
            for window, dil in DIL_PATTERNS:
                assert dil & (dil - 1) == 0
                member = jnp.logical_and((delta & (dil - 1)) == 0, delta <= window)
                count = count + jnp.where(member, 1.0, 0.0)
            valid = jnp.logical_and(delta >= 0, count > 0.0)
            bias = jnp.log(jnp.maximum(count, 1.0)) - slope * delta.astype(F32)
            bias_scr[d] = jnp.where(valid, bias, NEG_INF) * LOG2E

    s_len = qa_ref.shape[1]
    lane = lax.broadcasted_iota(jnp.int32, (s_len, LANES), 1)
    cq = jnp.sum(jnp.where(lane == h, ccol_ref[0], 0.0), axis=1, keepdims=True) * LOG2E
    ck = crow_ref[0, 0] * LOG2E
    causal = c <= r

    def forget_bias(tile, qi, kj):
        z = tile + cq[qi * tq:(qi + 1) * tq] - ck[:, kj * tq:(kj + 1) * tq]
        return jnp.where(causal, z, NEG_INF) if kj == qi else z

    def distance_bias(tile, qi, kj):
        return tile + bias_scr[qi - kj]

    _causal_sweeps([(qa_ref, ka_ref, va_aug, oa_ref, forget_bias),
                    (qb_ref, kb_ref, vb_aug, ob_ref, distance_bias)], tq)

    for i_ref, o_ref in zip(rest[:n_cast], rest[n_cast + 2:-3]):
        o_ref[...] = i_ref[...].astype(BF16)


def _attention(qkv, ccol, crow4, slopes, cast_weights, *, n_heads, tq):
    b, s, _ = qkv.shape
    spec = lambda group: pl.BlockSpec((1, s, HEAD_DIM),
                                      lambda h, bi, sl: (bi, 0, group * n_heads + h))
    out_spec = pl.BlockSpec((1, s, HEAD_DIM), lambda h, bi, sl: (bi, 0, h))
    c_in, c_out, c_shapes = _cast_specs(cast_weights, b * n_heads, lambda h, bi, sl: h * b + bi)
    o_shape = jax.ShapeDtypeStruct((b, s, n_heads * HEAD_DIM), BF16)
    return pl.pallas_call(
        functools.partial(_attention_kernel, tq=tq),
        grid_spec=pltpu.PrefetchScalarGridSpec(
            num_scalar_prefetch=1,
            grid=(n_heads, b),
            in_specs=[spec(g) for g in range(6)] + [
                pl.BlockSpec((1, s, LANES), lambda h, bi, sl: (bi, 0, 0)),
                pl.BlockSpec((1, 1, 1, s), lambda h, bi, sl: (bi, h, 0, 0)),
            ] + c_in,
            out_specs=[out_spec, out_spec] + c_out,
            scratch_shapes=[pltpu.VMEM((s // tq, tq, tq), F32),
                            pltpu.VMEM((s, 2 * HEAD_DIM), BF16),
                            pltpu.VMEM((s, 2 * HEAD_DIM), BF16)],
        ),
        out_shape=[o_shape, o_shape] + c_shapes,
        compiler_params=_params(("arbitrary", "arbitrary")),
        name="attention",
    )(slopes, *([qkv] * 6), ccol, crow4, *cast_weights)


def _mix_kernel(oa_ref, ob_ref, ga_ref, gb_ref, x_ref, wa_ref, wb_ref, wo_ref, g_ref,
                x1_ref, h2_ref):
    ta = jnp.dot(oa_ref[...], wa_ref[...], preferred_element_type=F32)
    tb = jnp.dot(ob_ref[...], wb_ref[...], preferred_element_type=F32)
    merged = (jax.nn.sigmoid(ga_ref[...]) * ta + jax.nn.sigmoid(gb_ref[...]) * tb).astype(BF16)
    x1 = x_ref[...] + jnp.dot(merged, wo_ref[...], preferred_element_type=F32)
    x1_ref[...] = x1
    ms = jnp.mean(x1 * x1, axis=-1, keepdims=True)
    h2_ref[...] = (x1 * lax.rsqrt(ms + EPS) * g_ref[...]).astype(BF16)


def _mix(oa, ob, gates, x2, w_a, w_b, w_o, g_ffn):
    m, d = x2.shape
    wa_rows, wb_rows = w_a.shape[0], w_b.shape[0]
    tm = 256
    resident = lambda shape: pl.BlockSpec(shape, lambda i: (0, 0), pipeline_mode=pl.Buffered(1))
    return pl.pallas_call(
        _mix_kernel,
        grid=(m // tm,),
        in_specs=[
            pl.BlockSpec((tm, wa_rows), lambda i: (i, 0)),
            pl.BlockSpec((tm, wb_rows), lambda i: (i, 0)),
            pl.BlockSpec((tm, d), lambda i: (i, 0)),
            pl.BlockSpec((tm, d), lambda i: (i, 1)),
            pl.BlockSpec((tm, d), lambda i: (i, 0)),
            resident((wa_rows, d)),
            resident((wb_rows, d)),
            resident((d, d)),
            pl.BlockSpec((1, d), lambda i: (0, 0)),
        ],
        out_specs=[
            pl.BlockSpec((tm, d), lambda i: (i, 0)),
            pl.BlockSpec((tm, d), lambda i: (i, 0)),
        ],
        out_shape=[
            jax.ShapeDtypeStruct((m, d), F32),
            jax.ShapeDtypeStruct((m, d), BF16),
        ],
        compiler_params=_params(("arbitrary",)),
        name="branch_mix_out_proj",
    )(oa, ob, gates, gates, x2, w_a, w_b, w_o, g_ffn)


def _ffn_kernel(h_ref, x1_ref, wg_ref, wv_ref, cg_ref, cv_ref, bg_ref, bv_ref, wd_ref,
                out_ref, ug_scr, uv_scr, carry_g, carry_v, *, tm, tiles_per_seq):
    i = pl.program_id(0)
    j = pl.program_id(1)
    first = (i % tiles_per_seq) == 0

    @pl.when(jnp.logical_and(i == 0, j == 0))
    def _():
        carry_g[...] = jnp.zeros(carry_g.shape, F32)
        carry_v[...] = jnp.zeros(carry_v.shape, F32)

    @pl.when(j == 0)
    def _():
        out_ref[...] = x1_ref[...]

    rows = tm // FFN_ROW_PARTS

    def up(w_ref, u_scr, r0):
        u_scr[SUBLANES + r0:SUBLANES + r0 + rows, :] = jnp.dot(
            h_ref[r0:r0 + rows, :], w_ref[...], preferred_element_type=F32)

    def conv(c_ref, b_ref, u_scr, r0):
        out = b_ref[...]
        for t in range(CONV_WIDTH):
            shift = CONV_WIDTH - 1 - t
            out = out + c_ref[t:t + 1, :] * u_scr[pl.ds(SUBLANES - shift + r0, rows), :]
        return out

    for u_scr, carry in ((ug_scr, carry_g), (uv_scr, carry_v)):
        prev = carry[j]
        u_scr[0:SUBLANES, :] = jnp.where(first, jnp.zeros_like(prev), prev)
    for r0 in range(0, tm, rows):
        up(wg_ref, ug_scr, r0)
        up(wv_ref, uv_scr, r0)
    carry_g[j] = ug_scr[tm:tm + SUBLANES, :]
    carry_v[j] = uv_scr[tm:tm + SUBLANES, :]
    for r0 in range(0, tm, rows):
        gate = conv(cg_ref, bg_ref, ug_scr, r0)
        val = conv(cv_ref, bv_ref, uv_scr, r0)
        a = (gate * jax.nn.sigmoid(gate) * val).astype(BF16)
        out_ref[r0:r0 + rows, :] += jnp.dot(a, wd_ref[...], preferred_element_type=F32)


def _ffn(h2, x1, w_up, w_conv, b_conv, w_down, *, seq_len):
    m, d = h2.shape
    d_ff = w_down.shape[0]
    tm, tf = 1024, 512
    nj = d_ff // tf
    kern = functools.partial(_ffn_kernel, tm=tm, tiles_per_seq=seq_len // tm)
    return pl.pallas_call(
        kern,
        grid=(m // tm, nj),
        in_specs=[
            pl.BlockSpec((tm, d), lambda i, j: (i, 0)),
            pl.BlockSpec((tm, d), lambda i, j: (i, 0)),
            pl.BlockSpec((d, tf), lambda i, j: (0, j)),
            pl.BlockSpec((d, tf), lambda i, j: (0, nj + j)),
            pl.BlockSpec((CONV_WIDTH, tf), lambda i, j: (0, j)),
            pl.BlockSpec((CONV_WIDTH, tf), lambda i, j: (0, nj + j)),
            pl.BlockSpec((1, tf), lambda i, j: (0, j)),
            pl.BlockSpec((1, tf), lambda i, j: (0, nj + j)),
            pl.BlockSpec((tf, d), lambda i, j: (j, 0)),
        ],
        out_specs=pl.BlockSpec((tm, d), lambda i, j: (i, 0)),
        out_shape=jax.ShapeDtypeStruct((m, d), F32),
        scratch_shapes=[
            pltpu.VMEM((tm + SUBLANES, tf), F32),
            pltpu.VMEM((tm + SUBLANES, tf), F32),
            pltpu.VMEM((nj, SUBLANES, tf), F32),
            pltpu.VMEM((nj, SUBLANES, tf), F32),
        ],
        compiler_params=_params(("arbitrary", "arbitrary"), 62 * 1024 * 1024),
        name="conv_ffn",
    )(h2, x1, w_up, w_up, w_conv, w_conv, b_conv, b_conv, w_down)


def _layer(x, g_attn, w_in, b_forget, g_q_fox, g_k_fox, g_q_dil, g_k_dil,
           w_br_fox, w_br_dil, w_out, g_ffn, w_up, w_conv, b_conv, w_down):
    b, s, d = x.shape
    w_fox = N_HEADS_FOX * HEAD_DIM
    w_dil = N_HEADS_DIL * HEAD_DIM
    n_qkv = 3 * w_fox + 3 * w_dil
    f_lo, f_hi = 3 * w_fox, 3 * w_fox + N_HEADS_FOX

    ones = jnp.ones((w_fox,), F32)
    g_qk = jnp.concatenate([g_q_fox.reshape(-1) * QK_SCALE2, g_k_fox.reshape(-1), ones,
                            g_q_dil.reshape(-1) * QK_SCALE2, g_k_dil.reshape(-1), ones]
                           ).reshape(1, n_qkv)
    b_pad = jnp.pad(b_forget, (0, LANES - N_HEADS_FOX)).reshape(1, LANES)
    slopes = jnp.asarray(2.0 ** (-8.0 * np.arange(1, N_HEADS_DIL + 1) / N_HEADS_DIL), dtype=F32)

    x2 = x.reshape(b * s, d)
    qkv, gates, f_pre = _in_proj(x2, g_attn.reshape(1, d), w_in.T, g_qk,
                                 f_lo=f_lo, f_hi=f_hi, n_qkv=n_qkv, n_gate=2 * d)
    qkv = qkv.reshape(b, s, n_qkv)

    ccol, crow = _forget(f_pre.reshape(b, s, LANES), b_pad, n_heads=N_HEADS_FOX)
    crow4 = crow.reshape(b, N_HEADS_FOX, 1, s)
    assert N_HEADS_FOX == N_HEADS_DIL
    o_a, o_b, w_up_b, w_down_b, w_out_b, w_br_fox_b, w_br_dil_b = _attention(
        qkv, ccol, crow4, slopes, [w_up, w_down, w_out, w_br_fox, w_br_dil],
        n_heads=N_HEADS_FOX, tq=256)

    x1, h2 = _mix(o_a.reshape(b * s, w_fox), o_b.reshape(b * s, w_dil), gates, x2,
                  w_br_fox_b, w_br_dil_b, w_out_b, g_ffn.reshape(1, d))
    out = _ffn(h2, x1, w_up_b, w_conv, b_conv.reshape(1, -1), w_down_b, seq_len=s)
    return out.reshape(b, s, d)


def kernel(x, g_attn, w_in, b_forget, g_q_fox, g_k_fox, g_q_dil, g_k_dil, w_br_fox, w_br_dil,
           w_out, g_ffn, w_up, w_conv, b_conv, w_down):
    for l in range(w_in.shape[0]):
        x = _layer(x, g_attn[l], w_in[l], b_forget[l], g_q_fox[l], g_k_fox[l], g_q_dil[l],
                   g_k_dil[l], w_br_fox[l], w_br_dil[l], w_out[l], g_ffn[l], w_up[l], w_conv[l],
                   b_conv[l], w_down[l])
    return x
```

```python
import functools

import numpy as np
import jax
import jax.numpy as jnp
from jax import lax
from jax.experimental import pallas as pl
from jax.experimental.pallas import tpu as pltpu

HEAD_DIM = 128
N_HEADS_FOX = 8
N_HEADS_DIL = 8
DIL_PATTERNS = ((128, 1), (512, 4), (2048, 16))
CONV_WIDTH = 3
PROJ_ROW_PARTS = 4
FFN_ROW_PARTS = 2
EPS = 1e-6
NEG_INF = -1e30
LOG2E = float(np.log2(np.e))
QK_SCALE2 =LOG2E / float(np.sqrt(HEAD_DIM))

F32 = jnp.float32
BF16 = jnp.bfloat16

LANES = 128
SUBLANES = 8
BF16_ROWS = 2 * SUBLANES
VMEM_LIMIT = 56 * 1024 * 1024

NT_DIMS = (((1,), (1,)), ((), ()))


def _params(sem, vmem_limit=VMEM_LIMIT):
    return pltpu.CompilerParams(dimension_semantics=sem, vmem_limit_bytes=vmem_limit)


def _in_proj_kernel(x_ref, g_ref, w_ref, wf_ref, gqk_ref, qkv_ref, gate_ref, f_ref, h_scr,
                    *, n_qkv_tiles, tiles_per_group, heads_per_tile):
    s = pl.program_id(1)
    j = s - 1
    half = x_ref.shape[0]

    def project(h, w_t):
        return lax.dot_general(h, w_t.astype(BF16), NT_DIMS, preferred_element_type=F32)

    @pl.when(s <= 1)
    def _():
        rows = pl.ds(pl.multiple_of(s * half, half), half)
        x = x_ref[...]
        ms = jnp.mean(x * x, axis=-1, keepdims=True)
        h = (x * lax.rsqrt(ms + EPS) * g_ref[...]).astype(BF16)
        h_scr[rows, :] = h
        wf = wf_ref[...]
        wf = jnp.concatenate([wf, jnp.zeros((LANES - wf.shape[0], wf.shape[1]), F32)], axis=0)
        f_ref[rows, :] = project(h, wf)

    def store_qk(acc, rows):
        for hh in range(heads_per_tile):
            sl = slice(hh * HEAD_DIM, (hh + 1) * HEAD_DIM)
            a = acc[:, sl]
            ms = jnp.mean(a * a, axis=-1, keepdims=True)
            qkv_ref[rows, sl] = (a * lax.rsqrt(ms + EPS) * gqk_ref[:, sl]).astype(BF16)

    def store_v(acc, rows):
        qkv_ref[rows, :] = acc.astype(BF16)

    def store_gate(acc, rows):
        gate_ref[rows, :] = acc

    def tile(store):
        w = w_ref[...].T.astype(BF16)
        part = h_scr.shape[0] // PROJ_ROW_PARTS
        for r0 in range(0, h_scr.shape[0], part):
            rows = slice(r0, r0 + part)
            store(jnp.dot(h_scr[rows, :], w, preferred_element_type=F32), rows)

    active = s >= 1
    is_qkv = jnp.logical_and(active, j < n_qkv_tiles)
    is_qk = jnp.logical_and(is_qkv, (j % (3 * tiles_per_group)) < 2 * tiles_per_group)
    cases = (
        (is_qk, store_qk),
        (jnp.logical_and(is_qkv, jnp.logical_not(is_qk)), store_v),
        (jnp.logical_and(active, j >= n_qkv_tiles), store_gate),
    )
    for cond, store in cases:
        pl.when(cond)(functools.partial(tile, store))


def _in_proj(x2, g_attn, w_t, g_qk, *, f_lo, f_hi, n_qkv, n_gate):
    m, d = x2.shape
    tm, tn = 2048, 512
    assert f_lo % tn == 0 and f_hi % SUBLANES == 0
    n_qkv_tiles = n_qkv // tn
    n_tiles = (n_qkv + n_gate) // tn
    kern = functools.partial(
        _in_proj_kernel, n_qkv_tiles=n_qkv_tiles,
        tiles_per_group=(N_HEADS_FOX * HEAD_DIM) // tn, heads_per_tile=tn // HEAD_DIM)
    tile = lambda s: jnp.maximum(s - 1, 0)
    qkv_tile = lambda s: jnp.minimum(tile(s), n_qkv_tiles - 1)
    w_row = lambda j: pl.multiple_of(j * tn + jnp.where(j * tn >= f_lo, f_hi - f_lo, 0), SUBLANES)
    return pl.pallas_call(
        kern,
        grid=(m // tm, n_tiles + 1),
        in_specs=[
            pl.BlockSpec((tm // 2, d), lambda i, s: (2 * i + jnp.minimum(s, 1), 0)),
            pl.BlockSpec((1, d), lambda i, s: (0, 0)),
            pl.BlockSpec((pl.Element(tn), pl.Element(d)), lambda i, s: (w_row(tile(s)), 0)),
            pl.BlockSpec((pl.Element(f_hi - f_lo), pl.Element(d)), lambda i, s: (f_lo, 0)),
            pl.BlockSpec((1, tn), lambda i, s: (0, qkv_tile(s))),
        ],
        out_specs=[
            pl.BlockSpec((tm, tn), lambda i, s: (i, qkv_tile(s))),
            pl.BlockSpec((tm, tn), lambda i, s: (i, jnp.maximum(tile(s) - n_qkv_tiles, 0))),
            pl.BlockSpec((tm, LANES), lambda i, s: (i, 0)),
        ],
        out_shape=[
            jax.ShapeDtypeStruct((m, n_qkv), BF16),
            jax.ShapeDtypeStruct((m, n_gate), F32),
            jax.ShapeDtypeStruct((m, LANES), F32),
        ],
        scratch_shapes=[pltpu.VMEM((tm, d), BF16)],
        compiler_params=_params(("arbitrary", "arbitrary")),
        name="in_proj",
    )(x2, g_attn, w_t, w_t, g_qk)


def _forget_kernel(f_ref, b_ref, ccol_ref, crow_ref, *, n_heads):
    z = f_ref[0] + b_ref[...]
    c = jax.nn.log_sigmoid(z)
    s = c.shape[0]
    row = lax.broadcasted_iota(jnp.int32, c.shape, 0)
    k = 1
    while k < s:
        c = c + jnp.where(row >= k, pltpu.roll(c, k, axis=0), 0.0)
        k *= 2
    ccol_ref[0] = c
    crow_ref[0] = c.T[:n_heads, :]


def _forget(f_pre, b_pad, *, n_heads):
    b, s, _ = f_pre.shape
    return pl.pallas_call(
        functools.partial(_forget_kernel, n_heads=n_heads),
        grid=(b,),
        in_specs=[
            pl.BlockSpec((1, s, LANES), lambda i: (i, 0, 0)),
            pl.BlockSpec((1, LANES), lambda i: (0, 0)),
        ],
        out_specs=[
            pl.BlockSpec((1, s, LANES), lambda i: (i, 0, 0)),
            pl.BlockSpec((1, n_heads, s), lambda i: (i, 0, 0)),
        ],
        out_shape=[
            jax.ShapeDtypeStruct((b, s, LANES), F32),
            jax.ShapeDtypeStruct((b, n_heads, s), F32),
        ],
        compiler_params=_params(("arbitrary",)),
        name="forget_cumsum",
    )(f_pre, b_pad)


def _causal_sweeps(streams, tq):
    s_len = streams[0][0].shape[1]
    n_q = s_len // tq

    def scores(stream, qi):
        q_ref, k_ref = stream[:2]
        q0, kv = qi * tq, (qi + 1) * tq
        return lax.dot_general(q_ref[0, q0:kv, :], k_ref[0, :kv, :], NT_DIMS,
                               preferred_element_type=F32)

    def biased(stream, qi, s):
        add_bias = stream[4]
        tiles = [add_bias(s[:, kj * tq:(kj + 1) * tq], qi, kj) for kj in range(qi + 1)]
        z = jnp.concatenate(tiles, axis=1) if qi else tiles[0]
        return z, jnp.max(z, axis=1, keepdims=True)

    def weights(z, m):
        return jnp.exp2(z - m).astype(BF16)

    def weighted(stream, qi, p):
        v_aug, o_ref = stream[2:4]
        q0, kv = qi * tq, (qi + 1) * tq
        o = jnp.dot(p, v_aug[0:kv, :], preferred_element_type=F32)
        o_ref[0, q0:kv, :] = (o[:, :HEAD_DIM] / o[:, HEAD_DIM:HEAD_DIM + 1]).astype(o_ref.dtype)

    logit = [biased(st, 0, scores(st, 0)) for st in streams]
    for qi in range(n_q):
        last = qi + 1 == n_q
        raw = [None if last else scores(st, qi + 1) for st in streams]
        probs = [weights(*zm) for zm in logit]
        for st, p in zip(streams, probs):
            weighted(st, qi, p)
        logit = [None if last else biased(st, qi + 1, s) for st, s in zip(streams, raw)]


def _fill_v_aug(v_ref, v_aug):
    v_aug[:, :HEAD_DIM] = v_ref[0]
    lane = lax.broadcasted_iota(jnp.int32, (v_aug.shape[0], v_aug.shape[1] - HEAD_DIM), 1)
    v_aug[:, HEAD_DIM:] = jnp.where(lane == 0, 1.0, 0.0).astype(BF16)


def _cast_specs(weights, n_steps, step_index):
    in_specs, out_specs, out_shapes = [], [], []
    for w in weights:
        rows, cols = w.shape[0] // n_steps, w.shape[1]
        assert rows * n_steps == w.shape[0] and rows % BF16_ROWS == 0
        spec = pl.BlockSpec((rows, cols), lambda *idx: (step_index(*idx), 0))
        in_specs.append(spec)
        out_specs.append(spec)
        out_shapes.append(jax.ShapeDtypeStruct(w.shape, BF16))
    return in_specs, out_specs, out_shapes


def _attention_kernel(slopes_ref, qa_ref, ka_ref, va_ref, qb_ref, kb_ref, vb_ref, ccol_ref, crow_ref,
                      *rest, tq):
    n_cast = (len(rest) - 5) // 2
    oa_ref, ob_ref = rest[n_cast], rest[n_cast + 1]
    bias_scr, va_aug, vb_aug = rest[-3:]
    _fill_v_aug(va_ref, va_aug)
    _fill_v_aug(vb_ref, vb_aug)
    h = pl.program_id(0)
    r = lax.broadcasted_iota(jnp.int32, (tq, tq), 0)
    c = lax.broadcasted_iota(jnp.int32, (tq, tq), 1)

    @pl.when(pl.program_id(1) == 0)
    def _():
        slope = slopes_ref[h]
        for d in range(bias_scr.shape[0]):
            delta = r - c + d * tq
            count = jnp.zeros((tq, tq), F32)
            for window, dil in DIL_PATTERNS:
                assert dil & (dil - 1) == 0
                member = jnp.logical_and((delta & (dil - 1)) == 0, delta <= window)
                count = count + jnp.where(member, 1.0, 0.0)
            valid = jnp.logical_and(delta >= 0, count > 0.0)
            bias = jnp.log(jnp.maximum(count, 1.0)) - slope * delta.astype(F32)
            bias_scr[d] = jnp.where(valid, bias, NEG_INF) * LOG2E

    s_len = qa_ref.shape[1]
    lane = lax.broadcasted_iota(jnp.int32, (s_len, LANES), 1)
    cq = jnp.sum(jnp.where(lane == h, ccol_ref[0], 0.0), axis=1, keepdims=True) * LOG2E
    ck = crow_ref[0, 0] * LOG2E
    causal = c <= r

    def forget_bias(tile, qi, kj):
        z = tile + cq[qi * tq:(qi + 1) * tq] - ck[:, kj * tq:(kj + 1) * tq]
        return jnp.where(causal, z, NEG_INF) if kj == qi else z

    def distance_bias(tile, qi, kj):
        return tile + bias_scr[qi - kj]

    _causal_sweeps([(qa_ref, ka_ref, va_aug, oa_ref, forget_bias),
                    (qb_ref, kb_ref, vb_aug, ob_ref, distance_bias)], tq)

    for i_ref, o_ref in zip(rest[:n_cast], rest[n_cast + 2:-3]):
        o_ref[...] = i_ref[...].astype(BF16)


def _attention(qkv, ccol, crow4, slopes, cast_weights, *, n_heads, tq):
    b, s, _ = qkv.shape
    spec = lambda group: pl.BlockSpec((1, s, HEAD_DIM),
                                      lambda h, bi, sl: (bi, 0, group * n_heads + h))
    out_spec = pl.BlockSpec((1, s, HEAD_DIM), lambda h, bi, sl: (bi, 0, h))
    c_in, c_out, c_shapes = _cast_specs(cast_weights, b * n_heads, lambda h, bi, sl: h * b + bi)
    o_shape = jax.ShapeDtypeStruct((b, s, n_heads * HEAD_DIM), BF16)
    return pl.pallas_call(
        functools.partial(_attention_kernel, tq=tq),
        grid_spec=pltpu.PrefetchScalarGridSpec(
            num_scalar_prefetch=1,
            grid=(n_heads, b),
            in_specs=[spec(g) for g in range(6)] + [
                pl.BlockSpec((1, s, LANES), lambda h, bi, sl: (bi, 0, 0)),
                pl.BlockSpec((1, 1, 1, s), lambda h, bi, sl: (bi, h, 0, 0)),
            ] + c_in,
            out_specs=[out_spec, out_spec] + c_out,
            scratch_shapes=[pltpu.VMEM((s // tq, tq, tq), F32),
                            pltpu.VMEM((s, 2 * HEAD_DIM), BF16),
                            pltpu.VMEM((s, 2 * HEAD_DIM), BF16)],
        ),
        out_shape=[o_shape, o_shape] + c_shapes,
        compiler_params=_params(("arbitrary", "arbitrary")),
        name="attention",
    )(slopes, *([qkv] * 6), ccol, crow4, *cast_weights)


def _mix_kernel(oa_ref, ob_ref, ga_ref, gb_ref, x_ref, wa_ref, wb_ref, wo_ref, g_ref,
                x1_ref, h2_ref):
    ta = jnp.dot(oa_ref[...], wa_ref[...], preferred_element_type=F32)
    tb = jnp.dot(ob_ref[...], wb_ref[...], preferred_element_type=F32)
    merged = (jax.nn.sigmoid(ga_ref[...]) * ta + jax.nn.sigmoid(gb_ref[...]) * tb).astype(BF16)
    x1 = x_ref[...] + jnp.dot(merged, wo_ref[...], preferred_element_type=F32)
    x1_ref[...] = x1
    ms = jnp.mean(x1 * x1, axis=-1, keepdims=True)
    h2_ref[...] = (x1 * lax.rsqrt(ms + EPS) * g_ref[...]).astype(BF16)


def _mix(oa, ob, gates, x2, w_a, w_b, w_o, g_ffn):
    m, d = x2.shape
    wa_rows, wb_rows = w_a.shape[0], w_b.shape[0]
    tm = 256
    resident = lambda shape: pl.BlockSpec(shape, lambda i: (0, 0), pipeline_mode=pl.Buffered(1))
    return pl.pallas_call(
        _mix_kernel,
        grid=(m // tm,),
        in_specs=[
            pl.BlockSpec((tm, wa_rows), lambda i: (i, 0)),
            pl.BlockSpec((tm, wb_rows), lambda i: (i, 0)),
            pl.BlockSpec((tm, d), lambda i: (i, 0)),
            pl.BlockSpec((tm, d), lambda i: (i, 1)),
            pl.BlockSpec((tm, d), lambda i: (i, 0)),
            resident((wa_rows, d)),
            resident((wb_rows, d)),
            resident((d, d)),
            pl.BlockSpec((1, d), lambda i: (0, 0)),
        ],
        out_specs=[
            pl.BlockSpec((tm, d), lambda i: (i, 0)),
            pl.BlockSpec((tm, d), lambda i: (i, 0)),
        ],
        out_shape=[
            jax.ShapeDtypeStruct((m, d), F32),
            jax.ShapeDtypeStruct((m, d), BF16),
        ],
        compiler_params=_params(("arbitrary",)),
        name="branch_mix_out_proj",
    )(oa, ob, gates, gates, x2, w_a, w_b, w_o, g_ffn)


def _ffn_kernel(h_ref, x1_ref, wg_ref, wv_ref, cg_ref, cv_ref, bg_ref, bv_ref, wd_ref,
                out_ref, ug_scr, uv_scr, carry_g, carry_v, *, tm, tiles_per_seq):
    i = pl.program_id(0)
    j = pl.program_id(1)
    first = (i % tiles_per_seq) == 0

    @pl.when(jnp.logical_and(i == 0, j == 0))
    def _():
        carry_g[...] = jnp.zeros(carry_g.shape, F32)
        carry_v[...] = jnp.zeros(carry_v.shape, F32)

    @pl.when(j == 0)
    def _():
        out_ref[...] = x1_ref[...]

    rows = tm // FFN_ROW_PARTS

    def up(w_ref, u_scr, r0):
        u_scr[SUBLANES + r0:SUBLANES + r0 + rows, :] = jnp.dot(
            h_ref[r0:r0 + rows, :], w_ref[...], preferred_element_type=F32)

    def conv(c_ref, b_ref, u_scr, r0):
        out = b_ref[...]
        for t in range(CONV_WIDTH):
            shift = CONV_WIDTH - 1 - t
            out = out + c_ref[t:t + 1, :] * u_scr[pl.ds(SUBLANES - shift + r0, rows), :]
        return out

    for u_scr, carry in ((ug_scr, carry_g), (uv_scr, carry_v)):
        prev = carry[j]
        u_scr[0:SUBLANES, :] = jnp.where(first, jnp.zeros_like(prev), prev)
    for r0 in range(0, tm, rows):
        up(wg_ref, ug_scr, r0)
        up(wv_ref, uv_scr, r0)
    carry_g[j] = ug_scr[tm:tm + SUBLANES, :]
    carry_v[j] = uv_scr[tm:tm + SUBLANES, :]
    for r0 in range(0, tm, rows):
        gate = conv(cg_ref, bg_ref, ug_scr, r0)
        val = conv(cv_ref, bv_ref, uv_scr, r0)
        a = (gate * jax.nn.sigmoid(gate) * val).astype(BF16)
        out_ref[r0:r0 + rows, :] += jnp.dot(a, wd_ref[...], preferred_element_type=F32)


def _ffn(h2, x1, w_up, w_conv, b_conv, w_down, *, seq_len):
    m, d = h2.shape
    d_ff = w_down.shape[0]
    tm, tf = 1024, 512
    nj = d_ff // tf
    kern = functools.partial(_ffn_kernel, tm=tm, tiles_per_seq=seq_len // tm)
    return pl.pallas_call(
        kern,
        grid=(m // tm, nj),
        in_specs=[
            pl.BlockSpec((tm, d), lambda i, j: (i, 0)),
            pl.BlockSpec((tm, d), lambda i, j: (i, 0)),
            pl.BlockSpec((d, tf), lambda i, j: (0, j)),
            pl.BlockSpec((d, tf), lambda i, j: (0, nj + j)),
            pl.BlockSpec((CONV_WIDTH, tf), lambda i, j: (0, j)),
            pl.BlockSpec((CONV_WIDTH, tf), lambda i, j: (0, nj + j)),
            pl.BlockSpec((1, tf), lambda i, j: (0, j)),
            pl.BlockSpec((1, tf), lambda i, j: (0, nj + j)),
            pl.BlockSpec((tf, d), lambda i, j: (j, 0)),
        ],
        out_specs=pl.BlockSpec((tm, d), lambda i, j: (i, 0)),
        out_shape=jax.ShapeDtypeStruct((m, d), F32),
        scratch_shapes=[
            pltpu.VMEM((tm + SUBLANES, tf), F32),
            pltpu.VMEM((tm + SUBLANES, tf), F32),
            pltpu.VMEM((nj, SUBLANES, tf), F32),
            pltpu.VMEM((nj, SUBLANES, tf), F32),
        ],
        compiler_params=_params(("arbitrary", "arbitrary"), 62 * 1024 * 1024),
        name="conv_ffn",
    )(h2, x1, w_up, w_up, w_conv, w_conv, b_conv, b_conv, w_down)


def _layer(x, g_attn, w_in, b_forget, g_q_fox, g_k_fox, g_q_dil, g_k_dil,
           w_br_fox, w_br_dil, w_out, g_ffn, w_up, w_conv, b_conv, w_down):
    b, s, d = x.shape
    w_fox = N_HEADS_FOX * HEAD_DIM
    w_dil = N_HEADS_DIL * HEAD_DIM
    n_qkv = 3 * w_fox + 3 * w_dil
    f_lo, f_hi = 3 * w_fox, 3 * w_fox + N_HEADS_FOX

    ones = jnp.ones((w_fox,), F32)
    g_qk = jnp.concatenate([g_q_fox.reshape(-1) * QK_SCALE2, g_k_fox.reshape(-1), ones,
                            g_q_dil.reshape(-1) * QK_SCALE2, g_k_dil.reshape(-1), ones]
                           ).reshape(1, n_qkv)
    b_pad = jnp.pad(b_forget, (0, LANES - N_HEADS_FOX)).reshape(1, LANES)
    slopes = jnp.asarray(2.0 ** (-8.0 * np.arange(1, N_HEADS_DIL + 1) / N_HEADS_DIL), dtype=F32)

    x2 = x.reshape(b * s, d)
    qkv, gates, f_pre = _in_proj(x2, g_attn.reshape(1, d), w_in.T, g_qk,
                                 f_lo=f_lo, f_hi=f_hi, n_qkv=n_qkv, n_gate=2 * d)
    qkv = qkv.reshape(b, s, n_qkv)

    ccol, crow = _forget(f_pre.reshape(b, s, LANES), b_pad, n_heads=N_HEADS_FOX)
    crow4 = crow.reshape(b, N_HEADS_FOX, 1, s)
    assert N_HEADS_FOX == N_HEADS_DIL
    o_a, o_b, w_up_b, w_down_b, w_out_b, w_br_fox_b, w_br_dil_b = _attention(
        qkv, ccol, crow4, slopes, [w_up, w_down, w_out, w_br_fox, w_br_dil],
        n_heads=N_HEADS_FOX, tq=256)

    x1, h2 = _mix(o_a.reshape(b * s, w_fox), o_b.reshape(b * s, w_dil), gates, x2,
                  w_br_fox_b, w_br_dil_b, w_out_b, g_ffn.reshape(1, d))
    out = _ffn(h2, x1, w_up_b, w_conv, b_conv.reshape(1, -1), w_down_b, seq_len=s)
    return out.reshape(b, s, d)


def kernel(x, g_attn, w_in, b_forget, g_q_fox, g_k_fox, g_q_dil, g_k_dil, w_br_fox, w_br_dil,
           w_out, g_ffn, w_up, w_conv, b_conv, w_down):
    for l in range(w_in.shape[0]):
        x = _layer(x, g_attn[l], w_in[l], b_forget[l], g_q_fox[l], g_k_fox[l], g_q_dil[l],
                   g_k_dil[l], w_br_fox[l], w_br_dil[l], w_out[l], g_ffn[l], w_up[l], w_conv[l],
                   b_conv[l], w_down[l])
    return x
```

```python
import functools

import numpy as np
import jax
import jax.numpy as jnp
from jax import lax
from jax.experimental import pallas as pl
from jax.experimental.pallas import tpu as pltpu

HEAD_DIM = 128
N_HEADS_FOX = 8
N_HEADS_DIL = 8
DIL_PATTERNS = ((128, 1), (512, 4), (2048, 16))
CONV_WIDTH = 3
PROJ_ROW_PARTS = 4
FFN_ROW_PARTS = 2
EPS = 1e-6
NEG_INF = -1e30
LOG2E = float(np.log2(np.e))
QK_SCALE2 =LOG2E / float(np.sqrt(HEAD_DIM))

F32 = jnp.float32
BF16 = jnp.bfloat16

LANES = 128
SUBLANES = 8
BF16_ROWS = 2 * SUBLANES
VMEM_LIMIT = 56 * 1024 * 1024

NT_DIMS = (((1,), (1,)), ((), ()))


def _params(sem, vmem_limit=VMEM_LIMIT):
    return pltpu.CompilerParams(dimension_semantics=sem, vmem_limit_bytes=vmem_limit)


def _in_proj_kernel(x_ref, g_ref, w_ref, wf_ref, gqk_ref, qkv_ref, gate_ref, f_ref, h_scr,
                    *, n_qkv_tiles, tiles_per_group, heads_per_tile):
    s = pl.program_id(1)
    j = s - 1
    half = x_ref.shape[0]

    def project(h, w_t):
        return lax.dot_general(h, w_t.astype(BF16), NT_DIMS, preferred_element_type=F32)

    @pl.when(s <= 1)
    def _():
        rows = pl.ds(pl.multiple_of(s * half, half), half)
        x = x_ref[...]
        ms = jnp.mean(x * x, axis=-1, keepdims=True)
        h = (x * lax.rsqrt(ms + EPS) * g_ref[...]).astype(BF16)
        h_scr[rows, :] = h
        wf = wf_ref[...]
        wf = jnp.concatenate([wf, jnp.zeros((LANES - wf.shape[0], wf.shape[1]), F32)], axis=0)
        f_ref[rows, :] = project(h, wf)

    def store_qk(acc, rows):
        for hh in range(heads_per_tile):
            sl = slice(hh * HEAD_DIM, (hh + 1) * HEAD_DIM)
            a = acc[:, sl]
            ms = jnp.mean(a * a, axis=-1, keepdims=True)
            qkv_ref[rows, sl] = (a * lax.rsqrt(ms + EPS) * gqk_ref[:, sl]).astype(BF16)

    def store_v(acc, rows):
        qkv_ref[rows, :] = acc.astype(BF16)

    def store_gate(acc, rows):
        gate_ref[rows, :] = acc.astype(gate_ref.dtype)

    def tile(store):
        w = w_ref[...].astype(BF16)
        part = h_scr.shape[0] // PROJ_ROW_PARTS
        for r0 in range(0, h_scr.shape[0], part):
            rows = slice(r0, r0 + part)
            store(lax.dot_general(h_scr[rows, :], w, NT_DIMS, preferred_element_type=F32), rows)

    active = s >= 1
    is_qkv = jnp.logical_and(active, j < n_qkv_tiles)
    is_qk = jnp.logical_and(is_qkv, (j % (3 * tiles_per_group)) < 2 * tiles_per_group)
    cases = (
        (is_qk, store_qk),
        (jnp.logical_and(is_qkv, jnp.logical_not(is_qk)), store_v),
        (jnp.logical_and(active, j >= n_qkv_tiles), store_gate),
    )
    for cond, store in cases:
        pl.when(cond)(functools.partial(tile, store))


def _in_proj(x2, g_attn, w_t, g_qk, *, f_lo, f_hi, n_qkv, n_gate):
    m, d = x2.shape
    tm, tn = 2048, 512
    assert f_lo % tn == 0 and f_hi % SUBLANES == 0
    n_qkv_tiles = n_qkv // tn
    n_tiles = (n_qkv + n_gate) // tn
    kern = functools.partial(
        _in_proj_kernel, n_qkv_tiles=n_qkv_tiles,
        tiles_per_group=(N_HEADS_FOX * HEAD_DIM) // tn, heads_per_tile=tn // HEAD_DIM)
    tile = lambda s: jnp.maximum(s - 1, 0)
    qkv_tile = lambda s: jnp.minimum(tile(s), n_qkv_tiles - 1)
    w_row = lambda j: pl.multiple_of(j * tn + jnp.where(j * tn >= f_lo, f_hi - f_lo, 0), SUBLANES)
    return pl.pallas_call(
        kern,
        grid=(m // tm, n_tiles + 1),
        in_specs=[
            pl.BlockSpec((tm // 2, d), lambda i, s: (2 * i + jnp.minimum(s, 1), 0)),
            pl.BlockSpec((1, d), lambda i, s: (0, 0)),
            pl.BlockSpec((pl.Element(tn), pl.Element(d)), lambda i, s: (w_row(tile(s)), 0)),
            pl.BlockSpec((pl.Element(f_hi - f_lo), pl.Element(d)), lambda i, s: (f_lo, 0)),
            pl.BlockSpec((1, tn), lambda i, s: (0, qkv_tile(s))),
        ],
        out_specs=[
            pl.BlockSpec((tm, tn), lambda i, s: (i, qkv_tile(s))),
            pl.BlockSpec((tm, tn), lambda i, s: (i, jnp.maximum(tile(s) - n_qkv_tiles, 0))),
            pl.BlockSpec((tm, LANES), lambda i, s: (i, 0)),
        ],
        out_shape=[
            jax.ShapeDtypeStruct((m, n_qkv), BF16),
            jax.ShapeDtypeStruct((m, n_gate), BF16),
            jax.ShapeDtypeStruct((m, LANES), F32),
        ],
        scratch_shapes=[pltpu.VMEM((tm, d), BF16)],
        compiler_params=_params(("arbitrary", "arbitrary")),
        name="in_proj",
    )(x2, g_attn, w_t, w_t, g_qk)


def _forget_kernel(f_ref, b_ref, ccol_ref, crow_ref, *, n_heads):
    z = f_ref[0] + b_ref[...]
    c = jax.nn.log_sigmoid(z)
    s = c.shape[0]
    row = lax.broadcasted_iota(jnp.int32, c.shape, 0)
    k = 1
    while k < s:
        c = c + jnp.where(row >= k, pltpu.roll(c, k, axis=0), 0.0)
        k *= 2
    ccol_ref[0] = c
    crow_ref[0] = c.T[:n_heads, :]


def _forget(f_pre, b_pad, *, n_heads):
    b, s, _ = f_pre.shape
    return pl.pallas_call(
        functools.partial(_forget_kernel, n_heads=n_heads),
        grid=(b,),
        in_specs=[
            pl.BlockSpec((1, s, LANES), lambda i: (i, 0, 0)),
            pl.BlockSpec((1, LANES), lambda i: (0, 0)),
        ],
        out_specs=[
            pl.BlockSpec((1, s, LANES), lambda i: (i, 0, 0)),
            pl.BlockSpec((1, n_heads, s), lambda i: (i, 0, 0)),
        ],
        out_shape=[
            jax.ShapeDtypeStruct((b, s, LANES), F32),
            jax.ShapeDtypeStruct((b, n_heads, s), F32),
        ],
        compiler_params=_params(("arbitrary",)),
        name="forget_cumsum",
    )(f_pre, b_pad)


def _causal_sweeps(streams, tq):
    s_len = streams[0][0].shape[1]
    n_q = s_len // tq

    def scores(stream, qi):
        q_ref, k_ref = stream[:2]
        q0, kv = qi * tq, (qi + 1) * tq
        return lax.dot_general(q_ref[0, q0:kv, :], k_ref[0, :kv, :], NT_DIMS,
                               preferred_element_type=F32)

    def biased(stream, qi, s):
        add_bias = stream[4]
        tiles = [add_bias(s[:, kj * tq:(kj + 1) * tq], qi, kj) for kj in range(qi + 1)]
        z = jnp.concatenate(tiles, axis=1) if qi else tiles[0]
        return z, jnp.max(z, axis=1, keepdims=True)

    def weights(z, m):
        return jnp.exp2(z - m).astype(BF16)

    def weighted(stream, qi, p):
        v_aug, o_ref = stream[2:4]
        q0, kv = qi * tq, (qi + 1) * tq
        o = jnp.dot(p, v_aug[0:kv, :], preferred_element_type=F32)
        o_ref[0, q0:kv, :] = (o[:, :HEAD_DIM] / o[:, HEAD_DIM:HEAD_DIM + 1]).astype(o_ref.dtype)

    logit = [biased(st, 0, scores(st, 0)) for st in streams]
    for qi in range(n_q):
        last = qi + 1 == n_q
        raw = [None if last else scores(st, qi + 1) for st in streams]
        probs = [weights(*zm) for zm in logit]
        for st, p in zip(streams, probs):
            weighted(st, qi, p)
        logit = [None if last else biased(st, qi + 1, s) for st, s in zip(streams, raw)]


def _fill_v_aug(v_ref, v_aug):
    v_aug[:, :HEAD_DIM] = v_ref[0]
    lane = lax.broadcasted_iota(jnp.int32, (v_aug.shape[0], v_aug.shape[1] - HEAD_DIM), 1)
    v_aug[:, HEAD_DIM:] = jnp.where(lane == 0, 1.0, 0.0).astype(BF16)


def _cast_specs(weights, n_steps, step_index):
    in_specs, out_specs, out_shapes = [], [], []
    for w in weights:
        rows, cols = w.shape[0] // n_steps, w.shape[1]
        assert rows * n_steps == w.shape[0] and rows % BF16_ROWS == 0
        spec = pl.BlockSpec((rows, cols), lambda *idx: (step_index(*idx), 0))
        in_specs.append(spec)
        out_specs.append(spec)
        out_shapes.append(jax.ShapeDtypeStruct(w.shape, BF16))
    return in_specs, out_specs, out_shapes


def _attention_kernel(slopes_ref, qa_ref, ka_ref, va_ref, qb_ref, kb_ref, vb_ref, ccol_ref, crow_ref,
                      *rest, tq):
    n_cast = (len(rest) - 5) // 2
    oa_ref, ob_ref = rest[n_cast], rest[n_cast + 1]
    bias_scr, va_aug, vb_aug = rest[-3:]
    _fill_v_aug(va_ref, va_aug)
    _fill_v_aug(vb_ref, vb_aug)
    h = pl.program_id(0)
    r = lax.broadcasted_iota(jnp.int32, (tq, tq), 0)
    c = lax.broadcasted_iota(jnp.int32, (tq, tq), 1)

    @pl.when(pl.program_id(1) == 0)
    def _():
        slope = slopes_ref[h]
        for d in range(bias_scr.shape[0]):
            delta = r - c + d * tq
            count = jnp.zeros((tq, tq), F32)
            for window, dil in DIL_PATTERNS:
                assert dil & (dil - 1) == 0
                member = jnp.logical_and((delta & (dil - 1)) == 0, delta <= window)
                count = count + jnp.where(member, 1.0, 0.0)
            valid = jnp.logical_and(delta >= 0, count > 0.0)
            bias = jnp.log(jnp.maximum(count, 1.0)) - slope * delta.astype(F32)
            bias_scr[d] = jnp.where(valid, bias, NEG_INF) * LOG2E

    s_len = qa_ref.shape[1]
    lane = lax.broadcasted_iota(jnp.int32, (s_len, LANES), 1)
    cq = jnp.sum(jnp.where(lane == h, ccol_ref[0], 0.0), axis=1, keepdims=True) * LOG2E
    ck = crow_ref[0, 0] * LOG2E
    causal = c <= r

    def forget_bias(tile, qi, kj):
        z = tile + cq[qi * tq:(qi + 1) * tq] - ck[:, kj * tq:(kj + 1) * tq]
        return jnp.where(causal, z, NEG_INF) if kj == qi else z

    def distance_bias(tile, qi, kj):
        return tile + bias_scr[qi - kj]

    _causal_sweeps([(qa_ref, ka_ref, va_aug, oa_ref, forget_bias),
                    (qb_ref, kb_ref, vb_aug, ob_ref, distance_bias)], tq)

    for i_ref, o_ref in zip(rest[:n_cast], rest[n_cast + 2:-3]):
        o_ref[...] = i_ref[...].astype(BF16)


def _attention(qkv, ccol, crow4, slopes, cast_weights, *, n_heads, tq):
    b, s, _ = qkv.shape
    spec = lambda group: pl.BlockSpec((1, s, HEAD_DIM),
                                      lambda h, bi, sl: (bi, 0, group * n_heads + h))
    out_spec = pl.BlockSpec((1, s, HEAD_DIM), lambda h, bi, sl: (bi, 0, h))
    c_in, c_out, c_shapes = _cast_specs(cast_weights, b * n_heads, lambda h, bi, sl: h * b + bi)
    o_shape = jax.ShapeDtypeStruct((b, s, n_heads * HEAD_DIM), BF16)
    return pl.pallas_call(
        functools.partial(_attention_kernel, tq=tq),
        grid_spec=pltpu.PrefetchScalarGridSpec(
            num_scalar_prefetch=1,
            grid=(n_heads, b),
            in_specs=[spec(g) for g in range(6)] + [
                pl.BlockSpec((1, s, LANES), lambda h, bi, sl: (bi, 0, 0)),
                pl.BlockSpec((1, 1, 1, s), lambda h, bi, sl: (bi, h, 0, 0)),
            ] + c_in,
            out_specs=[out_spec, out_spec] + c_out,
            scratch_shapes=[pltpu.VMEM((s // tq, tq, tq), F32),
                            pltpu.VMEM((s, 2 * HEAD_DIM), BF16),
                            pltpu.VMEM((s, 2 * HEAD_DIM), BF16)],
        ),
        out_shape=[o_shape, o_shape] + c_shapes,
        compiler_params=_params(("arbitrary", "arbitrary")),
        name="attention",
    )(slopes, *([qkv] * 6), ccol, crow4, *cast_weights)


def _mix_kernel(oa_ref, ob_ref, ga_ref, gb_ref, x_ref, wa_ref, wb_ref, wo_ref, g_ref,
                x1_ref, h2_ref):
    ta = jnp.dot(oa_ref[...], wa_ref[...], preferred_element_type=F32)
    tb = jnp.dot(ob_ref[...], wb_ref[...], preferred_element_type=F32)
    merged = (jax.nn.sigmoid(ga_ref[...].astype(F32)) * ta
              + jax.nn.sigmoid(gb_ref[...].astype(F32)) * tb).astype(BF16)
    x1 = x_ref[...] + jnp.dot(merged, wo_ref[...], preferred_element_type=F32)
    x1_ref[...] = x1
    ms = jnp.mean(x1 * x1, axis=-1, keepdims=True)
    h2_ref[...] = (x1 * lax.rsqrt(ms + EPS) * g_ref[...]).astype(BF16)


def _mix(oa, ob, gates, x2, w_a, w_b, w_o, g_ffn):
    m, d = x2.shape
    wa_rows, wb_rows = w_a.shape[0], w_b.shape[0]
    tm = 256
    resident = lambda shape: pl.BlockSpec(shape, lambda i: (0, 0), pipeline_mode=pl.Buffered(1))
    return pl.pallas_call(
        _mix_kernel,
        grid=(m // tm,),
        in_specs=[
            pl.BlockSpec((tm, wa_rows), lambda i: (i, 0)),
            pl.BlockSpec((tm, wb_rows), lambda i: (i, 0)),
            pl.BlockSpec((tm, d), lambda i: (i, 0)),
            pl.BlockSpec((tm, d), lambda i: (i, 1)),
            pl.BlockSpec((tm, d), lambda i: (i, 0)),
            resident((wa_rows, d)),
            resident((wb_rows, d)),
            resident((d, d)),
            pl.BlockSpec((1, d), lambda i: (0, 0)),
        ],
        out_specs=[
            pl.BlockSpec((tm, d), lambda i: (i, 0)),
            pl.BlockSpec((tm, d), lambda i: (i, 0)),
        ],
        out_shape=[
            jax.ShapeDtypeStruct((m, d), F32),
            jax.ShapeDtypeStruct((m, d), BF16),
        ],
        compiler_params=_params(("arbitrary",)),
        name="branch_mix_out_proj",
    )(oa, ob, gates, gates, x2, w_a, w_b, w_o, g_ffn)


def _ffn_kernel(h_ref, x1_ref, wg_ref, wv_ref, cg_ref, cv_ref, bg_ref, bv_ref, wd_ref,
                out_ref, ug_scr, uv_scr, carry_g, carry_v, *, tm, tiles_per_seq):
    i = pl.program_id(0)
    j = pl.program_id(1)
    first = (i % tiles_per_seq) == 0

    @pl.when(jnp.logical_and(i == 0, j == 0))
    def _():
        carry_g[...] = jnp.zeros(carry_g.shape, F32)
        carry_v[...] = jnp.zeros(carry_v.shape, F32)

    @pl.when(j == 0)
    def _():
        out_ref[...] = x1_ref[...]

    rows = tm // FFN_ROW_PARTS

    def up(w_ref, u_scr, r0):
        u_scr[SUBLANES + r0:SUBLANES + r0 + rows, :] = jnp.dot(
            h_ref[r0:r0 + rows, :], w_ref[...], preferred_element_type=F32)

    def conv(c_ref, b_ref, u_scr, r0):
        out = b_ref[...]
        for t in range(CONV_WIDTH):
            shift = CONV_WIDTH - 1 - t
            out = out + c_ref[t:t + 1, :] * u_scr[pl.ds(SUBLANES - shift + r0, rows), :]
        return out

    for u_scr, carry in ((ug_scr, carry_g), (uv_scr, carry_v)):
        prev = carry[j]
        u_scr[0:SUBLANES, :] = jnp.where(first, jnp.zeros_like(prev), prev)
    for r0 in range(0, tm, rows):
        up(wg_ref, ug_scr, r0)
        up(wv_ref, uv_scr, r0)
    carry_g[j] = ug_scr[tm:tm + SUBLANES, :]
    carry_v[j] = uv_scr[tm:tm + SUBLANES, :]
    for r0 in range(0, tm, rows):
        gate = conv(cg_ref, bg_ref, ug_scr, r0)
        val = conv(cv_ref, bv_ref, uv_scr, r0)
        a = (gate * jax.nn.sigmoid(gate) * val).astype(BF16)
        out_ref[r0:r0 + rows, :] += jnp.dot(a, wd_ref[...], preferred_element_type=F32)


def _ffn(h2, x1, w_up, w_conv, b_conv, w_down, *, seq_len):
    m, d = h2.shape
    d_ff = w_down.shape[0]
    tm, tf = 1024, 512
    nj = d_ff // tf
    kern = functools.partial(_ffn_kernel, tm=tm, tiles_per_seq=seq_len // tm)
    return pl.pallas_call(
        kern,
        grid=(m // tm, nj),
        in_specs=[
            pl.BlockSpec((tm, d), lambda i, j: (i, 0)),
            pl.BlockSpec((tm, d), lambda i, j: (i, 0)),
            pl.BlockSpec((d, tf), lambda i, j: (0, j)),
            pl.BlockSpec((d, tf), lambda i, j: (0, nj + j)),
            pl.BlockSpec((CONV_WIDTH, tf), lambda i, j: (0, j)),
            pl.BlockSpec((CONV_WIDTH, tf), lambda i, j: (0, nj + j)),
            pl.BlockSpec((1, tf), lambda i, j: (0, j)),
            pl.BlockSpec((1, tf), lambda i, j: (0, nj + j)),
            pl.BlockSpec((tf, d), lambda i, j: (j, 0)),
        ],
        out_specs=pl.BlockSpec((tm, d), lambda i, j: (i, 0)),
        out_shape=jax.ShapeDtypeStruct((m, d), F32),
        scratch_shapes=[
            pltpu.VMEM((tm + SUBLANES, tf), F32),
            pltpu.VMEM((tm + SUBLANES, tf), F32),
            pltpu.VMEM((nj, SUBLANES, tf), F32),
            pltpu.VMEM((nj, SUBLANES, tf), F32),
        ],
        compiler_params=_params(("arbitrary", "arbitrary"), 62 * 1024 * 1024),
        name="conv_ffn",
    )(h2, x1, w_up, w_up, w_conv, w_conv, b_conv, b_conv, w_down)


def _layer(x, g_attn, w_in, b_forget, g_q_fox, g_k_fox, g_q_dil, g_k_dil,
           w_br_fox, w_br_dil, w_out, g_ffn, w_up, w_conv, b_conv, w_down):
    b, s, d = x.shape
    w_fox = N_HEADS_FOX * HEAD_DIM
    w_dil = N_HEADS_DIL * HEAD_DIM
    n_qkv = 3 * w_fox + 3 * w_dil
    f_lo, f_hi = 3 * w_fox, 3 * w_fox + N_HEADS_FOX

    ones = jnp.ones((w_fox,), F32)
    g_qk = jnp.concatenate([g_q_fox.reshape(-1) * QK_SCALE2, g_k_fox.reshape(-1), ones,
                            g_q_dil.reshape(-1) * QK_SCALE2, g_k_dil.reshape(-1), ones]
                           ).reshape(1, n_qkv)
    b_pad = jnp.pad(b_forget, (0, LANES - N_HEADS_FOX)).reshape(1, LANES)
    slopes = jnp.asarray(2.0 ** (-8.0 * np.arange(1, N_HEADS_DIL + 1) / N_HEADS_DIL), dtype=F32)

    x2 = x.reshape(b * s, d)
    qkv, gates, f_pre = _in_proj(x2, g_attn.reshape(1, d), w_in.T, g_qk,
                                 f_lo=f_lo, f_hi=f_hi, n_qkv=n_qkv, n_gate=2 * d)
    qkv = qkv.reshape(b, s, n_qkv)

    ccol, crow = _forget(f_pre.reshape(b, s, LANES), b_pad, n_heads=N_HEADS_FOX)
    crow4 = crow.reshape(b, N_HEADS_FOX, 1, s)
    assert N_HEADS_FOX == N_HEADS_DIL
    o_a, o_b, w_up_b, w_down_b, w_out_b, w_br_fox_b, w_br_dil_b = _attention(
        qkv, ccol, crow4, slopes, [w_up, w_down, w_out, w_br_fox, w_br_dil],
        n_heads=N_HEADS_FOX, tq=256)

    x1, h2 = _mix(o_a.reshape(b * s, w_fox), o_b.reshape(b * s, w_dil), gates, x2,
                  w_br_fox_b, w_br_dil_b, w_out_b, g_ffn.reshape(1, d))
    out = _ffn(h2, x1, w_up_b, w_conv, b_conv.reshape(1, -1), w_down_b, seq_len=s)
    return out.reshape(b, s, d)


def kernel(x, g_attn, w_in, b_forget, g_q_fox, g_k_fox, g_q_dil, g_k_dil, w_br_fox, w_br_dil,
           w_out, g_ffn, w_up, w_conv, b_conv, w_down):
    for l in range(w_in.shape[0]):
        x = _layer(x, g_attn[l], w_in[l], b_forget[l], g_q_fox[l], g_k_fox[l], g_q_dil[l],
                   g_k_dil[l], w_br_fox[l], w_br_dil[l], w_out[l], g_ffn[l], w_up[l], w_conv[l],
                   b_conv[l], w_down[l])
    return x
```

```python
import functools

import numpy as np
import jax
import jax.numpy as jnp
from jax import lax
from jax.experimental import pallas as pl
from jax.experimental.pallas import tpu as pltpu

HEAD_DIM = 128
N_HEADS_FOX = 8
N_HEADS_DIL = 8
DIL_PATTERNS = ((128, 1), (512, 4), (2048, 16))
CONV_WIDTH = 3
PROJ_ROW_PARTS = 4
PROJ_X_PARTS = 4
FFN_ROW_PARTS = 2
EPS = 1e-6
NEG_INF = -1e30
LOG2E = float(np.log2(np.e))
QK_SCALE2 =LOG2E / float(np.sqrt(HEAD_DIM))

F32 = jnp.float32
BF16 = jnp.bfloat16

LANES = 128
SUBLANES = 8
BF16_ROWS = 2 * SUBLANES
VMEM_LIMIT = 56 * 1024 * 1024
BIG_VMEM_LIMIT = 62 * 1024 * 1024

NT_DIMS = (((1,), (1,)), ((), ()))


def _params(sem, vmem_limit=VMEM_LIMIT):
    return pltpu.CompilerParams(dimension_semantics=sem, vmem_limit_bytes=vmem_limit)


def _in_proj_kernel(x_ref, g_ref, w_ref, wf_ref, gqk_ref, qkv_ref, gate_ref, f_ref, h_scr,
                    *, n_qkv_tiles, tiles_per_group, heads_per_tile):
    s = pl.program_id(1)
    j = s - (PROJ_X_PARTS - 1)
    x_rows = x_ref.shape[0]

    def project(h, w_t):
        return lax.dot_general(h, w_t.astype(BF16), NT_DIMS, preferred_element_type=F32)

    @pl.when(s < PROJ_X_PARTS)
    def _():
        rows = pl.ds(pl.multiple_of(s * x_rows, x_rows), x_rows)
        x = x_ref[...]
        ms = jnp.mean(x * x, axis=-1, keepdims=True)
        h = (x * lax.rsqrt(ms + EPS) * g_ref[...]).astype(BF16)
        h_scr[rows, :] = h
        wf = wf_ref[...]
        wf = jnp.concatenate([wf, jnp.zeros((LANES - wf.shape[0], wf.shape[1]), F32)], axis=0)
        f_ref[rows, :] = project(h, wf)

    def store_qk(acc, rows):
        for hh in range(heads_per_tile):
            sl = slice(hh * HEAD_DIM, (hh + 1) * HEAD_DIM)
            a = acc[:, sl]
            ms = jnp.mean(a * a, axis=-1, keepdims=True)
            qkv_ref[rows, sl] = (a * lax.rsqrt(ms + EPS) * gqk_ref[:, sl]).astype(BF16)

    def store_v(acc, rows):
        qkv_ref[rows, :] = acc.astype(BF16)

    def store_gate(acc, rows):
        gate_ref[rows, :] = acc.astype(gate_ref.dtype)

    def tile(store):
        w = w_ref[...].astype(BF16)
        part = h_scr.shape[0] // PROJ_ROW_PARTS
        for r0 in range(0, h_scr.shape[0], part):
            rows = slice(r0, r0 + part)
            store(lax.dot_general(h_scr[rows, :], w, NT_DIMS, preferred_element_type=F32), rows)

    active = j >= 0
    is_qkv = jnp.logical_and(active, j < n_qkv_tiles)
    is_qk = jnp.logical_and(is_qkv, (j % (3 * tiles_per_group)) < 2 * tiles_per_group)
    cases = (
        (is_qk, store_qk),
        (jnp.logical_and(is_qkv, jnp.logical_not(is_qk)), store_v),
        (jnp.logical_and(active, j >= n_qkv_tiles), store_gate),
    )
    for cond, store in cases:
        pl.when(cond)(functools.partial(tile, store))


def _in_proj(x2, g_attn, w_t, g_qk, *, f_lo, f_hi, n_qkv, n_gate):
    m, d = x2.shape
    tm, tn = 2048, 1024
    assert f_lo % tn == 0 and f_hi % SUBLANES == 0
    n_qkv_tiles = n_qkv // tn
    n_tiles = (n_qkv + n_gate) // tn
    kern = functools.partial(
        _in_proj_kernel, n_qkv_tiles=n_qkv_tiles,
        tiles_per_group=(N_HEADS_FOX * HEAD_DIM) // tn, heads_per_tile=tn // HEAD_DIM)
    tile = lambda s: jnp.maximum(s - (PROJ_X_PARTS - 1), 0)
    qkv_tile = lambda s: jnp.minimum(tile(s), n_qkv_tiles - 1)
    w_row = lambda j: pl.multiple_of(j * tn + jnp.where(j * tn >= f_lo, f_hi - f_lo, 0), SUBLANES)
    return pl.pallas_call(
        kern,
        grid=(m // tm, n_tiles + PROJ_X_PARTS - 1),
        in_specs=[
            pl.BlockSpec((tm // PROJ_X_PARTS, d),
                         lambda i, s: (PROJ_X_PARTS * i + jnp.minimum(s, PROJ_X_PARTS - 1), 0)),
            pl.BlockSpec((1, d), lambda i, s: (0, 0)),
            pl.BlockSpec((pl.Element(tn), pl.Element(d)), lambda i, s: (w_row(tile(s)), 0)),
            pl.BlockSpec((pl.Element(f_hi - f_lo), pl.Element(d)), lambda i, s: (f_lo, 0)),
            pl.BlockSpec((1, tn), lambda i, s: (0, qkv_tile(s))),
        ],
        out_specs=[
            pl.BlockSpec((tm, tn), lambda i, s: (i, qkv_tile(s))),
            pl.BlockSpec((tm, tn), lambda i, s: (i, jnp.maximum(tile(s) - n_qkv_tiles, 0))),
            pl.BlockSpec((tm, LANES), lambda i, s: (i, 0)),
        ],
        out_shape=[
            jax.ShapeDtypeStruct((m, n_qkv), BF16),
            jax.ShapeDtypeStruct((m, n_gate), BF16),
            jax.ShapeDtypeStruct((m, LANES), F32),
        ],
        scratch_shapes=[pltpu.VMEM((tm, d), BF16)],
        compiler_params=_params(("arbitrary", "arbitrary"), BIG_VMEM_LIMIT),
        name="in_proj",
    )(x2, g_attn, w_t, w_t, g_qk)


def _forget_kernel(f_ref, b_ref, ccol_ref, crow_ref, *, n_heads):
    z = f_ref[0] + b_ref[...]
    c = jax.nn.log_sigmoid(z)
    s = c.shape[0]
    row = lax.broadcasted_iota(jnp.int32, c.shape, 0)
    k = 1
    while k < s:
        c = c + jnp.where(row >= k, pltpu.roll(c, k, axis=0), 0.0)
        k *= 2
    ccol_ref[0] = c
    crow_ref[0] = c.T[:n_heads, :]


def _forget(f_pre, b_pad, *, n_heads):
    b, s, _ = f_pre.shape
    return pl.pallas_call(
        functools.partial(_forget_kernel, n_heads=n_heads),
        grid=(b,),
        in_specs=[
            pl.BlockSpec((1, s, LANES), lambda i: (i, 0, 0)),
            pl.BlockSpec((1, LANES), lambda i: (0, 0)),
        ],
        out_specs=[
            pl.BlockSpec((1, s, LANES), lambda i: (i, 0, 0)),
            pl.BlockSpec((1, n_heads, s), lambda i: (i, 0, 0)),
        ],
        out_shape=[
            jax.ShapeDtypeStruct((b, s, LANES), F32),
            jax.ShapeDtypeStruct((b, n_heads, s), F32),
        ],
        compiler_params=_params(("arbitrary",)),
        name="forget_cumsum",
    )(f_pre, b_pad)


def _causal_sweeps(streams, tq):
    s_len = streams[0][0].shape[1]
    n_q = s_len // tq

    def scores(stream, qi):
        q_ref, k_ref = stream[:2]
        q0, kv = qi * tq, (qi + 1) * tq
        return lax.dot_general(q_ref[0, q0:kv, :], k_ref[0, :kv, :], NT_DIMS,
                               preferred_element_type=F32)

    def biased(stream, qi, s):
        add_bias = stream[4]
        tiles = [add_bias(s[:, kj * tq:(kj + 1) * tq], qi, kj) for kj in range(qi + 1)]
        z = jnp.concatenate(tiles, axis=1) if qi else tiles[0]
        return z, jnp.max(z, axis=1, keepdims=True)

    def weights(z, m):
        return jnp.exp2(z - m).astype(BF16)

    def weighted(stream, qi, p):
        v_aug, o_ref = stream[2:4]
        q0, kv = qi * tq, (qi + 1) * tq
        o = jnp.dot(p, v_aug[0:kv, :], preferred_element_type=F32)
        o_ref[0, q0:kv, :] = (o[:, :HEAD_DIM] / o[:, HEAD_DIM:HEAD_DIM + 1]).astype(o_ref.dtype)

    logit = [biased(st, 0, scores(st, 0)) for st in streams]
    for qi in range(n_q):
        last = qi + 1 == n_q
        raw = [None if last else scores(st, qi + 1) for st in streams]
        probs = [weights(*zm) for zm in logit]
        for st, p in zip(streams, probs):
            weighted(st, qi, p)
        logit = [None if last else biased(st, qi + 1, s) for st, s in zip(streams, raw)]


def _fill_v_aug(v_ref, v_aug):
    v_aug[:, :HEAD_DIM] = v_ref[0]
    lane = lax.broadcasted_iota(jnp.int32, (v_aug.shape[0], v_aug.shape[1] - HEAD_DIM), 1)
    v_aug[:, HEAD_DIM:] = jnp.where(lane == 0, 1.0, 0.0).astype(BF16)


def _cast_specs(weights, n_steps, step_index):
    in_specs, out_specs, out_shapes = [], [], []
    for w in weights:
        rows, cols = w.shape[0] // n_steps, w.shape[1]
        assert rows * n_steps == w.shape[0] and rows % BF16_ROWS == 0
        spec = pl.BlockSpec((rows, cols), lambda *idx: (step_index(*idx), 0))
        in_specs.append(spec)
        out_specs.append(spec)
        out_shapes.append(jax.ShapeDtypeStruct(w.shape, BF16))
    return in_specs, out_specs, out_shapes


def _attention_kernel(slopes_ref, qa_ref, ka_ref, va_ref, qb_ref, kb_ref, vb_ref, ccol_ref, crow_ref,
                      *rest, tq):
    n_cast = (len(rest) - 5) // 2
    oa_ref, ob_ref = rest[n_cast], rest[n_cast + 1]
    bias_scr, va_aug, vb_aug = rest[-3:]
    _fill_v_aug(va_ref, va_aug)
    _fill_v_aug(vb_ref, vb_aug)
    h = pl.program_id(0)
    r = lax.broadcasted_iota(jnp.int32, (tq, tq), 0)
    c = lax.broadcasted_iota(jnp.int32, (tq, tq), 1)

    @pl.when(pl.program_id(1) == 0)
    def _():
        slope = slopes_ref[h]
        for d in range(bias_scr.shape[0]):
            delta = r - c + d * tq
            count = jnp.zeros((tq, tq), F32)
            for window, dil in DIL_PATTERNS:
                assert dil & (dil - 1) == 0
                member = jnp.logical_and((delta & (dil - 1)) == 0, delta <= window)
                count = count + jnp.where(member, 1.0, 0.0)
            valid = jnp.logical_and(delta >= 0, count > 0.0)
            bias = jnp.log(jnp.maximum(count, 1.0)) - slope * delta.astype(F32)
            bias_scr[d] = jnp.where(valid, bias, NEG_INF) * LOG2E

    s_len = qa_ref.shape[1]
    lane = lax.broadcasted_iota(jnp.int32, (s_len, LANES), 1)
    cq = jnp.sum(jnp.where(lane == h, ccol_ref[0], 0.0), axis=1, keepdims=True) * LOG2E
    ck = crow_ref[0, 0] * LOG2E
    causal = c <= r

    def forget_bias(tile, qi, kj):
        z = tile + cq[qi * tq:(qi + 1) * tq] - ck[:, kj * tq:(kj + 1) * tq]
        return jnp.where(causal, z, NEG_INF) if kj == qi else z

    def distance_bias(tile, qi, kj):
        return tile + bias_scr[qi - kj]

    _causal_sweeps([(qa_ref, ka_ref, va_aug, oa_ref, forget_bias),
                    (qb_ref, kb_ref, vb_aug, ob_ref, distance_bias)], tq)

    for i_ref, o_ref in zip(rest[:n_cast], rest[n_cast + 2:-3]):
        o_ref[...] = i_ref[...].astype(BF16)


def _attention(qkv, ccol, crow4, slopes, cast_weights, *, n_heads, tq):
    b, s, _ = qkv.shape
    spec = lambda group: pl.BlockSpec((1, s, HEAD_DIM),
                                      lambda h, bi, sl: (bi, 0, group * n_heads + h))
    out_spec = pl.BlockSpec((1, s, HEAD_DIM), lambda h, bi, sl: (bi, 0, h))
    c_in, c_out, c_shapes = _cast_specs(cast_weights, b * n_heads, lambda h, bi, sl: h * b + bi)
    o_shape = jax.ShapeDtypeStruct((b, s, n_heads * HEAD_DIM), BF16)
    return pl.pallas_call(
        functools.partial(_attention_kernel, tq=tq),
        grid_spec=pltpu.PrefetchScalarGridSpec(
            num_scalar_prefetch=1,
            grid=(n_heads, b),
            in_specs=[spec(g) for g in range(6)] + [
                pl.BlockSpec((1, s, LANES), lambda h, bi, sl: (bi, 0, 0)),
                pl.BlockSpec((1, 1, 1, s), lambda h, bi, sl: (bi, h, 0, 0)),
            ] + c_in,
            out_specs=[out_spec, out_spec] + c_out,
            scratch_shapes=[pltpu.VMEM((s // tq, tq, tq), F32),
                            pltpu.VMEM((s, 2 * HEAD_DIM), BF16),
                            pltpu.VMEM((s, 2 * HEAD_DIM), BF16)],
        ),
        out_shape=[o_shape, o_shape] + c_shapes,
        compiler_params=_params(("arbitrary", "arbitrary")),
        name="attention",
    )(slopes, *([qkv] * 6), ccol, crow4, *cast_weights)


def _mix_kernel(oa_ref, ob_ref, ga_ref, gb_ref, x_ref, wa_ref, wb_ref, wo_ref, g_ref,
                x1_ref, h2_ref):
    ta = jnp.dot(oa_ref[...], wa_ref[...], preferred_element_type=F32)
    tb = jnp.dot(ob_ref[...], wb_ref[...], preferred_element_type=F32)
    merged = (jax.nn.sigmoid(ga_ref[...].astype(F32)) * ta
              + jax.nn.sigmoid(gb_ref[...].astype(F32)) * tb).astype(BF16)
    x1 = x_ref[...] + jnp.dot(merged, wo_ref[...], preferred_element_type=F32)
    x1_ref[...] = x1
    ms = jnp.mean(x1 * x1, axis=-1, keepdims=True)
    h2_ref[...] = (x1 * lax.rsqrt(ms + EPS) * g_ref[...]).astype(BF16)


def _mix(oa, ob, gates, x2, w_a, w_b, w_o, g_ffn):
    m, d = x2.shape
    wa_rows, wb_rows = w_a.shape[0], w_b.shape[0]
    tm = 256
    resident = lambda shape: pl.BlockSpec(shape, lambda i: (0, 0), pipeline_mode=pl.Buffered(1))
    return pl.pallas_call(
        _mix_kernel,
        grid=(m // tm,),
        in_specs=[
            pl.BlockSpec((tm, wa_rows), lambda i: (i, 0)),
            pl.BlockSpec((tm, wb_rows), lambda i: (i, 0)),
            pl.BlockSpec((tm, d), lambda i: (i, 0)),
            pl.BlockSpec((tm, d), lambda i: (i, 1)),
            pl.BlockSpec((tm, d), lambda i: (i, 0)),
            resident((wa_rows, d)),
            resident((wb_rows, d)),
            resident((d, d)),
            pl.BlockSpec((1, d), lambda i: (0, 0)),
        ],
        out_specs=[
            pl.BlockSpec((tm, d), lambda i: (i, 0)),
            pl.BlockSpec((tm, d), lambda i: (i, 0)),
        ],
        out_shape=[
            jax.ShapeDtypeStruct((m, d), F32),
            jax.ShapeDtypeStruct((m, d), BF16),
        ],
        compiler_params=_params(("arbitrary",)),
        name="branch_mix_out_proj",
    )(oa, ob, gates, gates, x2, w_a, w_b, w_o, g_ffn)


def _ffn_kernel(h_ref, x1_ref, wg_ref, wv_ref, cg_ref, cv_ref, bg_ref, bv_ref, wd_ref,
                out_ref, ug_scr, uv_scr, carry_g, carry_v, *, tm, tiles_per_seq):
    i = pl.program_id(0)
    j = pl.program_id(1)
    first = (i % tiles_per_seq) == 0

    @pl.when(jnp.logical_and(i == 0, j == 0))
    def _():
        carry_g[...] = jnp.zeros(carry_g.shape, F32)
        carry_v[...] = jnp.zeros(carry_v.shape, F32)

    @pl.when(j == 0)
    def _():
        out_ref[...] = x1_ref[...]

    rows = tm // FFN_ROW_PARTS

    def up(w_ref, u_scr, r0):
        u_scr[SUBLANES + r0:SUBLANES + r0 + rows, :] = jnp.dot(
            h_ref[r0:r0 + rows, :], w_ref[...], preferred_element_type=F32)

    def conv(c_ref, b_ref, u_scr, r0):
        out = b_ref[...]
        for t in range(CONV_WIDTH):
            shift = CONV_WIDTH - 1 - t
            out = out + c_ref[t:t + 1, :] * u_scr[pl.ds(SUBLANES - shift + r0, rows), :]
        return out

    for u_scr, carry in ((ug_scr, carry_g), (uv_scr, carry_v)):
        prev = carry[j]
        u_scr[0:SUBLANES, :] = jnp.where(first, jnp.zeros_like(prev), prev)
    for r0 in range(0, tm, rows):
        up(wg_ref, ug_scr, r0)
        up(wv_ref, uv_scr, r0)
    carry_g[j] = ug_scr[tm:tm + SUBLANES, :]
    carry_v[j] = uv_scr[tm:tm + SUBLANES, :]
    for r0 in range(0, tm, rows):
        gate = conv(cg_ref, bg_ref, ug_scr, r0)
        val = conv(cv_ref, bv_ref, uv_scr, r0)
        a = (gate * jax.nn.sigmoid(gate) * val).astype(BF16)
        out_ref[r0:r0 + rows, :] += jnp.dot(a, wd_ref[...], preferred_element_type=F32)


def _ffn(h2, x1, w_up, w_conv, b_conv, w_down, *, seq_len):
    m, d = h2.shape
    d_ff = w_down.shape[0]
    tm, tf = 1024, 512
    nj = d_ff // tf
    kern = functools.partial(_ffn_kernel, tm=tm, tiles_per_seq=seq_len // tm)
    return pl.pallas_call(
        kern,
        grid=(m // tm, nj),
        in_specs=[
            pl.BlockSpec((tm, d), lambda i, j: (i, 0)),
            pl.BlockSpec((tm, d), lambda i, j: (i, 0)),
            pl.BlockSpec((d, tf), lambda i, j: (0, j)),
            pl.BlockSpec((d, tf), lambda i, j: (0, nj + j)),
            pl.BlockSpec((CONV_WIDTH, tf), lambda i, j: (0, j)),
            pl.BlockSpec((CONV_WIDTH, tf), lambda i, j: (0, nj + j)),
            pl.BlockSpec((1, tf), lambda i, j: (0, j)),
            pl.BlockSpec((1, tf), lambda i, j: (0, nj + j)),
            pl.BlockSpec((tf, d), lambda i, j: (j, 0)),
        ],
        out_specs=pl.BlockSpec((tm, d), lambda i, j: (i, 0)),
        out_shape=jax.ShapeDtypeStruct((m, d), F32),
        scratch_shapes=[
            pltpu.VMEM((tm + SUBLANES, tf), F32),
            pltpu.VMEM((tm + SUBLANES, tf), F32),
            pltpu.VMEM((nj, SUBLANES, tf), F32),
            pltpu.VMEM((nj, SUBLANES, tf), F32),
        ],
        compiler_params=_params(("arbitrary", "arbitrary"), BIG_VMEM_LIMIT),
        name="conv_ffn",
    )(h2, x1, w_up, w_up, w_conv, w_conv, b_conv, b_conv, w_down)


def _layer(x, g_attn, w_in, b_forget, g_q_fox, g_k_fox, g_q_dil, g_k_dil,
           w_br_fox, w_br_dil, w_out, g_ffn, w_up, w_conv, b_conv, w_down):
    b, s, d = x.shape
    w_fox = N_HEADS_FOX * HEAD_DIM
    w_dil = N_HEADS_DIL * HEAD_DIM
    n_qkv = 3 * w_fox + 3 * w_dil
    f_lo, f_hi = 3 * w_fox, 3 * w_fox + N_HEADS_FOX

    ones = jnp.ones((w_fox,), F32)
    g_qk = jnp.concatenate([g_q_fox.reshape(-1) * QK_SCALE2, g_k_fox.reshape(-1), ones,
                            g_q_dil.reshape(-1) * QK_SCALE2, g_k_dil.reshape(-1), ones]
                           ).reshape(1, n_qkv)
    b_pad = jnp.pad(b_forget, (0, LANES - N_HEADS_FOX)).reshape(1, LANES)
    slopes = jnp.asarray(2.0 ** (-8.0 * np.arange(1, N_HEADS_DIL + 1) / N_HEADS_DIL), dtype=F32)

    x2 = x.reshape(b * s, d)
    qkv, gates, f_pre = _in_proj(x2, g_attn.reshape(1, d), w_in.T, g_qk,
                                 f_lo=f_lo, f_hi=f_hi, n_qkv=n_qkv, n_gate=2 * d)
    qkv = qkv.reshape(b, s, n_qkv)

    ccol, crow = _forget(f_pre.reshape(b, s, LANES), b_pad, n_heads=N_HEADS_FOX)
    crow4 = crow.reshape(b, N_HEADS_FOX, 1, s)
    assert N_HEADS_FOX == N_HEADS_DIL
    o_a, o_b, w_up_b, w_down_b, w_out_b, w_br_fox_b, w_br_dil_b = _attention(
        qkv, ccol, crow4, slopes, [w_up, w_down, w_out, w_br_fox, w_br_dil],
        n_heads=N_HEADS_FOX, tq=256)

    x1, h2 = _mix(o_a.reshape(b * s, w_fox), o_b.reshape(b * s, w_dil), gates, x2,
                  w_br_fox_b, w_br_dil_b, w_out_b, g_ffn.reshape(1, d))
    out = _ffn(h2, x1, w_up_b, w_conv, b_conv.reshape(1, -1), w_down_b, seq_len=s)
    return out.reshape(b, s, d)


def kernel(x, g_attn, w_in, b_forget, g_q_fox, g_k_fox, g_q_dil, g_k_dil, w_br_fox, w_br_dil,
           w_out, g_ffn, w_up, w_conv, b_conv, w_down):
    for l in range(w_in.shape[0]):
        x = _layer(x, g_attn[l], w_in[l], b_forget[l], g_q_fox[l], g_k_fox[l], g_q_dil[l],
                   g_k_dil[l], w_br_fox[l], w_br_dil[l], w_out[l], g_ffn[l], w_up[l], w_conv[l],
                   b_conv[l], w_down[l])
    return x
```

```python
import functools

import numpy as np
import jax
import jax.numpy as jnp
from jax import lax
from jax.experimental import pallas as pl
from jax.experimental.pallas import tpu as pltpu

HEAD_DIM = 128
N_HEADS_FOX = 8
N_HEADS_DIL = 8
DIL_PATTERNS = ((128, 1), (512, 4), (2048, 16))
CONV_WIDTH = 3
PROJ_ROW_PARTS = 4
PROJ_X_PARTS = 4
FFN_ROW_PARTS = 2
EPS = 1e-6
NEG_INF = -1e30
LOG2E = float(np.log2(np.e))
QK_SCALE2 =LOG2E / float(np.sqrt(HEAD_DIM))

F32 = jnp.float32
BF16 = jnp.bfloat16

LANES = 128
SUBLANES = 8
BF16_ROWS = 2 * SUBLANES
VMEM_LIMIT = 56 * 1024 * 1024
BIG_VMEM_LIMIT = 62 * 1024 * 1024

NT_DIMS = (((1,), (1,)), ((), ()))


def _params(sem, vmem_limit=VMEM_LIMIT):
    return pltpu.CompilerParams(dimension_semantics=sem, vmem_limit_bytes=vmem_limit)


def _in_proj_kernel(x_ref, g_ref, w_ref, wf_ref, gqk_ref, qkv_ref, gate_ref, f_ref, h_scr,
                    *, n_qkv_tiles, tiles_per_group, heads_per_tile):
    s = pl.program_id(1)
    j = s - (PROJ_X_PARTS - 1)
    x_rows = x_ref.shape[0]

    def project(h, w_t):
        return lax.dot_general(h, w_t.astype(BF16), NT_DIMS, preferred_element_type=F32)

    @pl.when(s < PROJ_X_PARTS)
    def _():
        rows = pl.ds(pl.multiple_of(s * x_rows, x_rows), x_rows)
        x = x_ref[...]
        ms = jnp.mean(x * x, axis=-1, keepdims=True)
        h = (x * lax.rsqrt(ms + EPS) * g_ref[...]).astype(BF16)
        h_scr[rows, :] = h
        wf = wf_ref[...]
        wf = jnp.concatenate([wf, jnp.zeros((LANES - wf.shape[0], wf.shape[1]), F32)], axis=0)
        f_ref[rows, :] = project(h, wf)

    def store_qk(acc, rows):
        for hh in range(heads_per_tile):
            sl = slice(hh * HEAD_DIM, (hh + 1) * HEAD_DIM)
            a = acc[:, sl]
            ms = jnp.mean(a * a, axis=-1, keepdims=True)
            qkv_ref[rows, sl] = (a * lax.rsqrt(ms + EPS) * gqk_ref[:, sl]).astype(BF16)

    def store_v(acc, rows):
        qkv_ref[rows, :] = acc.astype(BF16)

    def store_gate(acc, rows):
        gate_ref[rows, :] = acc.astype(gate_ref.dtype)

    def tile(store):
        w = w_ref[...].astype(BF16)
        part = h_scr.shape[0] // PROJ_ROW_PARTS
        for r0 in range(0, h_scr.shape[0], part):
            rows = slice(r0, r0 + part)
            store(lax.dot_general(h_scr[rows, :], w, NT_DIMS, preferred_element_type=F32), rows)

    active = j >= 0
    is_qkv = jnp.logical_and(active, j < n_qkv_tiles)
    is_qk = jnp.logical_and(is_qkv, (j % (3 * tiles_per_group)) < 2 * tiles_per_group)
    cases = (
        (is_qk, store_qk),
        (jnp.logical_and(is_qkv, jnp.logical_not(is_qk)), store_v),
        (jnp.logical_and(active, j >= n_qkv_tiles), store_gate),
    )
    for cond, store in cases:
        pl.when(cond)(functools.partial(tile, store))


def _in_proj(x2, g_attn, w_t, g_qk, *, f_lo, f_hi, n_qkv, n_gate):
    m, d = x2.shape
    tm, tn = 2048, 1024
    assert f_lo % tn == 0 and f_hi % SUBLANES == 0
    n_qkv_tiles = n_qkv // tn
    n_tiles = (n_qkv + n_gate) // tn
    kern = functools.partial(
        _in_proj_kernel, n_qkv_tiles=n_qkv_tiles,
        tiles_per_group=(N_HEADS_FOX * HEAD_DIM) // tn, heads_per_tile=tn // HEAD_DIM)
    tile = lambda s: jnp.maximum(s - (PROJ_X_PARTS - 1), 0)
    qkv_tile = lambda s: jnp.minimum(tile(s), n_qkv_tiles - 1)
    w_row = lambda j: pl.multiple_of(j * tn + jnp.where(j * tn >= f_lo, f_hi - f_lo, 0), SUBLANES)
    return pl.pallas_call(
        kern,
        grid=(m // tm, n_tiles + PROJ_X_PARTS - 1),
        in_specs=[
            pl.BlockSpec((tm // PROJ_X_PARTS, d),
                         lambda i, s: (PROJ_X_PARTS * i + jnp.minimum(s, PROJ_X_PARTS - 1), 0)),
            pl.BlockSpec((1, d), lambda i, s: (0, 0)),
            pl.BlockSpec((pl.Element(tn), pl.Element(d)), lambda i, s: (w_row(tile(s)), 0)),
            pl.BlockSpec((pl.Element(f_hi - f_lo), pl.Element(d)), lambda i, s: (f_lo, 0)),
            pl.BlockSpec((1, tn), lambda i, s: (0, qkv_tile(s))),
        ],
        out_specs=[
            pl.BlockSpec((tm, tn), lambda i, s: (i, qkv_tile(s))),
            pl.BlockSpec((tm, tn), lambda i, s: (i, jnp.maximum(tile(s) - n_qkv_tiles, 0))),
            pl.BlockSpec((tm, LANES), lambda i, s: (i, 0)),
        ],
        out_shape=[
            jax.ShapeDtypeStruct((m, n_qkv), BF16),
            jax.ShapeDtypeStruct((m, n_gate), BF16),
            jax.ShapeDtypeStruct((m, LANES), F32),
        ],
        scratch_shapes=[pltpu.VMEM((tm, d), BF16)],
        compiler_params=_params(("arbitrary", "arbitrary"), BIG_VMEM_LIMIT),
        name="in_proj",
    )(x2, g_attn, w_t, w_t, g_qk)


def _forget_kernel(f_ref, b_ref, ccol_ref, crow_ref, *, n_heads):
    z = f_ref[0] + b_ref[...]
    c = jax.nn.log_sigmoid(z)
    s = c.shape[0]
    row = lax.broadcasted_iota(jnp.int32, c.shape, 0)
    k = 1
    while k < s:
        c = c + jnp.where(row >= k, pltpu.roll(c, k, axis=0), 0.0)
        k *= 2
    ccol_ref[0] = c
    crow_ref[0] = c.T[:n_heads, :]


def _forget(f_pre, b_pad, *, n_heads):
    b, s, _ = f_pre.shape
    return pl.pallas_call(
        functools.partial(_forget_kernel, n_heads=n_heads),
        grid=(b,),
        in_specs=[
            pl.BlockSpec((1, s, LANES), lambda i: (i, 0, 0)),
            pl.BlockSpec((1, LANES), lambda i: (0, 0)),
        ],
        out_specs=[
            pl.BlockSpec((1, s, LANES), lambda i: (i, 0, 0)),
            pl.BlockSpec((1, n_heads, s), lambda i: (i, 0, 0)),
        ],
        out_shape=[
            jax.ShapeDtypeStruct((b, s, LANES), F32),
            jax.ShapeDtypeStruct((b, n_heads, s), F32),
        ],
        compiler_params=_params(("arbitrary",)),
        name="forget_cumsum",
    )(f_pre, b_pad)


def _causal_sweeps(streams, tq):
    s_len = streams[0][0].shape[1]
    n_q = s_len // tq

    def scores(stream, qi):
        q_ref, k_ref = stream[:2]
        q0, kv = qi * tq, (qi + 1) * tq
        return lax.dot_general(q_ref[0, q0:kv, :], k_ref[0, :kv, :], NT_DIMS,
                               preferred_element_type=F32)

    def biased(stream, qi, s):
        add_bias = stream[4]
        tiles = [add_bias(s[:, kj * tq:(kj + 1) * tq], qi, kj) for kj in range(qi + 1)]
        z = jnp.concatenate(tiles, axis=1) if qi else tiles[0]
        return z, jnp.max(z, axis=1, keepdims=True)

    def weights(z, m):
        return jnp.exp2(z - m).astype(BF16)

    def weighted(stream, qi, p):
        v_aug, o_ref = stream[2:4]
        q0, kv = qi * tq, (qi + 1) * tq
        o = jnp.dot(p, v_aug[0:kv, :], preferred_element_type=F32)
        o_ref[0, q0:kv, :] = (o[:, :HEAD_DIM] / o[:, HEAD_DIM:HEAD_DIM + 1]).astype(o_ref.dtype)

    logit = [biased(st, 0, scores(st, 0)) for st in streams]
    for qi in range(n_q):
        last = qi + 1 == n_q
        raw = [None if last else scores(st, qi + 1) for st in streams]
        probs = [weights(*zm) for zm in logit]
        for st, p in zip(streams, probs):
            weighted(st, qi, p)
        logit = [None if last else biased(st, qi + 1, s) for st, s in zip(streams, raw)]


def _fill_v_aug(v_ref, v_aug):
    v_aug[:, :HEAD_DIM] = v_ref[0]
    lane = lax.broadcasted_iota(jnp.int32, (v_aug.shape[0], v_aug.shape[1] - HEAD_DIM), 1)
    v_aug[:, HEAD_DIM:] = jnp.where(lane == 0, 1.0, 0.0).astype(BF16)


def _cast_specs(weights, n_steps, step_index):
    in_specs, out_specs, out_shapes = [], [], []
    for w in weights:
        rows, cols = w.shape[0] // n_steps, w.shape[1]
        assert rows * n_steps == w.shape[0] and rows % BF16_ROWS == 0
        spec = pl.BlockSpec((rows, cols), lambda *idx: (step_index(*idx), 0))
        in_specs.append(spec)
        out_specs.append(spec)
        out_shapes.append(jax.ShapeDtypeStruct(w.shape, BF16))
    return in_specs, out_specs, out_shapes


def _attention_kernel(slopes_ref, qa_ref, ka_ref, va_ref, qb_ref, kb_ref, vb_ref, ccol_ref, crow_ref,
                      *rest, tq):
    n_cast = (len(rest) - 5) // 2
    oa_ref, ob_ref = rest[n_cast], rest[n_cast + 1]
    bias_scr, va_aug, vb_aug = rest[-3:]
    _fill_v_aug(va_ref, va_aug)
    _fill_v_aug(vb_ref, vb_aug)
    h = pl.program_id(0)
    r = lax.broadcasted_iota(jnp.int32, (tq, tq), 0)
    c = lax.broadcasted_iota(jnp.int32, (tq, tq), 1)

    @pl.when(pl.program_id(1) == 0)
    def _():
        slope = slopes_ref[h]
        for d in range(bias_scr.shape[0]):
            delta = r - c + d * tq
            count = jnp.zeros((tq, tq), F32)
            for window, dil in DIL_PATTERNS:
                assert dil & (dil - 1) == 0
                member = jnp.logical_and((delta & (dil - 1)) == 0, delta <= window)
                count = count + jnp.where(member, 1.0, 0.0)
            valid = jnp.logical_and(delta >= 0, count > 0.0)
            bias = jnp.log(jnp.maximum(count, 1.0)) - slope * delta.astype(F32)
            bias_scr[d] = jnp.where(valid, bias, NEG_INF) * LOG2E

    s_len = qa_ref.shape[1]
    lane = lax.broadcasted_iota(jnp.int32, (s_len, LANES), 1)
    cq = jnp.sum(jnp.where(lane == h, ccol_ref[0], 0.0), axis=1, keepdims=True) * LOG2E
    ck = crow_ref[0, 0] * LOG2E
    causal = c <= r

    def forget_bias(tile, qi, kj):
        z = tile + cq[qi * tq:(qi + 1) * tq] - ck[:, kj * tq:(kj + 1) * tq]
        return jnp.where(causal, z, NEG_INF) if kj == qi else z

    def distance_bias(tile, qi, kj):
        return tile + bias_scr[qi - kj]

    _causal_sweeps([(qa_ref, ka_ref, va_aug, oa_ref, forget_bias),
                    (qb_ref, kb_ref, vb_aug, ob_ref, distance_bias)], tq)

    for i_ref, o_ref in zip(rest[:n_cast], rest[n_cast + 2:-3]):
        o_ref[...] = i_ref[...].astype(BF16)


def _attention(qkv, ccol, crow4, slopes, cast_weights, *, n_heads, tq):
    b, s, _ = qkv.shape
    spec = lambda group: pl.BlockSpec((1, s, HEAD_DIM),
                                      lambda h, bi, sl: (bi, 0, group * n_heads + h))
    out_spec = pl.BlockSpec((1, s, HEAD_DIM), lambda h, bi, sl: (bi, 0, h))
    c_in, c_out, c_shapes = _cast_specs(cast_weights, b * n_heads, lambda h, bi, sl: h * b + bi)
    o_shape = jax.ShapeDtypeStruct((b, s, n_heads * HEAD_DIM), BF16)
    return pl.pallas_call(
        functools.partial(_attention_kernel, tq=tq),
        grid_spec=pltpu.PrefetchScalarGridSpec(
            num_scalar_prefetch=1,
            grid=(n_heads, b),
            in_specs=[spec(g) for g in range(6)] + [
                pl.BlockSpec((1, s, LANES), lambda h, bi, sl: (bi, 0, 0)),
                pl.BlockSpec((1, 1, 1, s), lambda h, bi, sl: (bi, h, 0, 0)),
            ] + c_in,
            out_specs=[out_spec, out_spec] + c_out,
            scratch_shapes=[pltpu.VMEM((s // tq, tq, tq), F32),
                            pltpu.VMEM((s, 2 * HEAD_DIM), BF16),
                            pltpu.VMEM((s, 2 * HEAD_DIM), BF16)],
        ),
        out_shape=[o_shape, o_shape] + c_shapes,
        compiler_params=_params(("arbitrary", "arbitrary")),
        name="attention",
    )(slopes, *([qkv] * 6), ccol, crow4, *cast_weights)


def _mix_kernel(oa_ref, ob_ref, ga_ref, gb_ref, x_ref, wa_ref, wb_ref, wo_ref, g_ref,
                x1_ref, h2_ref):
    ta = jnp.dot(oa_ref[...], wa_ref[...], preferred_element_type=F32)
    tb = jnp.dot(ob_ref[...], wb_ref[...], preferred_element_type=F32)
    merged = (jax.nn.sigmoid(ga_ref[...].astype(F32)) * ta
              + jax.nn.sigmoid(gb_ref[...].astype(F32)) * tb).astype(BF16)
    x1 = x_ref[...] + jnp.dot(merged, wo_ref[...], preferred_element_type=F32)
    x1_ref[...] = x1
    ms = jnp.mean(x1 * x1, axis=-1, keepdims=True)
    h2_ref[...] = (x1 * lax.rsqrt(ms + EPS) * g_ref[...]).astype(BF16)


def _mix(oa, ob, gates, x2, w_a, w_b, w_o, g_ffn):
    m, d = x2.shape
    wa_rows, wb_rows = w_a.shape[0], w_b.shape[0]
    tm = 512
    resident = lambda shape: pl.BlockSpec(shape, lambda i: (0, 0), pipeline_mode=pl.Buffered(1))
    return pl.pallas_call(
        _mix_kernel,
        grid=(m // tm,),
        in_specs=[
            pl.BlockSpec((tm, wa_rows), lambda i: (i, 0)),
            pl.BlockSpec((tm, wb_rows), lambda i: (i, 0)),
            pl.BlockSpec((tm, d), lambda i: (i, 0)),
            pl.BlockSpec((tm, d), lambda i: (i, 1)),
            pl.BlockSpec((tm, d), lambda i: (i, 0)),
            resident((wa_rows, d)),
            resident((wb_rows, d)),
            resident((d, d)),
            pl.BlockSpec((1, d), lambda i: (0, 0)),
        ],
        out_specs=[
            pl.BlockSpec((tm, d), lambda i: (i, 0)),
            pl.BlockSpec((tm, d), lambda i: (i, 0)),
        ],
        out_shape=[
            jax.ShapeDtypeStruct((m, d), F32),
            jax.ShapeDtypeStruct((m, d), BF16),
        ],
        compiler_params=_params(("arbitrary",)),
        name="branch_mix_out_proj",
    )(oa, ob, gates, gates, x2, w_a, w_b, w_o, g_ffn)


def _ffn_kernel(h_ref, x1_ref, wg_ref, wv_ref, cg_ref, cv_ref, bg_ref, bv_ref, wd_ref,
                out_ref, ug_scr, uv_scr, carry_g, carry_v, *, tm, tiles_per_seq):
    i = pl.program_id(0)
    j = pl.program_id(1)
    first = (i % tiles_per_seq) == 0

    @pl.when(jnp.logical_and(i == 0, j == 0))
    def _():
        carry_g[...] = jnp.zeros(carry_g.shape, F32)
        carry_v[...] = jnp.zeros(carry_v.shape, F32)

    @pl.when(j == 0)
    def _():
        out_ref[...] = x1_ref[...]

    rows = tm // FFN_ROW_PARTS

    def up(w_ref, u_scr, r0):
        u_scr[SUBLANES + r0:SUBLANES + r0 + rows, :] = jnp.dot(
            h_ref[r0:r0 + rows, :], w_ref[...], preferred_element_type=F32)

    def conv(c_ref, b_ref, u_scr, r0):
        out = b_ref[...]
        for t in range(CONV_WIDTH):
            shift = CONV_WIDTH - 1 - t
            out = out + c_ref[t:t + 1, :] * u_scr[pl.ds(SUBLANES - shift + r0, rows), :]
        return out

    for u_scr, carry in ((ug_scr, carry_g), (uv_scr, carry_v)):
        prev = carry[j]
        u_scr[0:SUBLANES, :] = jnp.where(first, jnp.zeros_like(prev), prev)
    for r0 in range(0, tm, rows):
        up(wg_ref, ug_scr, r0)
        up(wv_ref, uv_scr, r0)
    carry_g[j] = ug_scr[tm:tm + SUBLANES, :]
    carry_v[j] = uv_scr[tm:tm + SUBLANES, :]
    for r0 in range(0, tm, rows):
        gate = conv(cg_ref, bg_ref, ug_scr, r0)
        val = conv(cv_ref, bv_ref, uv_scr, r0)
        a = (gate * jax.nn.sigmoid(gate) * val).astype(BF16)
        out_ref[r0:r0 + rows, :] += jnp.dot(a, wd_ref[...], preferred_element_type=F32)


def _ffn(h2, x1, w_up, w_conv, b_conv, w_down, *, seq_len):
    m, d = h2.shape
    d_ff = w_down.shape[0]
    tm, tf = 1024, 512
    nj = d_ff // tf
    kern = functools.partial(_ffn_kernel, tm=tm, tiles_per_seq=seq_len // tm)
    return pl.pallas_call(
        kern,
        grid=(m // tm, nj),
        in_specs=[
            pl.BlockSpec((tm, d), lambda i, j: (i, 0)),
            pl.BlockSpec((tm, d), lambda i, j: (i, 0)),
            pl.BlockSpec((d, tf), lambda i, j: (0, j)),
            pl.BlockSpec((d, tf), lambda i, j: (0, nj + j)),
            pl.BlockSpec((CONV_WIDTH, tf), lambda i, j: (0, j)),
            pl.BlockSpec((CONV_WIDTH, tf), lambda i, j: (0, nj + j)),
            pl.BlockSpec((1, tf), lambda i, j: (0, j)),
            pl.BlockSpec((1, tf), lambda i, j: (0, nj + j)),
            pl.BlockSpec((tf, d), lambda i, j: (j, 0)),
        ],
        out_specs=pl.BlockSpec((tm, d), lambda i, j: (i, 0)),
        out_shape=jax.ShapeDtypeStruct((m, d), F32),
        scratch_shapes=[
            pltpu.VMEM((tm + SUBLANES, tf), F32),
            pltpu.VMEM((tm + SUBLANES, tf), F32),
            pltpu.VMEM((nj, SUBLANES, tf), F32),
            pltpu.VMEM((nj, SUBLANES, tf), F32),
        ],
        compiler_params=_params(("arbitrary", "arbitrary"), BIG_VMEM_LIMIT),
        name="conv_ffn",
    )(h2, x1, w_up, w_up, w_conv, w_conv, b_conv, b_conv, w_down)


def _layer(x, g_attn, w_in, b_forget, g_q_fox, g_k_fox, g_q_dil, g_k_dil,
           w_br_fox, w_br_dil, w_out, g_ffn, w_up, w_conv, b_conv, w_down):
    b, s, d = x.shape
    w_fox = N_HEADS_FOX * HEAD_DIM
    w_dil = N_HEADS_DIL * HEAD_DIM
    n_qkv = 3 * w_fox + 3 * w_dil
    f_lo, f_hi = 3 * w_fox, 3 * w_fox + N_HEADS_FOX

    ones = jnp.ones((w_fox,), F32)
    g_qk = jnp.concatenate([g_q_fox.reshape(-1) * QK_SCALE2, g_k_fox.reshape(-1), ones,
                            g_q_dil.reshape(-1) * QK_SCALE2, g_k_dil.reshape(-1), ones]
                           ).reshape(1, n_qkv)
    b_pad = jnp.pad(b_forget, (0, LANES - N_HEADS_FOX)).reshape(1, LANES)
    slopes = jnp.asarray(2.0 ** (-8.0 * np.arange(1, N_HEADS_DIL + 1) / N_HEADS_DIL), dtype=F32)

    x2 = x.reshape(b * s, d)
    qkv, gates, f_pre = _in_proj(x2, g_attn.reshape(1, d), w_in.T, g_qk,
                                 f_lo=f_lo, f_hi=f_hi, n_qkv=n_qkv, n_gate=2 * d)
    qkv = qkv.reshape(b, s, n_qkv)

    ccol, crow = _forget(f_pre.reshape(b, s, LANES), b_pad, n_heads=N_HEADS_FOX)
    crow4 = crow.reshape(b, N_HEADS_FOX, 1, s)
    assert N_HEADS_FOX == N_HEADS_DIL
    o_a, o_b, w_up_b, w_down_b, w_out_b, w_br_fox_b, w_br_dil_b = _attention(
        qkv, ccol, crow4, slopes, [w_up, w_down, w_out, w_br_fox, w_br_dil],
        n_heads=N_HEADS_FOX, tq=256)

    x1, h2 = _mix(o_a.reshape(b * s, w_fox), o_b.reshape(b * s, w_dil), gates, x2,
                  w_br_fox_b, w_br_dil_b, w_out_b, g_ffn.reshape(1, d))
    out = _ffn(h2, x1, w_up_b, w_conv, b_conv.reshape(1, -1), w_down_b, seq_len=s)
    return out.reshape(b, s, d)


def kernel(x, g_attn, w_in, b_forget, g_q_fox, g_k_fox, g_q_dil, g_k_dil, w_br_fox, w_br_dil,
           w_out, g_ffn, w_up, w_conv, b_conv, w_down):
    for l in range(w_in.shape[0]):
        x = _layer(x, g_attn[l], w_in[l], b_forget[l], g_q_fox[l], g_k_fox[l], g_q_dil[l],
                   g_k_dil[l], w_br_fox[l], w_br_dil[l], w_out[l], g_ffn[l], w_up[l], w_conv[l],
                   b_conv[l], w_down[l])
    return x
```

```python
import functools

import numpy as np
import jax
import jax.numpy as jnp
from jax import lax
from jax.experimental import pallas as pl
from jax.experimental.pallas import tpu as pltpu

HEAD_DIM = 128
N_HEADS_FOX = 8
N_HEADS_DIL = 8
DIL_PATTERNS = ((128, 1), (512, 4), (2048, 16))
CONV_WIDTH = 3
PROJ_ROW_PARTS = 8
PROJ_X_PARTS = 8
FFN_ROW_PARTS = 2
EPS = 1e-6
NEG_INF = -1e30
LOG2E = float(np.log2(np.e))
QK_SCALE2 =LOG2E / float(np.sqrt(HEAD_DIM))

F32 = jnp.float32
BF16 = jnp.bfloat16

LANES = 128
SUBLANES = 8
BF16_ROWS = 2 * SUBLANES
VMEM_LIMIT = 56 * 1024 * 1024
BIG_VMEM_LIMIT = 62 * 1024 * 1024

NT_DIMS = (((1,), (1,)), ((), ()))


def _params(sem, vmem_limit=VMEM_LIMIT):
    return pltpu.CompilerParams(dimension_semantics=sem, vmem_limit_bytes=vmem_limit)


def _in_proj_kernel(x_ref, g_ref, w_ref, wf_ref, gqk_ref, qkv_ref, gate_ref, f_ref, h_scr,
                    *, n_qkv_tiles, tiles_per_group, heads_per_tile):
    s = pl.program_id(1)
    j = s - (PROJ_X_PARTS - 1)
    x_rows = x_ref.shape[0]

    def project(h, w_t):
        return lax.dot_general(h, w_t.astype(BF16), NT_DIMS, preferred_element_type=F32)

    @pl.when(s < PROJ_X_PARTS)
    def _():
        rows = pl.ds(pl.multiple_of(s * x_rows, x_rows), x_rows)
        x = x_ref[...]
        ms = jnp.mean(x * x, axis=-1, keepdims=True)
        h = (x * lax.rsqrt(ms + EPS) * g_ref[...]).astype(BF16)
        h_scr[rows, :] = h
        wf = wf_ref[...]
        wf = jnp.concatenate([wf, jnp.zeros((LANES - wf.shape[0], wf.shape[1]), F32)], axis=0)
        f_ref[rows, :] = project(h, wf)

    def store_qk(acc, rows):
        for hh in range(heads_per_tile):
            sl = slice(hh * HEAD_DIM, (hh + 1) * HEAD_DIM)
            a = acc[:, sl]
            ms = jnp.mean(a * a, axis=-1, keepdims=True)
            qkv_ref[rows, sl] = (a * lax.rsqrt(ms + EPS) * gqk_ref[:, sl]).astype(BF16)

    def store_v(acc, rows):
        qkv_ref[rows, :] = acc.astype(BF16)

    def store_gate(acc, rows):
        gate_ref[rows, :] = acc.astype(gate_ref.dtype)

    def tile(store):
        w = w_ref[...].astype(BF16)
        part = h_scr.shape[0] // PROJ_ROW_PARTS
        for r0 in range(0, h_scr.shape[0], part):
            rows = slice(r0, r0 + part)
            store(lax.dot_general(h_scr[rows, :], w, NT_DIMS, preferred_element_type=F32), rows)

    active = j >= 0
    is_qkv = jnp.logical_and(active, j < n_qkv_tiles)
    is_qk = jnp.logical_and(is_qkv, (j % (3 * tiles_per_group)) < 2 * tiles_per_group)
    cases = (
        (is_qk, store_qk),
        (jnp.logical_and(is_qkv, jnp.logical_not(is_qk)), store_v),
        (jnp.logical_and(active, j >= n_qkv_tiles), store_gate),
    )
    for cond, store in cases:
        pl.when(cond)(functools.partial(tile, store))


def _in_proj(x2, g_attn, w_t, g_qk, *, f_lo, f_hi, n_qkv, n_gate):
    m, d = x2.shape
    tm, tn = 4096, 512
    assert f_lo % tn == 0 and f_hi % SUBLANES == 0
    n_qkv_tiles = n_qkv // tn
    n_tiles = (n_qkv + n_gate) // tn
    kern = functools.partial(
        _in_proj_kernel, n_qkv_tiles=n_qkv_tiles,
        tiles_per_group=(N_HEADS_FOX * HEAD_DIM) // tn, heads_per_tile=tn // HEAD_DIM)
    tile = lambda s: jnp.maximum(s - (PROJ_X_PARTS - 1), 0)
    qkv_tile = lambda s: jnp.minimum(tile(s), n_qkv_tiles - 1)
    w_row = lambda j: pl.multiple_of(j * tn + jnp.where(j * tn >= f_lo, f_hi - f_lo, 0), SUBLANES)
    return pl.pallas_call(
        kern,
        grid=(m // tm, n_tiles + PROJ_X_PARTS - 1),
        in_specs=[
            pl.BlockSpec((tm // PROJ_X_PARTS, d),
                         lambda i, s: (PROJ_X_PARTS * i + jnp.minimum(s, PROJ_X_PARTS - 1), 0)),
            pl.BlockSpec((1, d), lambda i, s: (0, 0)),
            pl.BlockSpec((pl.Element(tn), pl.Element(d)), lambda i, s: (w_row(tile(s)), 0)),
            pl.BlockSpec((pl.Element(f_hi - f_lo), pl.Element(d)), lambda i, s: (f_lo, 0)),
            pl.BlockSpec((1, tn), lambda i, s: (0, qkv_tile(s))),
        ],
        out_specs=[
            pl.BlockSpec((tm, tn), lambda i, s: (i, qkv_tile(s))),
            pl.BlockSpec((tm, tn), lambda i, s: (i, jnp.maximum(tile(s) - n_qkv_tiles, 0))),
            pl.BlockSpec((tm, LANES), lambda i, s: (i, 0)),
        ],
        out_shape=[
            jax.ShapeDtypeStruct((m, n_qkv), BF16),
            jax.ShapeDtypeStruct((m, n_gate), BF16),
            jax.ShapeDtypeStruct((m, LANES), F32),
        ],
        scratch_shapes=[pltpu.VMEM((tm, d), BF16)],
        compiler_params=_params(("arbitrary", "arbitrary"), BIG_VMEM_LIMIT),
        name="in_proj",
    )(x2, g_attn, w_t, w_t, g_qk)


def _forget_kernel(f_ref, b_ref, ccol_ref, crow_ref, *, n_heads):
    z = f_ref[0] + b_ref[...]
    c = jax.nn.log_sigmoid(z)
    s = c.shape[0]
    row = lax.broadcasted_iota(jnp.int32, c.shape, 0)
    k = 1
    while k < s:
        c = c + jnp.where(row >= k, pltpu.roll(c, k, axis=0), 0.0)
        k *= 2
    ccol_ref[0] = c
    crow_ref[0] = c.T[:n_heads, :]


def _forget(f_pre, b_pad, *, n_heads):
    b, s, _ = f_pre.shape
    return pl.pallas_call(
        functools.partial(_forget_kernel, n_heads=n_heads),
        grid=(b,),
        in_specs=[
            pl.BlockSpec((1, s, LANES), lambda i: (i, 0, 0)),
            pl.BlockSpec((1, LANES), lambda i: (0, 0)),
        ],
        out_specs=[
            pl.BlockSpec((1, s, LANES), lambda i: (i, 0, 0)),
            pl.BlockSpec((1, n_heads, s), lambda i: (i, 0, 0)),
        ],
        out_shape=[
            jax.ShapeDtypeStruct((b, s, LANES), F32),
            jax.ShapeDtypeStruct((b, n_heads, s), F32),
        ],
        compiler_params=_params(("arbitrary",)),
        name="forget_cumsum",
    )(f_pre, b_pad)


def _causal_sweeps(streams, tq):
    s_len = streams[0][0].shape[1]
    n_q = s_len // tq

    def scores(stream, qi):
        q_ref, k_ref = stream[:2]
        q0, kv = qi * tq, (qi + 1) * tq
        return lax.dot_general(q_ref[0, q0:kv, :], k_ref[0, :kv, :], NT_DIMS,
                               preferred_element_type=F32)

    def biased(stream, qi, s):
        add_bias = stream[4]
        tiles = [add_bias(s[:, kj * tq:(kj + 1) * tq], qi, kj) for kj in range(qi + 1)]
        z = jnp.concatenate(tiles, axis=1) if qi else tiles[0]
        return z, jnp.max(z, axis=1, keepdims=True)

    def weights(z, m):
        return jnp.exp2(z - m).astype(BF16)

    def weighted(stream, qi, p):
        v_aug, o_ref = stream[2:4]
        q0, kv = qi * tq, (qi + 1) * tq
        o = jnp.dot(p, v_aug[0:kv, :], preferred_element_type=F32)
        o_ref[0, q0:kv, :] = (o[:, :HEAD_DIM] / o[:, HEAD_DIM:HEAD_DIM + 1]).astype(o_ref.dtype)

    logit = [biased(st, 0, scores(st, 0)) for st in streams]
    for qi in range(n_q):
        last = qi + 1 == n_q
        raw = [None if last else scores(st, qi + 1) for st in streams]
        probs = [weights(*zm) for zm in logit]
        for st, p in zip(streams, probs):
            weighted(st, qi, p)
        logit = [None if last else biased(st, qi + 1, s) for st, s in zip(streams, raw)]


def _fill_v_aug(v_ref, v_aug):
    v_aug[:, :HEAD_DIM] = v_ref[0]
    lane = lax.broadcasted_iota(jnp.int32, (v_aug.shape[0], v_aug.shape[1] - HEAD_DIM), 1)
    v_aug[:, HEAD_DIM:] = jnp.where(lane == 0, 1.0, 0.0).astype(BF16)


def _cast_specs(weights, n_steps, step_index):
    in_specs, out_specs, out_shapes = [], [], []
    for w in weights:
        rows, cols = w.shape[0] // n_steps, w.shape[1]
        assert rows * n_steps == w.shape[0] and rows % BF16_ROWS == 0
        spec = pl.BlockSpec((rows, cols), lambda *idx: (step_index(*idx), 0))
        in_specs.append(spec)
        out_specs.append(spec)
        out_shapes.append(jax.ShapeDtypeStruct(w.shape, BF16))
    return in_specs, out_specs, out_shapes


def _attention_kernel(slopes_ref, qa_ref, ka_ref, va_ref, qb_ref, kb_ref, vb_ref, ccol_ref, crow_ref,
                      *rest, tq):
    n_cast = (len(rest) - 5) // 2
    oa_ref, ob_ref = rest[n_cast], rest[n_cast + 1]
    bias_scr, va_aug, vb_aug = rest[-3:]
    _fill_v_aug(va_ref, va_aug)
    _fill_v_aug(vb_ref, vb_aug)
    h = pl.program_id(0)
    r = lax.broadcasted_iota(jnp.int32, (tq, tq), 0)
    c = lax.broadcasted_iota(jnp.int32, (tq, tq), 1)

    @pl.when(pl.program_id(1) == 0)
    def _():
        slope = slopes_ref[h]
        for d in range(bias_scr.shape[0]):
            delta = r - c + d * tq
            count = jnp.zeros((tq, tq), F32)
            for window, dil in DIL_PATTERNS:
                assert dil & (dil - 1) == 0
                member = jnp.logical_and((delta & (dil - 1)) == 0, delta <= window)
                count = count + jnp.where(member, 1.0, 0.0)
            valid = jnp.logical_and(delta >= 0, count > 0.0)
            bias = jnp.log(jnp.maximum(count, 1.0)) - slope * delta.astype(F32)
            bias_scr[d] = jnp.where(valid, bias, NEG_INF) * LOG2E

    s_len = qa_ref.shape[1]
    lane = lax.broadcasted_iota(jnp.int32, (s_len, LANES), 1)
    cq = jnp.sum(jnp.where(lane == h, ccol_ref[0], 0.0), axis=1, keepdims=True) * LOG2E
    ck = crow_ref[0, 0] * LOG2E
    causal = c <= r

    def forget_bias(tile, qi, kj):
        z = tile + cq[qi * tq:(qi + 1) * tq] - ck[:, kj * tq:(kj + 1) * tq]
        return jnp.where(causal, z, NEG_INF) if kj == qi else z

    def distance_bias(tile, qi, kj):
        return tile + bias_scr[qi - kj]

    _causal_sweeps([(qa_ref, ka_ref, va_aug, oa_ref, forget_bias),
                    (qb_ref, kb_ref, vb_aug, ob_ref, distance_bias)], tq)

    for i_ref, o_ref in zip(rest[:n_cast], rest[n_cast + 2:-3]):
        o_ref[...] = i_ref[...].astype(BF16)


def _attention(qkv, ccol, crow4, slopes, cast_weights, *, n_heads, tq):
    b, s, _ = qkv.shape
    spec = lambda group: pl.BlockSpec((1, s, HEAD_DIM),
                                      lambda h, bi, sl: (bi, 0, group * n_heads + h))
    out_spec = pl.BlockSpec((1, s, HEAD_DIM), lambda h, bi, sl: (bi, 0, h))
    c_in, c_out, c_shapes = _cast_specs(cast_weights, b * n_heads, lambda h, bi, sl: h * b + bi)
    o_shape = jax.ShapeDtypeStruct((b, s, n_heads * HEAD_DIM), BF16)
    return pl.pallas_call(
        functools.partial(_attention_kernel, tq=tq),
        grid_spec=pltpu.PrefetchScalarGridSpec(
            num_scalar_prefetch=1,
            grid=(n_heads, b),
            in_specs=[spec(g) for g in range(6)] + [
                pl.BlockSpec((1, s, LANES), lambda h, bi, sl: (bi, 0, 0)),
                pl.BlockSpec((1, 1, 1, s), lambda h, bi, sl: (bi, h, 0, 0)),
            ] + c_in,
            out_specs=[out_spec, out_spec] + c_out,
            scratch_shapes=[pltpu.VMEM((s // tq, tq, tq), F32),
                            pltpu.VMEM((s, 2 * HEAD_DIM), BF16),
                            pltpu.VMEM((s, 2 * HEAD_DIM), BF16)],
        ),
        out_shape=[o_shape, o_shape] + c_shapes,
        compiler_params=_params(("arbitrary", "arbitrary")),
        name="attention",
    )(slopes, *([qkv] * 6), ccol, crow4, *cast_weights)


def _mix_kernel(oa_ref, ob_ref, ga_ref, gb_ref, x_ref, wa_ref, wb_ref, wo_ref, g_ref,
                x1_ref, h2_ref):
    ta = jnp.dot(oa_ref[...], wa_ref[...], preferred_element_type=F32)
    tb = jnp.dot(ob_ref[...], wb_ref[...], preferred_element_type=F32)
    merged = (jax.nn.sigmoid(ga_ref[...].astype(F32)) * ta
              + jax.nn.sigmoid(gb_ref[...].astype(F32)) * tb).astype(BF16)
    x1 = x_ref[...] + jnp.dot(merged, wo_ref[...], preferred_element_type=F32)
    x1_ref[...] = x1
    ms = jnp.mean(x1 * x1, axis=-1, keepdims=True)
    h2_ref[...] = (x1 * lax.rsqrt(ms + EPS) * g_ref[...]).astype(BF16)


def _mix(oa, ob, gates, x2, w_a, w_b, w_o, g_ffn):
    m, d = x2.shape
    wa_rows, wb_rows = w_a.shape[0], w_b.shape[0]
    tm = 512
    resident = lambda shape: pl.BlockSpec(shape, lambda i: (0, 0), pipeline_mode=pl.Buffered(1))
    return pl.pallas_call(
        _mix_kernel,
        grid=(m // tm,),
        in_specs=[
            pl.BlockSpec((tm, wa_rows), lambda i: (i, 0)),
            pl.BlockSpec((tm, wb_rows), lambda i: (i, 0)),
            pl.BlockSpec((tm, d), lambda i: (i, 0)),
            pl.BlockSpec((tm, d), lambda i: (i, 1)),
            pl.BlockSpec((tm, d), lambda i: (i, 0)),
            resident((wa_rows, d)),
            resident((wb_rows, d)),
            resident((d, d)),
            pl.BlockSpec((1, d), lambda i: (0, 0)),
        ],
        out_specs=[
            pl.BlockSpec((tm, d), lambda i: (i, 0)),
            pl.BlockSpec((tm, d), lambda i: (i, 0)),
        ],
        out_shape=[
            jax.ShapeDtypeStruct((m, d), F32),
            jax.ShapeDtypeStruct((m, d), BF16),
        ],
        compiler_params=_params(("arbitrary",)),
        name="branch_mix_out_proj",
    )(oa, ob, gates, gates, x2, w_a, w_b, w_o, g_ffn)


def _ffn_kernel(h_ref, x1_ref, wg_ref, wv_ref, cg_ref, cv_ref, bg_ref, bv_ref, wd_ref,
                out_ref, ug_scr, uv_scr, carry_g, carry_v, *, tm, tiles_per_seq):
    i = pl.program_id(0)
    j = pl.program_id(1)
    first = (i % tiles_per_seq) == 0

    @pl.when(jnp.logical_and(i == 0, j == 0))
    def _():
        carry_g[...] = jnp.zeros(carry_g.shape, F32)
        carry_v[...] = jnp.zeros(carry_v.shape, F32)

    @pl.when(j == 0)
    def _():
        out_ref[...] = x1_ref[...]

    rows = tm // FFN_ROW_PARTS

    def up(w_ref, u_scr, r0):
        u_scr[SUBLANES + r0:SUBLANES + r0 + rows, :] = jnp.dot(
            h_ref[r0:r0 + rows, :], w_ref[...], preferred_element_type=F32)

    def conv(c_ref, b_ref, u_scr, r0):
        out = b_ref[...]
        for t in range(CONV_WIDTH):
            shift = CONV_WIDTH - 1 - t
            out = out + c_ref[t:t + 1, :] * u_scr[pl.ds(SUBLANES - shift + r0, rows), :]
        return out

    for u_scr, carry in ((ug_scr, carry_g), (uv_scr, carry_v)):
        prev = carry[j]
        u_scr[0:SUBLANES, :] = jnp.where(first, jnp.zeros_like(prev), prev)
    for r0 in range(0, tm, rows):
        up(wg_ref, ug_scr, r0)
        up(wv_ref, uv_scr, r0)
    carry_g[j] = ug_scr[tm:tm + SUBLANES, :]
    carry_v[j] = uv_scr[tm:tm + SUBLANES, :]
    for r0 in range(0, tm, rows):
        gate = conv(cg_ref, bg_ref, ug_scr, r0)
        val = conv(cv_ref, bv_ref, uv_scr, r0)
        a = (gate * jax.nn.sigmoid(gate) * val).astype(BF16)
        out_ref[r0:r0 + rows, :] += jnp.dot(a, wd_ref[...], preferred_element_type=F32)


def _ffn(h2, x1, w_up, w_conv, b_conv, w_down, *, seq_len):
    m, d = h2.shape
    d_ff = w_down.shape[0]
    tm, tf = 1024, 512
    nj = d_ff // tf
    kern = functools.partial(_ffn_kernel, tm=tm, tiles_per_seq=seq_len // tm)
    return pl.pallas_call(
        kern,
        grid=(m // tm, nj),
        in_specs=[
            pl.BlockSpec((tm, d), lambda i, j: (i, 0)),
            pl.BlockSpec((tm, d), lambda i, j: (i, 0)),
            pl.BlockSpec((d, tf), lambda i, j: (0, j)),
            pl.BlockSpec((d, tf), lambda i, j: (0, nj + j)),
            pl.BlockSpec((CONV_WIDTH, tf), lambda i, j: (0, j)),
            pl.BlockSpec((CONV_WIDTH, tf), lambda i, j: (0, nj + j)),
            pl.BlockSpec((1, tf), lambda i, j: (0, j)),
            pl.BlockSpec((1, tf), lambda i, j: (0, nj + j)),
            pl.BlockSpec((tf, d), lambda i, j: (j, 0)),
        ],
        out_specs=pl.BlockSpec((tm, d), lambda i, j: (i, 0)),
        out_shape=jax.ShapeDtypeStruct((m, d), F32),
        scratch_shapes=[
            pltpu.VMEM((tm + SUBLANES, tf), F32),
            pltpu.VMEM((tm + SUBLANES, tf), F32),
            pltpu.VMEM((nj, SUBLANES, tf), F32),
            pltpu.VMEM((nj, SUBLANES, tf), F32),
        ],
        compiler_params=_params(("arbitrary", "arbitrary"), BIG_VMEM_LIMIT),
        name="conv_ffn",
    )(h2, x1, w_up, w_up, w_conv, w_conv, b_conv, b_conv, w_down)


def _layer(x, g_attn, w_in, b_forget, g_q_fox, g_k_fox, g_q_dil, g_k_dil,
           w_br_fox, w_br_dil, w_out, g_ffn, w_up, w_conv, b_conv, w_down):
    b, s, d = x.shape
    w_fox = N_HEADS_FOX * HEAD_DIM
    w_dil = N_HEADS_DIL * HEAD_DIM
    n_qkv = 3 * w_fox + 3 * w_dil
    f_lo, f_hi = 3 * w_fox, 3 * w_fox + N_HEADS_FOX

    ones = jnp.ones((w_fox,), F32)
    g_qk = jnp.concatenate([g_q_fox.reshape(-1) * QK_SCALE2, g_k_fox.reshape(-1), ones,
                            g_q_dil.reshape(-1) * QK_SCALE2, g_k_dil.reshape(-1), ones]
                           ).reshape(1, n_qkv)
    b_pad = jnp.pad(b_forget, (0, LANES - N_HEADS_FOX)).reshape(1, LANES)
    slopes = jnp.asarray(2.0 ** (-8.0 * np.arange(1, N_HEADS_DIL + 1) / N_HEADS_DIL), dtype=F32)

    x2 = x.reshape(b * s, d)
    qkv, gates, f_pre = _in_proj(x2, g_attn.reshape(1, d), w_in.T, g_qk,
                                 f_lo=f_lo, f_hi=f_hi, n_qkv=n_qkv, n_gate=2 * d)
    qkv = qkv.reshape(b, s, n_qkv)

    ccol, crow = _forget(f_pre.reshape(b, s, LANES), b_pad, n_heads=N_HEADS_FOX)
    crow4 = crow.reshape(b, N_HEADS_FOX, 1, s)
    assert N_HEADS_FOX == N_HEADS_DIL
    o_a, o_b, w_up_b, w_down_b, w_out_b, w_br_fox_b, w_br_dil_b = _attention(
        qkv, ccol, crow4, slopes, [w_up, w_down, w_out, w_br_fox, w_br_dil],
        n_heads=N_HEADS_FOX, tq=256)

    x1, h2 = _mix(o_a.reshape(b * s, w_fox), o_b.reshape(b * s, w_dil), gates, x2,
                  w_br_fox_b, w_br_dil_b, w_out_b, g_ffn.reshape(1, d))
    out = _ffn(h2, x1, w_up_b, w_conv, b_conv.reshape(1, -1), w_down_b, seq_len=s)
    return out.reshape(b, s, d)


def kernel(x, g_attn, w_in, b_forget, g_q_fox, g_k_fox, g_q_dil, g_k_dil, w_br_fox, w_br_dil,
           w_out, g_ffn, w_up, w_conv, b_conv, w_down):
    for l in range(w_in.shape[0]):
        x = _layer(x, g_attn[l], w_in[l], b_forget[l], g_q_fox[l], g_k_fox[l], g_q_dil[l],
                   g_k_dil[l], w_br_fox[l], w_br_dil[l], w_out[l], g_ffn[l], w_up[l], w_conv[l],
                   b_conv[l], w_down[l])
    return x
```

```python
import functools

import numpy as np
import jax
import jax.numpy as jnp
from jax import lax
from jax.experimental import pallas as pl
from jax.experimental.pallas import tpu as pltpu

HEAD_DIM = 128
N_HEADS_FOX = 8
N_HEADS_DIL = 8
DIL_PATTERNS = ((128, 1), (512, 4), (2048, 16))
CONV_WIDTH = 3
PROJ_ROW_PARTS = 8
PROJ_X_PARTS = 8
FFN_ROW_PARTS = 2
EPS = 1e-6
NEG_INF = -1e30
LOG2E = float(np.log2(np.e))
QK_SCALE2 = LOG2E / float(np.sqrt(HEAD_DIM))

F32 = jnp.float32
BF16 = jnp.bfloat16

LANES = 128
SUBLANES = 8
BF16_ROWS = 2 * SUBLANES
VMEM_LIMIT = 56 * 1024 * 1024
BIG_VMEM_LIMIT = 62 * 1024 * 1024

NT_DIMS = (((1,), (1,)), ((), ()))


def _params(sem, vmem_limit=VMEM_LIMIT):
    return pltpu.CompilerParams(dimension_semantics=sem, vmem_limit_bytes=vmem_limit)


def _in_proj_kernel(x_ref, g_ref, w_ref, wf_ref, gqk_ref, qkv_ref, gate_ref, f_ref, h_scr,
                    *, n_qkv_tiles, tiles_per_group, heads_per_tile):
    s = pl.program_id(1)
    j = s - (PROJ_X_PARTS - 1)
    x_rows = x_ref.shape[0]

    def project(h, w_t):
        return lax.dot_general(h, w_t.astype(BF16), NT_DIMS, preferred_element_type=F32)

    @pl.when(s < PROJ_X_PARTS)
    def _():
        rows = pl.ds(pl.multiple_of(s * x_rows, x_rows), x_rows)
        x = x_ref[...]
        ms = jnp.mean(x * x, axis=-1, keepdims=True)
        h = (x * lax.rsqrt(ms + EPS) * g_ref[...]).astype(BF16)
        h_scr[rows, :] = h
        wf = wf_ref[...]
        wf = jnp.concatenate([wf, jnp.zeros((LANES - wf.shape[0], wf.shape[1]), F32)], axis=0)
        f_ref[rows, :] = project(h, wf)

    def store_qk(acc, rows):
        for hh in range(heads_per_tile):
            sl = slice(hh * HEAD_DIM, (hh + 1) * HEAD_DIM)
            a = acc[:, sl]
            ms = jnp.mean(a * a, axis=-1, keepdims=True)
            qkv_ref[rows, sl] = (a * lax.rsqrt(ms + EPS) * gqk_ref[:, sl]).astype(BF16)

    def store_v(acc, rows):
        qkv_ref[rows, :] = acc.astype(BF16)

    def store_gate(acc, rows):
        gate_ref[rows, :] = acc.astype(gate_ref.dtype)

    def tile(store):
        w = w_ref[...].astype(BF16)
        part = h_scr.shape[0] // PROJ_ROW_PARTS
        for r0 in range(0, h_scr.shape[0], part):
            rows = slice(r0, r0 + part)
            store(lax.dot_general(h_scr[rows, :], w, NT_DIMS, preferred_element_type=F32), rows)

    active = j >= 0
    is_qkv = jnp.logical_and(active, j < n_qkv_tiles)
    is_qk = jnp.logical_and(is_qkv, (j % (3 * tiles_per_group)) < 2 * tiles_per_group)
    cases = (
        (is_qk, store_qk),
        (jnp.logical_and(is_qkv, jnp.logical_not(is_qk)), store_v),
        (jnp.logical_and(active, j >= n_qkv_tiles), store_gate),
    )
    for cond, store in cases:
        pl.when(cond)(functools.partial(tile, store))


def _in_proj(x2, g_attn, w_t, g_qk, *, f_lo, f_hi, n_qkv, n_gate):
    m, d = x2.shape
    tm, tn = 4096, 512
    assert f_lo % tn == 0 and f_hi % SUBLANES == 0
    n_qkv_tiles = n_qkv // tn
    n_tiles = (n_qkv + n_gate) // tn
    kern = functools.partial(
        _in_proj_kernel, n_qkv_tiles=n_qkv_tiles,
        tiles_per_group=(N_HEADS_FOX * HEAD_DIM) // tn, heads_per_tile=tn // HEAD_DIM)
    tile = lambda s: jnp.maximum(s - (PROJ_X_PARTS - 1), 0)
    qkv_tile = lambda s: jnp.minimum(tile(s), n_qkv_tiles - 1)
    w_row = lambda j: pl.multiple_of(j * tn + jnp.where(j * tn >= f_lo, f_hi - f_lo, 0), SUBLANES)
    return pl.pallas_call(
        kern,
        grid=(m // tm, n_tiles + PROJ_X_PARTS - 1),
        in_specs=[
            pl.BlockSpec((tm // PROJ_X_PARTS, d),
                         lambda i, s: (PROJ_X_PARTS * i + jnp.minimum(s, PROJ_X_PARTS - 1), 0)),
            pl.BlockSpec((1, d), lambda i, s: (0, 0)),
            pl.BlockSpec((pl.Element(tn), pl.Element(d)), lambda i, s: (w_row(tile(s)), 0)),
            pl.BlockSpec((pl.Element(f_hi - f_lo), pl.Element(d)), lambda i, s: (f_lo, 0)),
            pl.BlockSpec((1, tn), lambda i, s: (0, qkv_tile(s))),
        ],
        out_specs=[
            pl.BlockSpec((tm, tn), lambda i, s: (i, qkv_tile(s))),
            pl.BlockSpec((tm, tn), lambda i, s: (i, jnp.maximum(tile(s) - n_qkv_tiles, 0))),
            pl.BlockSpec((tm, LANES), lambda i, s: (i, 0)),
        ],
        out_shape=[
            jax.ShapeDtypeStruct((m, n_qkv), BF16),
            jax.ShapeDtypeStruct((m, n_gate), BF16),
            jax.ShapeDtypeStruct((m, LANES), F32),
        ],
        scratch_shapes=[pltpu.VMEM((tm, d), BF16)],
        compiler_params=_params(("arbitrary", "arbitrary"), BIG_VMEM_LIMIT),
        name="in_proj",
    )(x2, g_attn, w_t, w_t, g_qk)


def _forget_kernel(f_ref, b_ref, ccol_ref, crow_ref, *, n_heads):
    z = f_ref[0] + b_ref[...]
    c = jax.nn.log_sigmoid(z)
    s = c.shape[0]
    row = lax.broadcasted_iota(jnp.int32, c.shape, 0)
    k = 1
    while k < s:
        c = c + jnp.where(row >= k, pltpu.roll(c, k, axis=0), 0.0)
        k *= 2
    ccol_ref[0] = c
    crow_ref[0] = c.T[:n_heads, :]


def _forget(f_pre, b_pad, *, n_heads):
    b, s, _ = f_pre.shape
    return pl.pallas_call(
        functools.partial(_forget_kernel, n_heads=n_heads),
        grid=(b,),
        in_specs=[
            pl.BlockSpec((1, s, LANES), lambda i: (i, 0, 0)),
            pl.BlockSpec((1, LANES), lambda i: (0, 0)),
        ],
        out_specs=[
            pl.BlockSpec((1, s, LANES), lambda i: (i, 0, 0)),
            pl.BlockSpec((1, n_heads, s), lambda i: (i, 0, 0)),
        ],
        out_shape=[
            jax.ShapeDtypeStruct((b, s, LANES), F32),
            jax.ShapeDtypeStruct((b, n_heads, s), F32),
        ],
        compiler_params=_params(("arbitrary",)),
        name="forget_cumsum",
    )(f_pre, b_pad)


def _causal_sweeps(streams, tq):
    s_len = streams[0][0].shape[1]
    n_q = s_len // tq

    def scores(stream, qi):
        q_ref, k_ref = stream[:2]
        q0, kv = qi * tq, (qi + 1) * tq
        return lax.dot_general(q_ref[0, q0:kv, :], k_ref[0, :kv, :], NT_DIMS,
                               preferred_element_type=F32)

    def biased(stream, qi, s):
        add_bias = stream[4]
        tiles = [add_bias(s[:, kj * tq:(kj + 1) * tq], qi, kj) for kj in range(qi + 1)]
        z = jnp.concatenate(tiles, axis=1) if qi else tiles[0]
        return z, jnp.max(z, axis=1, keepdims=True)

    def weights(z, m):
        return jnp.exp2(z - m).astype(BF16)

    def weighted(stream, qi, p):
        v_aug, o_ref = stream[2:4]
        q0, kv = qi * tq, (qi + 1) * tq
        o = jnp.dot(p, v_aug[0:kv, :], preferred_element_type=F32)
        o_ref[0, q0:kv, :] = (o[:, :HEAD_DIM] / o[:, HEAD_DIM:HEAD_DIM + 1]).astype(o_ref.dtype)

    logit = [biased(st, 0, scores(st, 0)) for st in streams]
    for qi in range(n_q):
        last = qi + 1 == n_q
        raw = [None if last else scores(st, qi + 1) for st in streams]
        probs = [weights(*zm) for zm in logit]
        for st, p in zip(streams, probs):
            weighted(st, qi, p)
        logit = [None if last else biased(st, qi + 1, s) for st, s in zip(streams, raw)]


def _fill_v_aug(v_ref, v_aug):
    v_aug[:, :HEAD_DIM] = v_ref[0]
    lane = lax.broadcasted_iota(jnp.int32, (v_aug.shape[0], v_aug.shape[1] - HEAD_DIM), 1)
    v_aug[:, HEAD_DIM:] = jnp.where(lane == 0, 1.0, 0.0).astype(BF16)


def _cast_specs(weights, n_steps, step_index):
    in_specs, out_specs, out_shapes = [], [], []
    for w in weights:
        rows, cols = w.shape[0] // n_steps, w.shape[1]
        assert rows * n_steps == w.shape[0] and rows % BF16_ROWS == 0
        spec = pl.BlockSpec((rows, cols), lambda *idx: (step_index(*idx), 0))
        in_specs.append(spec)
        out_specs.append(spec)
        out_shapes.append(jax.ShapeDtypeStruct(w.shape, BF16))
    return in_specs, out_specs, out_shapes


def _attention_kernel(slopes_ref, qa_ref, ka_ref, va_ref, qb_ref, kb_ref, vb_ref, ccol_ref, crow_ref,
                      *rest, tq):
    n_cast = (len(rest) - 5) // 2
    oa_ref, ob_ref = rest[n_cast], rest[n_cast + 1]
    bias_scr, va_aug, vb_aug = rest[-3:]
    _fill_v_aug(va_ref, va_aug)
    _fill_v_aug(vb_ref, vb_aug)
    h = pl.program_id(0)
    r = lax.broadcasted_iota(jnp.int32, (tq, tq), 0)
    c = lax.broadcasted_iota(jnp.int32, (tq, tq), 1)

    @pl.when(pl.program_id(1) == 0)
    def _():
        slope = slopes_ref[h]
        for d in range(bias_scr.shape[0]):
            delta = r - c + d * tq
            count = jnp.zeros((tq, tq), F32)
            for window, dil in DIL_PATTERNS:
                assert dil & (dil - 1) == 0
                member = jnp.logical_and((delta & (dil - 1)) == 0, delta <= window)
                count = count + jnp.where(member, 1.0, 0.0)
            valid = jnp.logical_and(delta >= 0, count > 0.0)
            bias = jnp.log(jnp.maximum(count, 1.0)) - slope * delta.astype(F32)
            bias_scr[d] = jnp.where(valid, bias, NEG_INF) * LOG2E

    s_len = qa_ref.shape[1]
    lane = lax.broadcasted_iota(jnp.int32, (s_len, LANES), 1)
    cq = jnp.sum(jnp.where(lane == h, ccol_ref[0], 0.0), axis=1, keepdims=True) * LOG2E
    ck = crow_ref[0, 0] * LOG2E
    causal = c <= r

    def forget_bias(tile, qi, kj):
        z = tile + cq[qi * tq:(qi + 1) * tq] - ck[:, kj * tq:(kj + 1) * tq]
        return jnp.where(causal, z, NEG_INF) if kj == qi else z

    def distance_bias(tile, qi, kj):
        return tile + bias_scr[qi - kj]

    _causal_sweeps([(qa_ref, ka_ref, va_aug, oa_ref, forget_bias),
                    (qb_ref, kb_ref, vb_aug, ob_ref, distance_bias)], tq)

    for i_ref, o_ref in zip(rest[:n_cast], rest[n_cast + 2:-3]):
        o_ref[...] = i_ref[...].astype(BF16)


def _attention(qkv, ccol, crow4, slopes, cast_weights, *, n_heads, tq):
    b, s, _ = qkv.shape
    spec = lambda group: pl.BlockSpec((1, s, HEAD_DIM),
                                      lambda h, bi, sl: (bi, 0, group * n_heads + h))
    out_spec = pl.BlockSpec((1, s, HEAD_DIM), lambda h, bi, sl: (bi, 0, h))
    c_in, c_out, c_shapes = _cast_specs(cast_weights, b * n_heads, lambda h, bi, sl: h * b + bi)
    o_shape = jax.ShapeDtypeStruct((b, s, n_heads * HEAD_DIM), BF16)
    return pl.pallas_call(
        functools.partial(_attention_kernel, tq=tq),
        grid_spec=pltpu.PrefetchScalarGridSpec(
            num_scalar_prefetch=1,
            grid=(n_heads, b),
            in_specs=[spec(g) for g in range(6)] + [
                pl.BlockSpec((1, s, LANES), lambda h, bi, sl: (bi, 0, 0)),
                pl.BlockSpec((1, 1, 1, s), lambda h, bi, sl: (bi, h, 0, 0)),
            ] + c_in,
            out_specs=[out_spec, out_spec] + c_out,
            scratch_shapes=[pltpu.VMEM((s // tq, tq, tq), F32),
                            pltpu.VMEM((s, 2 * HEAD_DIM), BF16),
                            pltpu.VMEM((s, 2 * HEAD_DIM), BF16)],
        ),
        out_shape=[o_shape, o_shape] + c_shapes,
        compiler_params=_params(("arbitrary", "arbitrary")),
        name="attention",
    )(slopes, *([qkv] * 6), ccol, crow4, *cast_weights)


def _mix_kernel(oa_ref, ob_ref, ga_ref, gb_ref, x_ref, wa_ref, wb_ref, wo_ref, g_ref,
                x1_ref, h2_ref):
    ta = jnp.dot(oa_ref[...], wa_ref[...], preferred_element_type=F32)
    tb = jnp.dot(ob_ref[...], wb_ref[...], preferred_element_type=F32)
    merged = (jax.nn.sigmoid(ga_ref[...].astype(F32)) * ta
              + jax.nn.sigmoid(gb_ref[...].astype(F32)) * tb).astype(BF16)
    x1 = x_ref[...] + jnp.dot(merged, wo_ref[...], preferred_element_type=F32)
    x1_ref[...] = x1
    ms = jnp.mean(x1 * x1, axis=-1, keepdims=True)
    h2_ref[...] = (x1 * lax.rsqrt(ms + EPS) * g_ref[...]).astype(BF16)


def _mix(oa, ob, gates, x2, w_a, w_b, w_o, g_ffn):
    m, d = x2.shape
    wa_rows, wb_rows = w_a.shape[0], w_b.shape[0]
    tm = 512
    resident = lambda shape: pl.BlockSpec(shape, lambda i: (0, 0), pipeline_mode=pl.Buffered(1))
    return pl.pallas_call(
        _mix_kernel,
        grid=(m // tm,),
        in_specs=[
            pl.BlockSpec((tm, wa_rows), lambda i: (i, 0)),
            pl.BlockSpec((tm, wb_rows), lambda i: (i, 0)),
            pl.BlockSpec((tm, d), lambda i: (i, 0)),
            pl.BlockSpec((tm, d), lambda i: (i, 1)),
            pl.BlockSpec((tm, d), lambda i: (i, 0)),
            resident((wa_rows, d)),
            resident((wb_rows, d)),
            resident((d, d)),
            pl.BlockSpec((1, d), lambda i: (0, 0)),
        ],
        out_specs=[
            pl.BlockSpec((tm, d), lambda i: (i, 0)),
            pl.BlockSpec((tm, d), lambda i: (i, 0)),
        ],
        out_shape=[
            jax.ShapeDtypeStruct((m, d), F32),
            jax.ShapeDtypeStruct((m, d), BF16),
        ],
        compiler_params=_params(("arbitrary",)),
        name="branch_mix_out_proj",
    )(oa, ob, gates, gates, x2, w_a, w_b, w_o, g_ffn)


def _ffn_kernel(h_ref, x1_ref, wg_ref, wv_ref, cg_ref, cv_ref, bg_ref, bv_ref, wd_ref,
                out_ref, ug_scr, uv_scr, carry_g, carry_v, *, tm, tiles_per_seq):
    i = pl.program_id(0)
    j = pl.program_id(1)
    first = (i % tiles_per_seq) == 0

    @pl.when(jnp.logical_and(i == 0, j == 0))
    def _():
        carry_g[...] = jnp.zeros(carry_g.shape, F32)
        carry_v[...] = jnp.zeros(carry_v.shape, F32)

    @pl.when(j == 0)
    def _():
        out_ref[...] = x1_ref[...]

    rows = tm // FFN_ROW_PARTS

    def up(w_ref, u_scr, r0):
        u_scr[SUBLANES + r0:SUBLANES + r0 + rows, :] = jnp.dot(
            h_ref[r0:r0 + rows, :], w_ref[...], preferred_element_type=F32)

    def conv(c_ref, b_ref, u_scr, r0):
        out = b_ref[...]
        for t in range(CONV_WIDTH):
            shift = CONV_WIDTH - 1 - t
            out = out + c_ref[t:t + 1, :] * u_scr[pl.ds(SUBLANES - shift + r0, rows), :]
        return out

    for u_scr, carry in ((ug_scr, carry_g), (uv_scr, carry_v)):
        prev = carry[j]
        u_scr[0:SUBLANES, :] = jnp.where(first, jnp.zeros_like(prev), prev)
    for r0 in range(0, tm, rows):
        up(wg_ref, ug_scr, r0)
        up(wv_ref, uv_scr, r0)
    carry_g[j] = ug_scr[tm:tm + SUBLANES, :]
    carry_v[j] = uv_scr[tm:tm + SUBLANES, :]
    for r0 in range(0, tm, rows):
        gate = conv(cg_ref, bg_ref, ug_scr, r0)
        val = conv(cv_ref, bv_ref, uv_scr, r0)
        a = (gate * jax.nn.sigmoid(gate) * val).astype(BF16)
        out_ref[r0:r0 + rows, :] += jnp.dot(a, wd_ref[...], preferred_element_type=F32)


def _ffn(h2, x1, w_up, w_conv, b_conv, w_down, *, seq_len):
    m, d = h2.shape
    d_ff = w_down.shape[0]
    tm, tf = 1024, 512
    nj = d_ff // tf
    kern = functools.partial(_ffn_kernel, tm=tm, tiles_per_seq=seq_len // tm)
    return pl.pallas_call(
        kern,
        grid=(m // tm, nj),
        in_specs=[
            pl.BlockSpec((tm, d), lambda i, j: (i, 0)),
            pl.BlockSpec((tm, d), lambda i, j: (i, 0)),
            pl.BlockSpec((d, tf), lambda i, j: (0, j)),
            pl.BlockSpec((d, tf), lambda i, j: (0, nj + j)),
            pl.BlockSpec((CONV_WIDTH, tf), lambda i, j: (0, j)),
            pl.BlockSpec((CONV_WIDTH, tf), lambda i, j: (0, nj + j)),
            pl.BlockSpec((1, tf), lambda i, j: (0, j)),
            pl.BlockSpec((1, tf), lambda i, j: (0, nj + j)),
            pl.BlockSpec((tf, d), lambda i, j: (j, 0)),
        ],
        out_specs=pl.BlockSpec((tm, d), lambda i, j: (i, 0)),
        out_shape=jax.ShapeDtypeStruct((m, d), F32),
        scratch_shapes=[
            pltpu.VMEM((tm + SUBLANES, tf), F32),
            pltpu.VMEM((tm + SUBLANES, tf), F32),
            pltpu.VMEM((nj, SUBLANES, tf), F32),
            pltpu.VMEM((nj, SUBLANES, tf), F32),
        ],
        compiler_params=_params(("arbitrary", "arbitrary"), BIG_VMEM_LIMIT),
        name="conv_ffn",
    )(h2, x1, w_up, w_up, w_conv, w_conv, b_conv, b_conv, w_down)


def _layer(x, g_attn, w_in, b_forget, g_q_fox, g_k_fox, g_q_dil, g_k_dil,
           w_br_fox, w_br_dil, w_out, g_ffn, w_up, w_conv, b_conv, w_down):
    b, s, d = x.shape
    w_fox = N_HEADS_FOX * HEAD_DIM
    w_dil = N_HEADS_DIL * HEAD_DIM
    n_qkv = 3 * w_fox + 3 * w_dil
    f_lo, f_hi = 3 * w_fox, 3 * w_fox + N_HEADS_FOX

    ones = jnp.ones((w_fox,), F32)
    g_qk = jnp.concatenate([g_q_fox.reshape(-1) * QK_SCALE2, g_k_fox.reshape(-1), ones,
                            g_q_dil.reshape(-1) * QK_SCALE2, g_k_dil.reshape(-1), ones]
                           ).reshape(1, n_qkv)
    b_pad = jnp.pad(b_forget, (0, LANES - N_HEADS_FOX)).reshape(1, LANES)
    slopes = jnp.asarray(2.0 ** (-8.0 * np.arange(1, N_HEADS_DIL + 1) / N_HEADS_DIL), dtype=F32)

    x2 = x.reshape(b * s, d)
    qkv, gates, f_pre = _in_proj(x2, g_attn.reshape(1, d), w_in.T, g_qk,
                                 f_lo=f_lo, f_hi=f_hi, n_qkv=n_qkv, n_gate=2 * d)
    qkv = qkv.reshape(b, s, n_qkv)

    ccol, crow = _forget(f_pre.reshape(b, s, LANES), b_pad, n_heads=N_HEADS_FOX)
    crow4 = crow.reshape(b, N_HEADS_FOX, 1, s)
    assert N_HEADS_FOX == N_HEADS_DIL
    o_a, o_b, w_up_b, w_down_b, w_out_b, w_br_fox_b, w_br_dil_b = _attention(
        qkv, ccol, crow4, slopes, [w_up, w_down, w_out, w_br_fox, w_br_dil],
        n_heads=N_HEADS_FOX, tq=256)

    x1, h2 = _mix(o_a.reshape(b * s, w_fox), o_b.reshape(b * s, w_dil), gates, x2,
                  w_br_fox_b, w_br_dil_b, w_out_b, g_ffn.reshape(1, d))
    out = _ffn(h2, x1, w_up_b, w_conv, b_conv.reshape(1, -1), w_down_b, seq_len=s)
    return out.reshape(b, s, d)


def kernel(x, g_attn, w_in, b_forget, g_q_fox, g_k_fox, g_q_dil, g_k_dil, w_br_fox, w_br_dil,
           w_out, g_ffn, w_up, w_conv, b_conv, w_down):
    for l in range(w_in.shape[0]):
        x = _layer(x, g_attn[l], w_in[l], b_forget[l], g_q_fox[l], g_k_fox[l], g_q_dil[l],
                   g_k_dil[l], w_br_fox[l], w_br_dil[l], w_out[l], g_ffn[l], w_up[l], w_conv[l],
                   b_conv[l], w_down[l])
    return x
```

```python
import functools

import numpy as np
import jax
import jax.numpy as jnp
from jax import lax
from jax.experimental import pallas as pl
from jax.experimental.pallas import tpu as pltpu

HEAD_DIM = 128
N_HEADS_FOX = 8
N_HEADS_DIL = 8
DIL_PATTERNS = ((128, 1), (512, 4), (2048, 16))
CONV_WIDTH = 3
PROJ_ROW_PARTS = 8
PROJ_X_PARTS = 8
FFN_ROW_PARTS = 2
EPS = 1e-6
NEG_INF = -1e30
LOG2E = float(np.log2(np.e))
QK_SCALE2 = LOG2E / float(np.sqrt(HEAD_DIM))

F32 = jnp.float32
BF16 = jnp.bfloat16

LANES = 128
SUBLANES = 8
BF16_ROWS = 2 * SUBLANES
VMEM_LIMIT = 56 * 1024 * 1024
BIG_VMEM_LIMIT = 62 * 1024 * 1024

NT_DIMS = (((1,), (1,)), ((), ()))


def _params(sem, vmem_limit=VMEM_LIMIT):
    return pltpu.CompilerParams(dimension_semantics=sem, vmem_limit_bytes=vmem_limit)


def _in_proj_kernel(x_ref, g_ref, w_ref, wf_ref, gqk_ref, qkv_ref, gate_ref, f_ref, h_scr,
                    *, n_qkv_tiles, tiles_per_group, heads_per_tile):
    s = pl.program_id(1)
    j = s - (PROJ_X_PARTS - 1)
    x_rows = x_ref.shape[0]

    def project(h, w_t):
        return lax.dot_general(h, w_t.astype(BF16), NT_DIMS, preferred_element_type=F32)

    @pl.when(s < PROJ_X_PARTS)
    def _():
        rows = pl.ds(pl.multiple_of(s * x_rows, x_rows), x_rows)
        x = x_ref[...]
        ms = jnp.mean(x * x, axis=-1, keepdims=True)
        h = (x * lax.rsqrt(ms + EPS) * g_ref[...]).astype(BF16)
        h_scr[rows, :] = h
        wf = wf_ref[...]
        wf = jnp.concatenate([wf, jnp.zeros((LANES - wf.shape[0], wf.shape[1]), F32)], axis=0)
        f_ref[rows, :] = project(h, wf)

    def store_qk(acc, rows):
        for hh in range(heads_per_tile):
            sl = slice(hh * HEAD_DIM, (hh + 1) * HEAD_DIM)
            a = acc[:, sl]
            ms = jnp.mean(a * a, axis=-1, keepdims=True)
            qkv_ref[rows, sl] = (a * lax.rsqrt(ms + EPS) * gqk_ref[:, sl]).astype(BF16)

    def store_v(acc, rows):
        qkv_ref[rows, :] = acc.astype(BF16)

    def store_gate(acc, rows):
        gate_ref[rows, :] = acc.astype(gate_ref.dtype)

    def tile(store):
        w = w_ref[...].astype(BF16)
        part = h_scr.shape[0] // PROJ_ROW_PARTS
        for r0 in range(0, h_scr.shape[0], part):
            rows = slice(r0, r0 + part)
            store(lax.dot_general(h_scr[rows, :], w, NT_DIMS, preferred_element_type=F32), rows)

    active = j >= 0
    is_qkv = jnp.logical_and(active, j < n_qkv_tiles)
    is_qk = jnp.logical_and(is_qkv, (j % (3 * tiles_per_group)) < 2 * tiles_per_group)
    cases = (
        (is_qk, store_qk),
        (jnp.logical_and(is_qkv, jnp.logical_not(is_qk)), store_v),
        (jnp.logical_and(active, j >= n_qkv_tiles), store_gate),
    )
    for cond, store in cases:
        pl.when(cond)(functools.partial(tile, store))


def _in_proj(x2, g_attn, w_t, g_qk, *, f_lo, f_hi, n_qkv, n_gate):
    m, d = x2.shape
    tm, tn = 4096, 512
    assert f_lo % tn == 0 and f_hi % SUBLANES == 0
    n_qkv_tiles = n_qkv // tn
    n_tiles = (n_qkv + n_gate) // tn
    kern = functools.partial(
        _in_proj_kernel, n_qkv_tiles=n_qkv_tiles,
        tiles_per_group=(N_HEADS_FOX * HEAD_DIM) // tn, heads_per_tile=tn // HEAD_DIM)
    tile = lambda s: jnp.maximum(s - (PROJ_X_PARTS - 1), 0)
    qkv_tile = lambda s: jnp.minimum(tile(s), n_qkv_tiles - 1)
    w_row = lambda j: pl.multiple_of(j * tn + jnp.where(j * tn >= f_lo, f_hi - f_lo, 0), SUBLANES)
    return pl.pallas_call(
        kern,
        grid=(m // tm, n_tiles + PROJ_X_PARTS - 1),
        in_specs=[
            pl.BlockSpec((tm // PROJ_X_PARTS, d),
                         lambda i, s: (PROJ_X_PARTS * i + jnp.minimum(s, PROJ_X_PARTS - 1), 0)),
            pl.BlockSpec((1, d), lambda i, s: (0, 0)),
            pl.BlockSpec((pl.Element(tn), pl.Element(d)), lambda i, s: (w_row(tile(s)), 0)),
            pl.BlockSpec((pl.Element(f_hi - f_lo), pl.Element(d)), lambda i, s: (f_lo, 0)),
            pl.BlockSpec((1, tn), lambda i, s: (0, qkv_tile(s))),
        ],
        out_specs=[
            pl.BlockSpec((tm, tn), lambda i, s: (i, qkv_tile(s))),
            pl.BlockSpec((tm, tn), lambda i, s: (i, jnp.maximum(tile(s) - n_qkv_tiles, 0))),
            pl.BlockSpec((tm, LANES), lambda i, s: (i, 0)),
        ],
        out_shape=[
            jax.ShapeDtypeStruct((m, n_qkv), BF16),
            jax.ShapeDtypeStruct((m, n_gate), BF16),
            jax.ShapeDtypeStruct((m, LANES), F32),
        ],
        scratch_shapes=[pltpu.VMEM((tm, d), BF16)],
        compiler_params=_params(("arbitrary", "arbitrary"), BIG_VMEM_LIMIT),
        name="in_proj",
    )(x2, g_attn, w_t, w_t, g_qk)


def _forget_kernel(f_ref, b_ref, ccol_ref, crow_ref, *, n_heads):
    z = f_ref[0] + b_ref[...]
    c = jax.nn.log_sigmoid(z)
    s = c.shape[0]
    row = lax.broadcasted_iota(jnp.int32, c.shape, 0)
    k = 1
    while k < s:
        c = c + jnp.where(row >= k, pltpu.roll(c, k, axis=0), 0.0)
        k *= 2
    ccol_ref[0] = c
    crow_ref[0] = c.T[:n_heads, :]


def _forget(f_pre, b_pad, *, n_heads):
    b, s, _ = f_pre.shape
    return pl.pallas_call(
        functools.partial(_forget_kernel, n_heads=n_heads),
        grid=(b,),
        in_specs=[
            pl.BlockSpec((1, s, LANES), lambda i: (i, 0, 0)),
            pl.BlockSpec((1, LANES), lambda i: (0, 0)),
        ],
        out_specs=[
            pl.BlockSpec((1, s, LANES), lambda i: (i, 0, 0)),
            pl.BlockSpec((1, n_heads, s), lambda i: (i, 0, 0)),
        ],
        out_shape=[
            jax.ShapeDtypeStruct((b, s, LANES), F32),
            jax.ShapeDtypeStruct((b, n_heads, s), F32),
        ],
        compiler_params=_params(("arbitrary",)),
        name="forget_cumsum",
    )(f_pre, b_pad)


def _causal_sweeps(streams, tq, prepare_values):
    s_len = streams[0][0].shape[1]
    n_q = s_len // tq

    def scores(stream, qi):
        q_ref, k_ref = stream[:2]
        q0, kv = qi * tq, (qi + 1) * tq
        return lax.dot_general(q_ref[0, q0:kv, :], k_ref[0, :kv, :], NT_DIMS,
                               preferred_element_type=F32)

    def biased(stream, qi, s):
        add_bias = stream[4]
        tiles = [add_bias(s[:, kj * tq:(kj + 1) * tq], qi, kj) for kj in range(qi + 1)]
        z = jnp.concatenate(tiles, axis=1) if qi else tiles[0]
        return z, jnp.max(z, axis=1, keepdims=True)

    def weights(z, m):
        return jnp.exp2(z - m).astype(BF16)

    def weighted(stream, qi, p):
        v_aug, o_ref = stream[2:4]
        q0, kv = qi * tq, (qi + 1) * tq
        o = jnp.dot(p, v_aug[0:kv, :], preferred_element_type=F32)
        o_ref[0, q0:kv, :] = (o[:, :HEAD_DIM] / o[:, HEAD_DIM:HEAD_DIM + 1]).astype(o_ref.dtype)

    logit = [biased(st, 0, scores(st, 0)) for st in streams]
    for qi in range(n_q):
        last = qi + 1 == n_q
        raw = [None if last else scores(st, qi + 1) for st in streams]
        if qi == 0:
            prepare_values()
        probs = [weights(*zm) for zm in logit]
        for st, p in zip(streams, probs):
            weighted(st, qi, p)
        logit = [None if last else biased(st, qi + 1, s) for st, s in zip(streams, raw)]


def _fill_v_aug(v_ref, v_aug):
    v_aug[:, :HEAD_DIM] = v_ref[0]
    lane = lax.broadcasted_iota(jnp.int32, (v_aug.shape[0], v_aug.shape[1] - HEAD_DIM), 1)
    v_aug[:, HEAD_DIM:] = jnp.where(lane == 0, 1.0, 0.0).astype(BF16)


def _cast_specs(weights, n_steps, step_index):
    in_specs, out_specs, out_shapes = [], [], []
    for w in weights:
        rows, cols = w.shape[0] // n_steps, w.shape[1]
        assert rows * n_steps == w.shape[0] and rows % BF16_ROWS == 0
        spec = pl.BlockSpec((rows, cols), lambda *idx: (step_index(*idx), 0))
        in_specs.append(spec)
        out_specs.append(spec)
        out_shapes.append(jax.ShapeDtypeStruct(w.shape, BF16))
    return in_specs, out_specs, out_shapes


def _attention_kernel(slopes_ref, qa_ref, ka_ref, va_ref, qb_ref, kb_ref, vb_ref, ccol_ref, crow_ref,
                      *rest, tq):
    n_cast = (len(rest) - 5) // 2
    oa_ref, ob_ref = rest[n_cast], rest[n_cast + 1]
    bias_scr, va_aug, vb_aug = rest[-3:]
    h = pl.program_id(0)
    r = lax.broadcasted_iota(jnp.int32, (tq, tq), 0)
    c = lax.broadcasted_iota(jnp.int32, (tq, tq), 1)

    @pl.when(pl.program_id(1) == 0)
    def _():
        slope = slopes_ref[h]
        for d in range(bias_scr.shape[0]):
            delta = r - c + d * tq
            count = jnp.zeros((tq, tq), F32)
            for window, dil in DIL_PATTERNS:
                assert dil & (dil - 1) == 0
                member = jnp.logical_and((delta & (dil - 1)) == 0, delta <= window)
                count = count + jnp.where(member, 1.0, 0.0)
            valid = jnp.logical_and(delta >= 0, count > 0.0)
            bias = jnp.log(jnp.maximum(count, 1.0)) - slope * delta.astype(F32)
            bias_scr[d] = jnp.where(valid, bias, NEG_INF) * LOG2E

    s_len = qa_ref.shape[1]
    lane = lax.broadcasted_iota(jnp.int32, (s_len, LANES), 1)
    cq = jnp.sum(jnp.where(lane == h, ccol_ref[0], 0.0), axis=1, keepdims=True) * LOG2E
    ck = crow_ref[0, 0] * LOG2E
    causal = c <= r

    def forget_bias(tile, qi, kj):
        z = tile + cq[qi * tq:(qi + 1) * tq] - ck[:, kj * tq:(kj + 1) * tq]
        return jnp.where(causal, z, NEG_INF) if kj == qi else z

    def distance_bias(tile, qi, kj):
        return tile + bias_scr[qi - kj]

    def fill_values():
        _fill_v_aug(va_ref, va_aug)
        _fill_v_aug(vb_ref, vb_aug)

    _causal_sweeps([(qa_ref, ka_ref, va_aug, oa_ref, forget_bias),
                    (qb_ref, kb_ref, vb_aug, ob_ref, distance_bias)], tq, fill_values)

    for i_ref, o_ref in zip(rest[:n_cast], rest[n_cast + 2:-3]):
        o_ref[...] = i_ref[...].astype(BF16)


def _attention(qkv, ccol, crow4, slopes, cast_weights, *, n_heads, tq):
    b, s, _ = qkv.shape
    spec = lambda group: pl.BlockSpec((1, s, HEAD_DIM),
                                      lambda h, bi, sl: (bi, 0, group * n_heads + h))
    out_spec = pl.BlockSpec((1, s, HEAD_DIM), lambda h, bi, sl: (bi, 0, h))
    c_in, c_out, c_shapes = _cast_specs(cast_weights, b * n_heads, lambda h, bi, sl: h * b + bi)
    o_shape = jax.ShapeDtypeStruct((b, s, n_heads * HEAD_DIM), BF16)
    return pl.pallas_call(
        functools.partial(_attention_kernel, tq=tq),
        grid_spec=pltpu.PrefetchScalarGridSpec(
            num_scalar_prefetch=1,
            grid=(n_heads, b),
            in_specs=[spec(g) for g in range(6)] + [
                pl.BlockSpec((1, s, LANES), lambda h, bi, sl: (bi, 0, 0)),
                pl.BlockSpec((1, 1, 1, s), lambda h, bi, sl: (bi, h, 0, 0)),
            ] + c_in,
            out_specs=[out_spec, out_spec] + c_out,
            scratch_shapes=[pltpu.VMEM((s // tq, tq, tq), F32),
                            pltpu.VMEM((s, 2 * HEAD_DIM), BF16),
                            pltpu.VMEM((s, 2 * HEAD_DIM), BF16)],
        ),
        out_shape=[o_shape, o_shape] + c_shapes,
        compiler_params=_params(("arbitrary", "arbitrary")),
        name="attention",
    )(slopes, *([qkv] * 6), ccol, crow4, *cast_weights)


def _mix_kernel(oa_ref, ob_ref, ga_ref, gb_ref, x_ref, wa_ref, wb_ref, wo_ref, g_ref,
                x1_ref, h2_ref):
    ta = jnp.dot(oa_ref[...], wa_ref[...], preferred_element_type=F32)
    tb = jnp.dot(ob_ref[...], wb_ref[...], preferred_element_type=F32)
    merged = (jax.nn.sigmoid(ga_ref[...].astype(F32)) * ta
              + jax.nn.sigmoid(gb_ref[...].astype(F32)) * tb).astype(BF16)
    x1 = x_ref[...] + jnp.dot(merged, wo_ref[...], preferred_element_type=F32)
    x1_ref[...] = x1
    ms = jnp.mean(x1 * x1, axis=-1, keepdims=True)
    h2_ref[...] = (x1 * lax.rsqrt(ms + EPS) * g_ref[...]).astype(BF16)


def _mix(oa, ob, gates, x2, w_a, w_b, w_o, g_ffn):
    m, d = x2.shape
    wa_rows, wb_rows = w_a.shape[0], w_b.shape[0]
    tm = 512
    resident = lambda shape: pl.BlockSpec(shape, lambda i: (0, 0), pipeline_mode=pl.Buffered(1))
    return pl.pallas_call(
        _mix_kernel,
        grid=(m // tm,),
        in_specs=[
            pl.BlockSpec((tm, wa_rows), lambda i: (i, 0)),
            pl.BlockSpec((tm, wb_rows), lambda i: (i, 0)),
            pl.BlockSpec((tm, d), lambda i: (i, 0)),
            pl.BlockSpec((tm, d), lambda i: (i, 1)),
            pl.BlockSpec((tm, d), lambda i: (i, 0)),
            resident((wa_rows, d)),
            resident((wb_rows, d)),
            resident((d, d)),
            pl.BlockSpec((1, d), lambda i: (0, 0)),
        ],
        out_specs=[
            pl.BlockSpec((tm, d), lambda i: (i, 0)),
            pl.BlockSpec((tm, d), lambda i: (i, 0)),
        ],
        out_shape=[
            jax.ShapeDtypeStruct((m, d), F32),
            jax.ShapeDtypeStruct((m, d), BF16),
        ],
        compiler_params=_params(("arbitrary",)),
        name="branch_mix_out_proj",
    )(oa, ob, gates, gates, x2, w_a, w_b, w_o, g_ffn)


def _ffn_kernel(h_ref, x1_ref, wg_ref, wv_ref, cg_ref, cv_ref, bg_ref, bv_ref, wd_ref,
                out_ref, ug_scr, uv_scr, carry_g, carry_v, *, tm, tiles_per_seq):
    i = pl.program_id(0)
    j = pl.program_id(1)
    first = (i % tiles_per_seq) == 0

    @pl.when(jnp.logical_and(i == 0, j == 0))
    def _():
        carry_g[...] = jnp.zeros(carry_g.shape, F32)
        carry_v[...] = jnp.zeros(carry_v.shape, F32)

    @pl.when(j == 0)
    def _():
        out_ref[...] = x1_ref[...]

    rows = tm // FFN_ROW_PARTS

    def up(w_ref, u_scr, r0):
        u_scr[SUBLANES + r0:SUBLANES + r0 + rows, :] = jnp.dot(
            h_ref[r0:r0 + rows, :], w_ref[...], preferred_element_type=F32)

    def conv(c_ref, b_ref, u_scr, r0):
        out = b_ref[...]
        for t in range(CONV_WIDTH):
            shift = CONV_WIDTH - 1 - t
            out = out + c_ref[t:t + 1, :] * u_scr[pl.ds(SUBLANES - shift + r0, rows), :]
        return out

    for u_scr, carry in ((ug_scr, carry_g), (uv_scr, carry_v)):
        prev = carry[j]
        u_scr[0:SUBLANES, :] = jnp.where(first, jnp.zeros_like(prev), prev)
    for r0 in range(0, tm, rows):
        up(wg_ref, ug_scr, r0)
        up(wv_ref, uv_scr, r0)
    carry_g[j] = ug_scr[tm:tm + SUBLANES, :]
    carry_v[j] = uv_scr[tm:tm + SUBLANES, :]
    for r0 in range(0, tm, rows):
        gate = conv(cg_ref, bg_ref, ug_scr, r0)
        val = conv(cv_ref, bv_ref, uv_scr, r0)
        a = (gate * jax.nn.sigmoid(gate) * val).astype(BF16)
        out_ref[r0:r0 + rows, :] += jnp.dot(a, wd_ref[...], preferred_element_type=F32)


def _ffn(h2, x1, w_up, w_conv, b_conv, w_down, *, seq_len):
    m, d = h2.shape
    d_ff = w_down.shape[0]
    tm, tf = 1024, 512
    nj = d_ff // tf
    kern = functools.partial(_ffn_kernel, tm=tm, tiles_per_seq=seq_len // tm)
    return pl.pallas_call(
        kern,
        grid=(m // tm, nj),
        in_specs=[
            pl.BlockSpec((tm, d), lambda i, j: (i, 0)),
            pl.BlockSpec((tm, d), lambda i, j: (i, 0)),
            pl.BlockSpec((d, tf), lambda i, j: (0, j)),
            pl.BlockSpec((d, tf), lambda i, j: (0, nj + j)),
            pl.BlockSpec((CONV_WIDTH, tf), lambda i, j: (0, j)),
            pl.BlockSpec((CONV_WIDTH, tf), lambda i, j: (0, nj + j)),
            pl.BlockSpec((1, tf), lambda i, j: (0, j)),
            pl.BlockSpec((1, tf), lambda i, j: (0, nj + j)),
            pl.BlockSpec((tf, d), lambda i, j: (j, 0)),
        ],
        out_specs=pl.BlockSpec((tm, d), lambda i, j: (i, 0)),
        out_shape=jax.ShapeDtypeStruct((m, d), F32),
        scratch_shapes=[
            pltpu.VMEM((tm + SUBLANES, tf), F32),
            pltpu.VMEM((tm + SUBLANES, tf), F32),
            pltpu.VMEM((nj, SUBLANES, tf), F32),
            pltpu.VMEM((nj, SUBLANES, tf), F32),
        ],
        compiler_params=_params(("arbitrary", "arbitrary"), BIG_VMEM_LIMIT),
        name="conv_ffn",
    )(h2, x1, w_up, w_up, w_conv, w_conv, b_conv, b_conv, w_down)


def _layer(x, g_attn, w_in, b_forget, g_q_fox, g_k_fox, g_q_dil, g_k_dil,
           w_br_fox, w_br_dil, w_out, g_ffn, w_up, w_conv, b_conv, w_down):
    b, s, d = x.shape
    w_fox = N_HEADS_FOX * HEAD_DIM
    w_dil = N_HEADS_DIL * HEAD_DIM
    n_qkv = 3 * w_fox + 3 * w_dil
    f_lo, f_hi = 3 * w_fox, 3 * w_fox + N_HEADS_FOX

    ones = jnp.ones((w_fox,), F32)
    g_qk = jnp.concatenate([g_q_fox.reshape(-1) * QK_SCALE2, g_k_fox.reshape(-1), ones,
                            g_q_dil.reshape(-1) * QK_SCALE2, g_k_dil.reshape(-1), ones]
                           ).reshape(1, n_qkv)
    b_pad = jnp.pad(b_forget, (0, LANES - N_HEADS_FOX)).reshape(1, LANES)
    slopes = jnp.asarray(2.0 ** (-8.0 * np.arange(1, N_HEADS_DIL + 1) / N_HEADS_DIL), dtype=F32)

    x2 = x.reshape(b * s, d)
    qkv, gates, f_pre = _in_proj(x2, g_attn.reshape(1, d), w_in.T, g_qk,
                                 f_lo=f_lo, f_hi=f_hi, n_qkv=n_qkv, n_gate=2 * d)
    qkv = qkv.reshape(b, s, n_qkv)

    ccol, crow = _forget(f_pre.reshape(b, s, LANES), b_pad, n_heads=N_HEADS_FOX)
    crow4 = crow.reshape(b, N_HEADS_FOX, 1, s)
    assert N_HEADS_FOX == N_HEADS_DIL
    o_a, o_b, w_up_b, w_down_b, w_out_b, w_br_fox_b, w_br_dil_b = _attention(
        qkv, ccol, crow4, slopes, [w_up, w_down, w_out, w_br_fox, w_br_dil],
        n_heads=N_HEADS_FOX, tq=256)

    x1, h2 = _mix(o_a.reshape(b * s, w_fox), o_b.reshape(b * s, w_dil), gates, x2,
                  w_br_fox_b, w_br_dil_b, w_out_b, g_ffn.reshape(1, d))
    out = _ffn(h2, x1, w_up_b, w_conv, b_conv.reshape(1, -1), w_down_b, seq_len=s)
    return out.reshape(b, s, d)


def kernel(x, g_attn, w_in, b_forget, g_q_fox, g_k_fox, g_q_dil, g_k_dil, w_br_fox, w_br_dil,
           w_out, g_ffn, w_up, w_conv, b_conv, w_down):
    for l in range(w_in.shape[0]):
        x = _layer(x, g_attn[l], w_in[l], b_forget[l], g_q_fox[l], g_k_fox[l], g_q_dil[l],
                   g_k_dil[l], w_br_fox[l], w_br_dil[l], w_out[l], g_ffn[l], w_up[l], w_conv[l],
                   b_conv[l], w_down[l])
    return x
```

```python
import functools

import numpy as np
import jax
import jax.numpy as jnp
from jax import lax
from jax.experimental import pallas as pl
from jax.experimental.pallas import tpu as pltpu

HEAD_DIM = 128
N_HEADS_FOX = 8
N_HEADS_DIL = 8
DIL_PATTERNS = ((128, 1), (512, 4), (2048, 16))
CONV_WIDTH = 3
PROJ_ROW_PARTS = 8
PROJ_X_PARTS = 8
FFN_ROW_PARTS = 2
EPS = 1e-6
NEG_INF = -1e30
LOG2E = float(np.log2(np.e))
QK_SCALE2 = LOG2E / float(np.sqrt(HEAD_DIM))

F32 = jnp.float32
BF16 = jnp.bfloat16

LANES = 128
SUBLANES = 8
BF16_ROWS = 2 * SUBLANES
VMEM_LIMIT = 56 * 1024 * 1024
BIG_VMEM_LIMIT = 62 * 1024 * 1024

NT_DIMS = (((1,), (1,)), ((), ()))


def _params(sem, vmem_limit=VMEM_LIMIT):
    return pltpu.CompilerParams(dimension_semantics=sem, vmem_limit_bytes=vmem_limit)


def _in_proj_kernel(x_ref, g_ref, w_ref, wf_ref, gqk_ref, qkv_ref, gate_ref, f_ref, h_scr,
                    *, n_qkv_tiles, tiles_per_group, heads_per_tile):
    s = pl.program_id(1)
    j = s - (PROJ_X_PARTS - 1)
    x_rows = x_ref.shape[0]

    def project(h, w_t):
        return lax.dot_general(h, w_t.astype(BF16), NT_DIMS, preferred_element_type=F32)

    @pl.when(s < PROJ_X_PARTS)
    def _():
        rows = pl.ds(pl.multiple_of(s * x_rows, x_rows), x_rows)
        x = x_ref[...]
        ms = jnp.mean(x * x, axis=-1, keepdims=True)
        h = (x * lax.rsqrt(ms + EPS) * g_ref[...]).astype(BF16)
        h_scr[rows, :] = h
        wf = wf_ref[...]
        wf = jnp.concatenate([wf, jnp.zeros((LANES - wf.shape[0], wf.shape[1]), F32)], axis=0)
        f_ref[rows, :] = project(h, wf)

    def store_qk(acc, rows):
        for hh in range(heads_per_tile):
            sl = slice(hh * HEAD_DIM, (hh + 1) * HEAD_DIM)
            a = acc[:, sl]
            ms = jnp.mean(a * a, axis=-1, keepdims=True)
            qkv_ref[rows, sl] = (a * lax.rsqrt(ms + EPS) * gqk_ref[:, sl]).astype(BF16)

    def store_v(acc, rows):
        qkv_ref[rows, :] = acc.astype(BF16)

    def store_gate(acc, rows):
        gate_ref[rows, :] = acc.astype(gate_ref.dtype)

    def tile(store):
        w = w_ref[...].astype(BF16)
        part = h_scr.shape[0] // PROJ_ROW_PARTS
        for r0 in range(0, h_scr.shape[0], part):
            rows = slice(r0, r0 + part)
            store(lax.dot_general(h_scr[rows, :], w, NT_DIMS, preferred_element_type=F32), rows)

    active = j >= 0
    is_qkv = jnp.logical_and(active, j < n_qkv_tiles)
    is_qk = jnp.logical_and(is_qkv, (j % (3 * tiles_per_group)) < 2 * tiles_per_group)
    cases = (
        (is_qk, store_qk),
        (jnp.logical_and(is_qkv, jnp.logical_not(is_qk)), store_v),
        (jnp.logical_and(active, j >= n_qkv_tiles), store_gate),
    )
    for cond, store in cases:
        pl.when(cond)(functools.partial(tile, store))


def _in_proj(x2, g_attn, w_t, g_qk, *, f_lo, f_hi, n_qkv, n_gate):
    m, d = x2.shape
    tm, tn = 4096, 512
    assert f_lo % tn == 0 and f_hi % SUBLANES == 0
    n_qkv_tiles = n_qkv // tn
    n_tiles = (n_qkv + n_gate) // tn
    kern = functools.partial(
        _in_proj_kernel, n_qkv_tiles=n_qkv_tiles,
        tiles_per_group=(N_HEADS_FOX * HEAD_DIM) // tn, heads_per_tile=tn // HEAD_DIM)
    tile = lambda s: jnp.maximum(s - (PROJ_X_PARTS - 1), 0)
    qkv_tile = lambda s: jnp.minimum(tile(s), n_qkv_tiles - 1)
    w_row = lambda j: pl.multiple_of(j * tn + jnp.where(j * tn >= f_lo, f_hi - f_lo, 0), SUBLANES)
    return pl.pallas_call(
        kern,
        grid=(m // tm, n_tiles + PROJ_X_PARTS - 1),
        in_specs=[
            pl.BlockSpec((tm // PROJ_X_PARTS, d),
                         lambda i, s: (PROJ_X_PARTS * i + jnp.minimum(s, PROJ_X_PARTS - 1), 0)),
            pl.BlockSpec((1, d), lambda i, s: (0, 0)),
            pl.BlockSpec((pl.Element(tn), pl.Element(d)), lambda i, s: (w_row(tile(s)), 0)),
            pl.BlockSpec((pl.Element(f_hi - f_lo), pl.Element(d)), lambda i, s: (f_lo, 0)),
            pl.BlockSpec((1, tn), lambda i, s: (0, qkv_tile(s))),
        ],
        out_specs=[
            pl.BlockSpec((tm, tn), lambda i, s: (i, qkv_tile(s))),
            pl.BlockSpec((tm, tn), lambda i, s: (i, jnp.maximum(tile(s) - n_qkv_tiles, 0))),
            pl.BlockSpec((tm, LANES), lambda i, s: (i, 0)),
        ],
        out_shape=[
            jax.ShapeDtypeStruct((m, n_qkv), BF16),
            jax.ShapeDtypeStruct((m, n_gate), BF16),
            jax.ShapeDtypeStruct((m, LANES), F32),
        ],
        scratch_shapes=[pltpu.VMEM((tm, d), BF16)],
        compiler_params=_params(("arbitrary", "arbitrary"), BIG_VMEM_LIMIT),
        name="in_proj",
    )(x2, g_attn, w_t, w_t, g_qk)


def _forget_kernel(f_ref, b_ref, ccol_ref, crow_ref, *, n_heads):
    z = f_ref[0] + b_ref[...]
    c = jax.nn.log_sigmoid(z)
    s = c.shape[0]
    row = lax.broadcasted_iota(jnp.int32, c.shape, 0)
    k = 1
    while k < s:
        c = c + jnp.where(row >= k, pltpu.roll(c, k, axis=0), 0.0)
        k *= 2
    ccol_ref[0] = c
    crow_ref[0] = c.T[:n_heads, :]


def _forget(f_pre, b_pad, *, n_heads):
    b, s, _ = f_pre.shape
    return pl.pallas_call(
        functools.partial(_forget_kernel, n_heads=n_heads),
        grid=(b,),
        in_specs=[
            pl.BlockSpec((1, s, LANES), lambda i: (i, 0, 0)),
            pl.BlockSpec((1, LANES), lambda i: (0, 0)),
        ],
        out_specs=[
            pl.BlockSpec((1, s, LANES), lambda i: (i, 0, 0)),
            pl.BlockSpec((1, n_heads, s), lambda i: (i, 0, 0)),
        ],
        out_shape=[
            jax.ShapeDtypeStruct((b, s, LANES), F32),
            jax.ShapeDtypeStruct((b, n_heads, s), F32),
        ],
        compiler_params=_params(("arbitrary",)),
        name="forget_cumsum",
    )(f_pre, b_pad)


def _causal_sweeps(streams, tq, prepare_values):
    s_len = streams[0][0].shape[1]
    n_q = s_len // tq

    def scores(stream, qi):
        q_ref, k_ref = stream[:2]
        q0, kv = qi * tq, (qi + 1) * tq
        return lax.dot_general(q_ref[0, q0:kv, :], k_ref[0, :kv, :], NT_DIMS,
                               preferred_element_type=F32)

    def biased(stream, qi, s):
        add_bias = stream[4]
        tiles = [add_bias(s[:, kj * tq:(kj + 1) * tq], qi, kj) for kj in range(qi + 1)]
        z = jnp.concatenate(tiles, axis=1) if qi else tiles[0]
        return z, jnp.max(z, axis=1, keepdims=True)

    def weights(z, m):
        return jnp.exp2(z - m).astype(BF16)

    def weighted(stream, qi, p):
        v_aug, o_ref = stream[2:4]
        q0, kv = qi * tq, (qi + 1) * tq
        o = jnp.dot(p, v_aug[0:kv, :], preferred_element_type=F32)
        o_ref[0, q0:kv, :] = (o[:, :HEAD_DIM] / o[:, HEAD_DIM:HEAD_DIM + 1]).astype(o_ref.dtype)

    logit = [biased(st, 0, scores(st, 0)) for st in streams]
    for qi in range(n_q):
        last = qi + 1 == n_q
        raw = [None if last else scores(st, qi + 1) for st in streams]
        if qi == 0:
            prepare_values()
        probs = [weights(*zm) for zm in logit]
        for st, p in zip(streams, probs):
            weighted(st, qi, p)
        logit = [None if last else biased(st, qi + 1, s) for st, s in zip(streams, raw)]


def _fill_v_aug(v_ref, v_aug):
    v_aug[:, :HEAD_DIM] = v_ref[0]
    lane = lax.broadcasted_iota(jnp.int32, (v_aug.shape[0], v_aug.shape[1] - HEAD_DIM), 1)
    v_aug[:, HEAD_DIM:] = jnp.where(lane == 0, 1.0, 0.0).astype(BF16)


def _cast_specs(weights, n_steps, step_index):
    in_specs, out_specs, out_shapes = [], [], []
    for w in weights:
        rows, cols = w.shape[0] // n_steps, w.shape[1]
        assert rows * n_steps == w.shape[0] and rows % BF16_ROWS == 0
        spec = pl.BlockSpec((rows, cols), lambda *idx: (step_index(*idx), 0))
        in_specs.append(spec)
        out_specs.append(spec)
        out_shapes.append(jax.ShapeDtypeStruct(w.shape, BF16))
    return in_specs, out_specs, out_shapes


def _attention_kernel(slopes_ref, qa_ref, ka_ref, va_ref, qb_ref, kb_ref, vb_ref, ccol_ref, crow_ref,
                      *rest, tq):
    n_cast = (len(rest) - 5) // 2
    oa_ref, ob_ref = rest[n_cast], rest[n_cast + 1]
    bias_scr, va_aug, vb_aug = rest[-3:]
    h = pl.program_id(0)
    r = lax.broadcasted_iota(jnp.int32, (tq, tq), 0)
    c = lax.broadcasted_iota(jnp.int32, (tq, tq), 1)

    @pl.when(pl.program_id(1) == 0)
    def _():
        slope = slopes_ref[h]
        for d in range(bias_scr.shape[0]):
            delta = r - c + d * tq
            count = jnp.zeros((tq, tq), F32)
            for window, dil in DIL_PATTERNS:
                assert dil & (dil - 1) == 0
                member = jnp.logical_and((delta & (dil - 1)) == 0, delta <= window)
                count = count + jnp.where(member, 1.0, 0.0)
            valid = jnp.logical_and(delta >= 0, count > 0.0)
            bias = jnp.log(jnp.maximum(count, 1.0)) - slope * delta.astype(F32)
            bias_scr[d] = jnp.where(valid, bias, NEG_INF) * LOG2E

    s_len = qa_ref.shape[1]
    lane = lax.broadcasted_iota(jnp.int32, (s_len, LANES), 1)
    cq = jnp.sum(jnp.where(lane == h, ccol_ref[0], 0.0), axis=1, keepdims=True) * LOG2E
    ck = crow_ref[0, 0] * LOG2E
    causal = c <= r

    def forget_bias(tile, qi, kj):
        z = tile + cq[qi * tq:(qi + 1) * tq] - ck[:, kj * tq:(kj + 1) * tq]
        return jnp.where(causal, z, NEG_INF) if kj == qi else z

    def distance_bias(tile, qi, kj):
        return tile + bias_scr[qi - kj]

    def fill_values():
        _fill_v_aug(va_ref, va_aug)
        _fill_v_aug(vb_ref, vb_aug)

    _causal_sweeps([(qa_ref, ka_ref, va_aug, oa_ref, forget_bias),
                    (qb_ref, kb_ref, vb_aug, ob_ref, distance_bias)], tq, fill_values)

    for i_ref, o_ref in zip(rest[:n_cast], rest[n_cast + 2:-3]):
        o_ref[...] = i_ref[...].astype(BF16)


def _attention(qkv, ccol, crow4, slopes, cast_weights, *, n_heads, tq):
    b, s, _ = qkv.shape
    spec = lambda group: pl.BlockSpec((1, s, HEAD_DIM),
                                      lambda h, bi, sl: (bi, 0, group * n_heads + h))
    out_spec = pl.BlockSpec((1, s, HEAD_DIM), lambda h, bi, sl: (bi, 0, h))
    c_in, c_out, c_shapes = _cast_specs(cast_weights, b * n_heads, lambda h, bi, sl: h * b + bi)
    o_shape = jax.ShapeDtypeStruct((b, s, n_heads * HEAD_DIM), BF16)
    return pl.pallas_call(
        functools.partial(_attention_kernel, tq=tq),
        grid_spec=pltpu.PrefetchScalarGridSpec(
            num_scalar_prefetch=1,
            grid=(n_heads, b),
            in_specs=[spec(g) for g in range(6)] + [
                pl.BlockSpec((1, s, LANES), lambda h, bi, sl: (bi, 0, 0)),
                pl.BlockSpec((1, 1, 1, s), lambda h, bi, sl: (bi, h, 0, 0)),
            ] + c_in,
            out_specs=[out_spec, out_spec] + c_out,
            scratch_shapes=[pltpu.VMEM((s // tq, tq, tq), F32),
                            pltpu.VMEM((s, 2 * HEAD_DIM), BF16),
                            pltpu.VMEM((s, 2 * HEAD_DIM), BF16)],
        ),
        out_shape=[o_shape, o_shape] + c_shapes,
        compiler_params=_params(("arbitrary", "arbitrary")),
        name="attention",
    )(slopes, *([qkv] * 6), ccol, crow4, *cast_weights)


def _mix_kernel(oa_ref, ob_ref, ga_ref, gb_ref, x_ref, wa_ref, wb_ref, wo_ref, g_ref,
                x1_ref, h2_ref):
    ta = jnp.dot(oa_ref[...], wa_ref[...], preferred_element_type=F32)
    tb = jnp.dot(ob_ref[...], wb_ref[...], preferred_element_type=F32)
    merged = (jax.nn.sigmoid(ga_ref[...].astype(F32)) * ta
              + jax.nn.sigmoid(gb_ref[...].astype(F32)) * tb).astype(BF16)
    x1 = x_ref[...] + jnp.dot(merged, wo_ref[...], preferred_element_type=F32)
    x1_ref[...] = x1
    ms = jnp.mean(x1 * x1, axis=-1, keepdims=True)
    h2_ref[...] = (x1 * lax.rsqrt(ms + EPS) * g_ref[...]).astype(BF16)


def _mix(oa, ob, gates, x2, w_a, w_b, w_o, g_ffn):
    m, d = x2.shape
    wa_rows, wb_rows = w_a.shape[0], w_b.shape[0]
    tm = 512
    resident = lambda shape: pl.BlockSpec(shape, lambda i: (0, 0), pipeline_mode=pl.Buffered(1))
    return pl.pallas_call(
        _mix_kernel,
        grid=(m // tm,),
        in_specs=[
            pl.BlockSpec((tm, wa_rows), lambda i: (i, 0)),
            pl.BlockSpec((tm, wb_rows), lambda i: (i, 0)),
            pl.BlockSpec((tm, d), lambda i: (i, 0)),
            pl.BlockSpec((tm, d), lambda i: (i, 1)),
            pl.BlockSpec((tm, d), lambda i: (i, 0)),
            resident((wa_rows, d)),
            resident((wb_rows, d)),
            resident((d, d)),
            pl.BlockSpec((1, d), lambda i: (0, 0)),
        ],
        out_specs=[
            pl.BlockSpec((tm, d), lambda i: (i, 0)),
            pl.BlockSpec((tm, d), lambda i: (i, 0)),
        ],
        out_shape=[
            jax.ShapeDtypeStruct((m, d), F32),
            jax.ShapeDtypeStruct((m, d), BF16),
        ],
        compiler_params=_params(("arbitrary",)),
        name="branch_mix_out_proj",
    )(oa, ob, gates, gates, x2, w_a, w_b, w_o, g_ffn)


def _ffn_kernel(h_ref, x1_ref, wg_ref, wv_ref, cg_ref, cv_ref, bg_ref, bv_ref, wd_ref,
                out_ref, ug_scr, uv_scr, carry_g, carry_v, *, tm, tiles_per_seq):
    i = pl.program_id(0)
    j = pl.program_id(1)
    first = (i % tiles_per_seq) == 0

    @pl.when(jnp.logical_and(i == 0, j == 0))
    def _():
        carry_g[...] = jnp.zeros(carry_g.shape, F32)
        carry_v[...] = jnp.zeros(carry_v.shape, F32)

    rows = tm // FFN_ROW_PARTS

    def up(w_ref, u_scr, r0):
        u_scr[SUBLANES + r0:SUBLANES + r0 + rows, :] = jnp.dot(
            h_ref[r0:r0 + rows, :], w_ref[...], preferred_element_type=F32)

    def conv(c_ref, b_ref, u_scr, r0):
        out = b_ref[...]
        for t in range(CONV_WIDTH):
            shift = CONV_WIDTH - 1 - t
            out = out + c_ref[t:t + 1, :] * u_scr[pl.ds(SUBLANES - shift + r0, rows), :]
        return out

    def chunk(is_first_chunk):
        for u_scr, carry in ((ug_scr, carry_g), (uv_scr, carry_v)):
            prev = carry[j]
            u_scr[0:SUBLANES, :] = jnp.where(first, jnp.zeros_like(prev), prev)
        for r0 in range(0, tm, rows):
            up(wg_ref, ug_scr, r0)
            up(wv_ref, uv_scr, r0)
        carry_g[j] = ug_scr[tm:tm + SUBLANES, :]
        carry_v[j] = uv_scr[tm:tm + SUBLANES, :]
        for r0 in range(0, tm, rows):
            gate = conv(cg_ref, bg_ref, ug_scr, r0)
            val = conv(cv_ref, bv_ref, uv_scr, r0)
            a = (gate * jax.nn.sigmoid(gate) * val).astype(BF16)
            part = slice(r0, r0 + rows)
            base = x1_ref[part, :] if is_first_chunk else out_ref[part, :]
            out_ref[part, :] = base + jnp.dot(a, wd_ref[...], preferred_element_type=F32)

    pl.when(j == 0)(functools.partial(chunk, True))
    pl.when(j > 0)(functools.partial(chunk, False))


def _ffn(h2, x1, w_up, w_conv, b_conv, w_down, *, seq_len):
    m, d = h2.shape
    d_ff = w_down.shape[0]
    tm, tf = 1024, 512
    nj = d_ff // tf
    kern = functools.partial(_ffn_kernel, tm=tm, tiles_per_seq=seq_len // tm)
    return pl.pallas_call(
        kern,
        grid=(m // tm, nj),
        in_specs=[
            pl.BlockSpec((tm, d), lambda i, j: (i, 0)),
            pl.BlockSpec((tm, d), lambda i, j: (i, 0)),
            pl.BlockSpec((d, tf), lambda i, j: (0, j)),
            pl.BlockSpec((d, tf), lambda i, j: (0, nj + j)),
            pl.BlockSpec((CONV_WIDTH, tf), lambda i, j: (0, j)),
            pl.BlockSpec((CONV_WIDTH, tf), lambda i, j: (0, nj + j)),
            pl.BlockSpec((1, tf), lambda i, j: (0, j)),
            pl.BlockSpec((1, tf), lambda i, j: (0, nj + j)),
            pl.BlockSpec((tf, d), lambda i, j: (j, 0)),
        ],
        out_specs=pl.BlockSpec((tm, d), lambda i, j: (i, 0)),
        out_shape=jax.ShapeDtypeStruct((m, d), F32),
        scratch_shapes=[
            pltpu.VMEM((tm + SUBLANES, tf), F32),
            pltpu.VMEM((tm + SUBLANES, tf), F32),
            pltpu.VMEM((nj, SUBLANES, tf), F32),
            pltpu.VMEM((nj, SUBLANES, tf), F32),
        ],
        compiler_params=_params(("arbitrary", "arbitrary"), BIG_VMEM_LIMIT),
        name="conv_ffn",
    )(h2, x1, w_up, w_up, w_conv, w_conv, b_conv, b_conv, w_down)


def _layer(x, g_attn, w_in, b_forget, g_q_fox, g_k_fox, g_q_dil, g_k_dil,
           w_br_fox, w_br_dil, w_out, g_ffn, w_up, w_conv, b_conv, w_down):
    b, s, d = x.shape
    w_fox = N_HEADS_FOX * HEAD_DIM
    w_dil = N_HEADS_DIL * HEAD_DIM
    n_qkv = 3 * w_fox + 3 * w_dil
    f_lo, f_hi = 3 * w_fox, 3 * w_fox + N_HEADS_FOX

    ones = jnp.ones((w_fox,), F32)
    g_qk = jnp.concatenate([g_q_fox.reshape(-1) * QK_SCALE2, g_k_fox.reshape(-1), ones,
                            g_q_dil.reshape(-1) * QK_SCALE2, g_k_dil.reshape(-1), ones]
                           ).reshape(1, n_qkv)
    b_pad = jnp.pad(b_forget, (0, LANES - N_HEADS_FOX)).reshape(1, LANES)
    slopes = jnp.asarray(2.0 ** (-8.0 * np.arange(1, N_HEADS_DIL + 1) / N_HEADS_DIL), dtype=F32)

    x2 = x.reshape(b * s, d)
    qkv, gates, f_pre = _in_proj(x2, g_attn.reshape(1, d), w_in.T, g_qk,
                                 f_lo=f_lo, f_hi=f_hi, n_qkv=n_qkv, n_gate=2 * d)
    qkv = qkv.reshape(b, s, n_qkv)

    ccol, crow = _forget(f_pre.reshape(b, s, LANES), b_pad, n_heads=N_HEADS_FOX)
    crow4 = crow.reshape(b, N_HEADS_FOX, 1, s)
    assert N_HEADS_FOX == N_HEADS_DIL
    o_a, o_b, w_up_b, w_down_b, w_out_b, w_br_fox_b, w_br_dil_b = _attention(
        qkv, ccol, crow4, slopes, [w_up, w_down, w_out, w_br_fox, w_br_dil],
        n_heads=N_HEADS_FOX, tq=256)

    x1, h2 = _mix(o_a.reshape(b * s, w_fox), o_b.reshape(b * s, w_dil), gates, x2,
                  w_br_fox_b, w_br_dil_b, w_out_b, g_ffn.reshape(1, d))
    out = _ffn(h2, x1, w_up_b, w_conv, b_conv.reshape(1, -1), w_down_b, seq_len=s)
    return out.reshape(b, s, d)


def kernel(x, g_attn, w_in, b_forget, g_q_fox, g_k_fox, g_q_dil, g_k_dil, w_br_fox, w_br_dil,
           w_out, g_ffn, w_up, w_conv, b_conv, w_down):
    for l in range(w_in.shape[0]):
        x = _layer(x, g_attn[l], w_in[l], b_forget[l], g_q_fox[l], g_k_fox[l], g_q_dil[l],
                   g_k_dil[l], w_br_fox[l], w_br_dil[l], w_out[l], g_ffn[l], w_up[l], w_conv[l],
                   b_conv[l], w_down[l])
    return x
```

```python
import functools

import numpy as np
import jax
import jax.numpy as jnp
from jax import lax
from jax.experimental import pallas as pl
from jax.experimental.pallas import tpu as pltpu

HEAD_DIM = 128
N_HEADS_FOX = 8
N_HEADS_DIL = 8
DIL_PATTERNS = ((128, 1), (512, 4), (2048, 16))
CONV_WIDTH = 3
PROJ_ROW_PARTS = 8
PROJ_X_PARTS = 8
FFN_ROW_PARTS = 2
EPS = 1e-6
NEG_INF = -1e30
LOG2E = float(np.log2(np.e))
QK_SCALE2 = LOG2E / float(np.sqrt(HEAD_DIM))

F32 = jnp.float32
BF16 = jnp.bfloat16

LANES = 128
SUBLANES = 8
BF16_ROWS = 2 * SUBLANES
VMEM_LIMIT = 56 * 1024 * 1024
BIG_VMEM_LIMIT = 62 * 1024 * 1024

NT_DIMS = (((1,), (1,)), ((), ()))


def _params(sem, vmem_limit=VMEM_LIMIT):
    return pltpu.CompilerParams(dimension_semantics=sem, vmem_limit_bytes=vmem_limit)


def _in_proj_kernel(x_ref, g_ref, w_ref, wf_ref, gqk_ref, qkv_ref, gate_ref, f_ref, h_scr,
                    *, n_qkv_tiles, tiles_per_group, heads_per_tile):
    s = pl.program_id(1)
    j = s - (PROJ_X_PARTS - 1)
    x_rows = x_ref.shape[0]

    def project(h, w_t):
        return lax.dot_general(h, w_t.astype(BF16), NT_DIMS, preferred_element_type=F32)

    @pl.when(s < PROJ_X_PARTS)
    def _():
        rows = pl.ds(pl.multiple_of(s * x_rows, x_rows), x_rows)
        x = x_ref[...]
        ms = jnp.mean(x * x, axis=-1, keepdims=True)
        h = (x * lax.rsqrt(ms + EPS) * g_ref[...]).astype(BF16)
        h_scr[rows, :] = h
        wf = wf_ref[...]
        wf = jnp.concatenate([wf, jnp.zeros((LANES - wf.shape[0], wf.shape[1]), F32)], axis=0)
        f_ref[rows, :] = project(h, wf)

    def store_qk(acc, rows):
        for hh in range(heads_per_tile):
            sl = slice(hh * HEAD_DIM, (hh + 1) * HEAD_DIM)
            a = acc[:, sl]
            ms = jnp.mean(a * a, axis=-1, keepdims=True)
            qkv_ref[rows, sl] = (a * lax.rsqrt(ms + EPS) * gqk_ref[:, sl]).astype(BF16)

    def store_v(acc, rows):
        qkv_ref[rows, :] = acc.astype(BF16)

    def store_gate(acc, rows):
        gate_ref[rows, :] = acc.astype(gate_ref.dtype)

    def tile(store):
        w = w_ref[...].astype(BF16)
        part = h_scr.shape[0] // PROJ_ROW_PARTS
        for r0 in range(0, h_scr.shape[0], part):
            rows = slice(r0, r0 + part)
            store(lax.dot_general(h_scr[rows, :], w, NT_DIMS, preferred_element_type=F32), rows)

    active = j >= 0
    is_qkv = jnp.logical_and(active, j < n_qkv_tiles)
    is_qk = jnp.logical_and(is_qkv, (j % (3 * tiles_per_group)) < 2 * tiles_per_group)
    cases = (
        (is_qk, store_qk),
        (jnp.logical_and(is_qkv, jnp.logical_not(is_qk)), store_v),
        (jnp.logical_and(active, j >= n_qkv_tiles), store_gate),
    )
    for cond, store in cases:
        pl.when(cond)(functools.partial(tile, store))


def _in_proj(x2, g_attn, w_t, g_qk, *, f_lo, f_hi, n_qkv, n_gate):
    m, d = x2.shape
    tm, tn = 4096, 512
    assert f_lo % tn == 0 and f_hi % SUBLANES == 0
    n_qkv_tiles = n_qkv // tn
    n_tiles = (n_qkv + n_gate) // tn
    kern = functools.partial(
        _in_proj_kernel, n_qkv_tiles=n_qkv_tiles,
        tiles_per_group=(N_HEADS_FOX * HEAD_DIM) // tn, heads_per_tile=tn // HEAD_DIM)
    tile = lambda s: jnp.maximum(s - (PROJ_X_PARTS - 1), 0)
    qkv_tile = lambda s: jnp.minimum(tile(s), n_qkv_tiles - 1)
    w_row = lambda j: pl.multiple_of(j * tn + jnp.where(j * tn >= f_lo, f_hi - f_lo, 0), SUBLANES)
    return pl.pallas_call(
        kern,
        grid=(m // tm, n_tiles + PROJ_X_PARTS - 1),
        in_specs=[
            pl.BlockSpec((tm // PROJ_X_PARTS, d),
                         lambda i, s: (PROJ_X_PARTS * i + jnp.minimum(s, PROJ_X_PARTS - 1), 0)),
            pl.BlockSpec((1, d), lambda i, s: (0, 0)),
            pl.BlockSpec((pl.Element(tn), pl.Element(d)), lambda i, s: (w_row(tile(s)), 0)),
            pl.BlockSpec((pl.Element(f_hi - f_lo), pl.Element(d)), lambda i, s: (f_lo, 0)),
            pl.BlockSpec((1, tn), lambda i, s: (0, qkv_tile(s))),
        ],
        out_specs=[
            pl.BlockSpec((tm, tn), lambda i, s: (i, qkv_tile(s))),
            pl.BlockSpec((tm, tn), lambda i, s: (i, jnp.maximum(tile(s) - n_qkv_tiles, 0))),
            pl.BlockSpec((tm, LANES), lambda i, s: (i, 0)),
        ],
        out_shape=[
            jax.ShapeDtypeStruct((m, n_qkv), BF16),
            jax.ShapeDtypeStruct((m, n_gate), BF16),
            jax.ShapeDtypeStruct((m, LANES), F32),
        ],
        scratch_shapes=[pltpu.VMEM((tm, d), BF16)],
        compiler_params=_params(("arbitrary", "arbitrary"), BIG_VMEM_LIMIT),
        name="in_proj",
    )(x2, g_attn, w_t, w_t, g_qk)


def _forget_kernel(f_ref, b_ref, ccol_ref, crow_ref, *, n_heads):
    z = f_ref[0] + b_ref[...]
    c = jax.nn.log_sigmoid(z)
    s = c.shape[0]
    row = lax.broadcasted_iota(jnp.int32, c.shape, 0)
    k = 1
    while k < s:
        c = c + jnp.where(row >= k, pltpu.roll(c, k, axis=0), 0.0)
        k *= 2
    ccol_ref[0] = c
    crow_ref[0] = c.T[:n_heads, :]


def _forget(f_pre, b_pad, *, n_heads):
    b, s, _ = f_pre.shape
    return pl.pallas_call(
        functools.partial(_forget_kernel, n_heads=n_heads),
        grid=(b,),
        in_specs=[
            pl.BlockSpec((1, s, LANES), lambda i: (i, 0, 0)),
            pl.BlockSpec((1, LANES), lambda i: (0, 0)),
        ],
        out_specs=[
            pl.BlockSpec((1, s, LANES), lambda i: (i, 0, 0)),
            pl.BlockSpec((1, n_heads, s), lambda i: (i, 0, 0)),
        ],
        out_shape=[
            jax.ShapeDtypeStruct((b, s, LANES), F32),
            jax.ShapeDtypeStruct((b, n_heads, s), F32),
        ],
        compiler_params=_params(("arbitrary",)),
        name="forget_cumsum",
    )(f_pre, b_pad)


def _causal_sweeps(streams, tq, prepare_values):
    s_len = streams[0][0].shape[1]
    n_q = s_len // tq

    def scores(stream, qi):
        q_ref, k_ref = stream[:2]
        q0, kv = qi * tq, (qi + 1) * tq
        return lax.dot_general(q_ref[0, q0:kv, :], k_ref[0, :kv, :], NT_DIMS,
                               preferred_element_type=F32)

    def biased(stream, qi, s):
        add_bias = stream[4]
        tiles = [add_bias(s[:, kj * tq:(kj + 1) * tq], qi, kj) for kj in range(qi + 1)]
        z = jnp.concatenate(tiles, axis=1) if qi else tiles[0]
        return z, jnp.max(z, axis=1, keepdims=True)

    def weights(z, m):
        return jnp.exp2(z - m).astype(BF16)

    def weighted(stream, qi, p):
        v_aug, o_ref = stream[2:4]
        q0, kv = qi * tq, (qi + 1) * tq
        o = jnp.dot(p, v_aug[0:kv, :], preferred_element_type=F32)
        o_ref[0, q0:kv, :] = (o[:, :HEAD_DIM] / o[:, HEAD_DIM:HEAD_DIM + 1]).astype(o_ref.dtype)

    logit = [biased(st, 0, scores(st, 0)) for st in streams]
    for qi in range(n_q):
        last = qi + 1 == n_q
        raw = [None if last else scores(st, qi + 1) for st in streams]
        if qi == 0:
            prepare_values()
        probs = [weights(*zm) for zm in logit]
        for st, p in zip(streams, probs):
            weighted(st, qi, p)
        logit = [None if last else biased(st, qi + 1, s) for st, s in zip(streams, raw)]


def _fill_v_aug(v_ref, v_aug):
    v_aug[:, :HEAD_DIM] = v_ref[0]
    lane = lax.broadcasted_iota(jnp.int32, (v_aug.shape[0], v_aug.shape[1] - HEAD_DIM), 1)
    v_aug[:, HEAD_DIM:] = jnp.where(lane == 0, 1.0, 0.0).astype(BF16)


def _cast_specs(weights, n_steps, step_index):
    in_specs, out_specs, out_shapes = [], [], []
    for w in weights:
        rows, cols = w.shape[0] // n_steps, w.shape[1]
        assert rows * n_steps == w.shape[0] and rows % BF16_ROWS == 0
        spec = pl.BlockSpec((rows, cols), lambda *idx: (step_index(*idx), 0))
        in_specs.append(spec)
        out_specs.append(spec)
        out_shapes.append(jax.ShapeDtypeStruct(w.shape, BF16))
    return in_specs, out_specs, out_shapes


def _attention_kernel(slopes_ref, qa_ref, ka_ref, va_ref, qb_ref, kb_ref, vb_ref, ccol_ref, crow_ref,
                      *rest, tq):
    n_cast = (len(rest) - 5) // 2
    oa_ref, ob_ref = rest[n_cast], rest[n_cast + 1]
    bias_scr, va_aug, vb_aug = rest[-3:]
    h = pl.program_id(0)
    r = lax.broadcasted_iota(jnp.int32, (tq, tq), 0)
    c = lax.broadcasted_iota(jnp.int32, (tq, tq), 1)

    def build_distance_bias():
        slope = slopes_ref[h]
        for d in range(bias_scr.shape[0]):
            delta = r - c + d * tq
            count = jnp.zeros((tq, tq), F32)
            for window, dil in DIL_PATTERNS:
                assert dil & (dil - 1) == 0
                member = jnp.logical_and((delta & (dil - 1)) == 0, delta <= window)
                count = count + jnp.where(member, 1.0, 0.0)
            valid = jnp.logical_and(delta >= 0, count > 0.0)
            bias = jnp.log(jnp.maximum(count, 1.0)) - slope * delta.astype(F32)
            bias_scr[d] = jnp.where(valid, bias, NEG_INF) * LOG2E

    def step(new_head):
        if new_head:
            build_distance_bias()
        s_len = qa_ref.shape[1]
        lane = lax.broadcasted_iota(jnp.int32, (s_len, LANES), 1)
        cq = jnp.sum(jnp.where(lane == h, ccol_ref[0], 0.0), axis=1, keepdims=True) * LOG2E
        ck = crow_ref[0, 0] * LOG2E
        causal = c <= r

        def forget_bias(tile, qi, kj):
            z = tile + cq[qi * tq:(qi + 1) * tq] - ck[:, kj * tq:(kj + 1) * tq]
            return jnp.where(causal, z, NEG_INF) if kj == qi else z

        def distance_bias(tile, qi, kj):
            return tile + bias_scr[qi - kj]

        def fill_values():
            _fill_v_aug(va_ref, va_aug)
            _fill_v_aug(vb_ref, vb_aug)

        _causal_sweeps([(qa_ref, ka_ref, va_aug, oa_ref, forget_bias),
                        (qb_ref, kb_ref, vb_aug, ob_ref, distance_bias)], tq, fill_values)

        for i_ref, o_ref in zip(rest[:n_cast], rest[n_cast + 2:-3]):
            o_ref[...] = i_ref[...].astype(BF16)

    first_batch = pl.program_id(1) == 0
    pl.when(first_batch)(functools.partial(step, True))
    pl.when(jnp.logical_not(first_batch))(functools.partial(step, False))


def _attention(qkv, ccol, crow4, slopes, cast_weights, *, n_heads, tq):
    b, s, _ = qkv.shape
    spec = lambda group: pl.BlockSpec((1, s, HEAD_DIM),
                                      lambda h, bi, sl: (bi, 0, group * n_heads + h))
    out_spec = pl.BlockSpec((1, s, HEAD_DIM), lambda h, bi, sl: (bi, 0, h))
    c_in, c_out, c_shapes = _cast_specs(cast_weights, b * n_heads, lambda h, bi, sl: h * b + bi)
    o_shape = jax.ShapeDtypeStruct((b, s, n_heads * HEAD_DIM), BF16)
    return pl.pallas_call(
        functools.partial(_attention_kernel, tq=tq),
        grid_spec=pltpu.PrefetchScalarGridSpec(
            num_scalar_prefetch=1,
            grid=(n_heads, b),
            in_specs=[spec(g) for g in range(6)] + [
                pl.BlockSpec((1, s, LANES), lambda h, bi, sl: (bi, 0, 0)),
                pl.BlockSpec((1, 1, 1, s), lambda h, bi, sl: (bi, h, 0, 0)),
            ] + c_in,
            out_specs=[out_spec, out_spec] + c_out,
            scratch_shapes=[pltpu.VMEM((s // tq, tq, tq), F32),
                            pltpu.VMEM((s, 2 * HEAD_DIM), BF16),
                            pltpu.VMEM((s, 2 * HEAD_DIM), BF16)],
        ),
        out_shape=[o_shape, o_shape] + c_shapes,
        compiler_params=_params(("arbitrary", "arbitrary")),
        name="attention",
    )(slopes, *([qkv] * 6), ccol, crow4, *cast_weights)


def _mix_kernel(oa_ref, ob_ref, ga_ref, gb_ref, x_ref, wa_ref, wb_ref, wo_ref, g_ref,
                x1_ref, h2_ref):
    ta = jnp.dot(oa_ref[...], wa_ref[...], preferred_element_type=F32)
    tb = jnp.dot(ob_ref[...], wb_ref[...], preferred_element_type=F32)
    merged = (jax.nn.sigmoid(ga_ref[...].astype(F32)) * ta
              + jax.nn.sigmoid(gb_ref[...].astype(F32)) * tb).astype(BF16)
    x1 = x_ref[...] + jnp.dot(merged, wo_ref[...], preferred_element_type=F32)
    x1_ref[...] = x1
    ms = jnp.mean(x1 * x1, axis=-1, keepdims=True)
    h2_ref[...] = (x1 * lax.rsqrt(ms + EPS) * g_ref[...]).astype(BF16)


def _mix(oa, ob, gates, x2, w_a, w_b, w_o, g_ffn):
    m, d = x2.shape
    wa_rows, wb_rows = w_a.shape[0], w_b.shape[0]
    tm = 512
    resident = lambda shape: pl.BlockSpec(shape, lambda i: (0, 0), pipeline_mode=pl.Buffered(1))
    return pl.pallas_call(
        _mix_kernel,
        grid=(m // tm,),
        in_specs=[
            pl.BlockSpec((tm, wa_rows), lambda i: (i, 0)),
            pl.BlockSpec((tm, wb_rows), lambda i: (i, 0)),
            pl.BlockSpec((tm, d), lambda i: (i, 0)),
            pl.BlockSpec((tm, d), lambda i: (i, 1)),
            pl.BlockSpec((tm, d), lambda i: (i, 0)),
            resident((wa_rows, d)),
            resident((wb_rows, d)),
            resident((d, d)),
            pl.BlockSpec((1, d), lambda i: (0, 0)),
        ],
        out_specs=[
            pl.BlockSpec((tm, d), lambda i: (i, 0)),
            pl.BlockSpec((tm, d), lambda i: (i, 0)),
        ],
        out_shape=[
            jax.ShapeDtypeStruct((m, d), F32),
            jax.ShapeDtypeStruct((m, d), BF16),
        ],
        compiler_params=_params(("arbitrary",)),
        name="branch_mix_out_proj",
    )(oa, ob, gates, gates, x2, w_a, w_b, w_o, g_ffn)


def _ffn_kernel(h_ref, x1_ref, wg_ref, wv_ref, cg_ref, cv_ref, bg_ref, bv_ref, wd_ref,
                out_ref, ug_scr, uv_scr, carry_g, carry_v, *, tm, tiles_per_seq):
    i = pl.program_id(0)
    j = pl.program_id(1)
    first = (i % tiles_per_seq) == 0

    @pl.when(jnp.logical_and(i == 0, j == 0))
    def _():
        carry_g[...] = jnp.zeros(carry_g.shape, F32)
        carry_v[...] = jnp.zeros(carry_v.shape, F32)

    rows = tm // FFN_ROW_PARTS

    def up(w_ref, u_scr, r0):
        u_scr[SUBLANES + r0:SUBLANES + r0 + rows, :] = jnp.dot(
            h_ref[r0:r0 + rows, :], w_ref[...], preferred_element_type=F32)

    def conv(c_ref, b_ref, u_scr, r0):
        out = b_ref[...]
        for t in range(CONV_WIDTH):
            shift = CONV_WIDTH - 1 - t
            out = out + c_ref[t:t + 1, :] * u_scr[pl.ds(SUBLANES - shift + r0, rows), :]
        return out

    def chunk(is_first_chunk):
        for u_scr, carry in ((ug_scr, carry_g), (uv_scr, carry_v)):
            prev = carry[j]
            u_scr[0:SUBLANES, :] = jnp.where(first, jnp.zeros_like(prev), prev)
        for r0 in range(0, tm, rows):
            up(wg_ref, ug_scr, r0)
            up(wv_ref, uv_scr, r0)
        carry_g[j] = ug_scr[tm:tm + SUBLANES, :]
        carry_v[j] = uv_scr[tm:tm + SUBLANES, :]
        for r0 in range(0, tm, rows):
            gate = conv(cg_ref, bg_ref, ug_scr, r0)
            val = conv(cv_ref, bv_ref, uv_scr, r0)
            a = (gate * jax.nn.sigmoid(gate) * val).astype(BF16)
            part = slice(r0, r0 + rows)
            base = x1_ref[part, :] if is_first_chunk else out_ref[part, :]
            out_ref[part, :] = base + jnp.dot(a, wd_ref[...], preferred_element_type=F32)

    pl.when(j == 0)(functools.partial(chunk, True))
    pl.when(j > 0)(functools.partial(chunk, False))


def _ffn(h2, x1, w_up, w_conv, b_conv, w_down, *, seq_len):
    m, d = h2.shape
    d_ff = w_down.shape[0]
    tm, tf = 1024, 512
    nj = d_ff // tf
    kern = functools.partial(_ffn_kernel, tm=tm, tiles_per_seq=seq_len // tm)
    return pl.pallas_call(
        kern,
        grid=(m // tm, nj),
        in_specs=[
            pl.BlockSpec((tm, d), lambda i, j: (i, 0)),
            pl.BlockSpec((tm, d), lambda i, j: (i, 0)),
            pl.BlockSpec((d, tf), lambda i, j: (0, j)),
            pl.BlockSpec((d, tf), lambda i, j: (0, nj + j)),
            pl.BlockSpec((CONV_WIDTH, tf), lambda i, j: (0, j)),
            pl.BlockSpec((CONV_WIDTH, tf), lambda i, j: (0, nj + j)),
            pl.BlockSpec((1, tf), lambda i, j: (0, j)),
            pl.BlockSpec((1, tf), lambda i, j: (0, nj + j)),
            pl.BlockSpec((tf, d), lambda i, j: (j, 0)),
        ],
        out_specs=pl.BlockSpec((tm, d), lambda i, j: (i, 0)),
        out_shape=jax.ShapeDtypeStruct((m, d), F32),
        scratch_shapes=[
            pltpu.VMEM((tm + SUBLANES, tf), F32),
            pltpu.VMEM((tm + SUBLANES, tf), F32),
            pltpu.VMEM((nj, SUBLANES, tf), F32),
            pltpu.VMEM((nj, SUBLANES, tf), F32),
        ],
        compiler_params=_params(("arbitrary", "arbitrary"), BIG_VMEM_LIMIT),
        name="conv_ffn",
    )(h2, x1, w_up, w_up, w_conv, w_conv, b_conv, b_conv, w_down)


def _layer(x, g_attn, w_in, b_forget, g_q_fox, g_k_fox, g_q_dil, g_k_dil,
           w_br_fox, w_br_dil, w_out, g_ffn, w_up, w_conv, b_conv, w_down):
    b, s, d = x.shape
    w_fox = N_HEADS_FOX * HEAD_DIM
    w_dil = N_HEADS_DIL * HEAD_DIM
    n_qkv = 3 * w_fox + 3 * w_dil
    f_lo, f_hi = 3 * w_fox, 3 * w_fox + N_HEADS_FOX

    ones = jnp.ones((w_fox,), F32)
    g_qk = jnp.concatenate([g_q_fox.reshape(-1) * QK_SCALE2, g_k_fox.reshape(-1), ones,
                            g_q_dil.reshape(-1) * QK_SCALE2, g_k_dil.reshape(-1), ones]
                           ).reshape(1, n_qkv)
    b_pad = jnp.pad(b_forget, (0, LANES - N_HEADS_FOX)).reshape(1, LANES)
    slopes = jnp.asarray(2.0 ** (-8.0 * np.arange(1, N_HEADS_DIL + 1) / N_HEADS_DIL), dtype=F32)

    x2 = x.reshape(b * s, d)
    qkv, gates, f_pre = _in_proj(x2, g_attn.reshape(1, d), w_in.T, g_qk,
                                 f_lo=f_lo, f_hi=f_hi, n_qkv=n_qkv, n_gate=2 * d)
    qkv = qkv.reshape(b, s, n_qkv)

    ccol, crow = _forget(f_pre.reshape(b, s, LANES), b_pad, n_heads=N_HEADS_FOX)
    crow4 = crow.reshape(b, N_HEADS_FOX, 1, s)
    assert N_HEADS_FOX == N_HEADS_DIL
    o_a, o_b, w_up_b, w_down_b, w_out_b, w_br_fox_b, w_br_dil_b = _attention(
        qkv, ccol, crow4, slopes, [w_up, w_down, w_out, w_br_fox, w_br_dil],
        n_heads=N_HEADS_FOX, tq=256)

    x1, h2 = _mix(o_a.reshape(b * s, w_fox), o_b.reshape(b * s, w_dil), gates, x2,
                  w_br_fox_b, w_br_dil_b, w_out_b, g_ffn.reshape(1, d))
    out = _ffn(h2, x1, w_up_b, w_conv, b_conv.reshape(1, -1), w_down_b, seq_len=s)
    return out.reshape(b, s, d)


def kernel(x, g_attn, w_in, b_forget, g_q_fox, g_k_fox, g_q_dil, g_k_dil, w_br_fox, w_br_dil,
           w_out, g_ffn, w_up, w_conv, b_conv, w_down):
    for l in range(w_in.shape[0]):
        x = _layer(x, g_attn[l], w_in[l], b_forget[l], g_q_fox[l], g_k_fox[l], g_q_dil[l],
                   g_k_dil[l], w_br_fox[l], w_br_dil[l], w_out[l], g_ffn[l], w_up[l], w_conv[l],
                   b_conv[l], w_down[l])
    return x
```
